```python
import math
import jax
import jax.numpy as jnp
from jax import lax
import numpy as np

D_MODEL = 1024
BATCH = 32
SEQ = 2048
DEPTH = 4

GRID_W = 64
CTX_LEN = 256
N_MIXERS = 4
N_MOD = 9
EPS = 1e-6
ROPE_BASE = 10000.0
Q_BLOCK = 128
NEG_INF = -1e30
D_FF = 2816
POOL_WINDOWS = (2, 4, 8, 16)
POOL_GROUPS = len(POOL_WINDOWS)
POOL_G = D_MODEL // POOL_GROUPS
DIFF_HEADS = 8
DIFF_DH = D_MODEL // (2 * DIFF_HEADS)
MLA_HEADS = 16
MLA_Q_RANK = 384
MLA_KV_RANK = 256
MLA_NOPE = 64
MLA_ROPE = 32
MLA_V = D_MODEL // MLA_HEADS
NA_HEADS = 16
NA_DH = D_MODEL // NA_HEADS
NA_KH = 8
NA_KW = 16

kernel_name = 'hybrid_interleaved_dit_ctx_prefix'


def n_layers_of(m):
    return len(range(m, DEPTH, N_MIXERS))


def rms_norm(x, g):
    xf = x.astype(jnp.float32)
    y = xf * lax.rsqrt(jnp.mean(xf * xf, axis=-1, keepdims=True) + EPS)
    return y.astype(x.dtype) * g


def modulate(x, g, shift, scale):
    return rms_norm(x, g) * (1 + scale) + shift


def swiglu(h, w_in, w_out):
    gate, up = jnp.split(h @ w_in, 2, axis=-1)
    return (jax.nn.silu(gate) * up) @ w_out


def ffn_half_step(xs, md, k, g, w_in, w_out):
    h = modulate(xs, g, md[3 * k], md[3 * k + 1])
    return xs + 0.5 * md[3 * k + 2] * swiglu(h, w_in, w_out)


def axial_rope(n_tok, dim, dtype):
    nf = dim // 4
    inv = ROPE_BASE ** (-jnp.arange(nf, dtype=jnp.float32) / nf)
    t = jnp.arange(n_tok, dtype=jnp.int32)
    pos = jnp.stack([t // GRID_W, t % GRID_W], axis=-1).astype(jnp.float32)
    ang = jnp.broadcast_to(pos[:, :, None, None] * inv, (n_tok, 2, 2, nf)).reshape(n_tok, dim)
    return jnp.cos(ang).astype(dtype), jnp.sin(ang).astype(dtype)


def apply_rope(x, cos, sin):
    n, dim = cos.shape
    nf = dim // 4
    xs = x.reshape(x.shape[:-1] + (2, 2, nf))
    rot = jnp.stack([-xs[..., 1, :], xs[..., 0, :]], axis=-2).reshape(x.shape)
    bshape = (n,) + (1,) * (x.ndim - 3) + (dim,)
    return x * cos.reshape(bshape) + rot * sin.reshape(bshape)


def over_query_blocks(fn, q):
    b, s = q.shape[:2]
    nb = s // Q_BLOCK
    qb = jnp.moveaxis(q.reshape((b, nb, Q_BLOCK) + q.shape[2:]), 1, 0)
    out = lax.map(fn, qb)
    return jnp.moveaxis(out, 0, 1).reshape((b, s) + out.shape[3:])


def softmax_attend(q, k, v, scale):
    s = jnp.einsum('bqhd,bkhd->bhqk', q, k).astype(jnp.float32) * scale
    p = jax.nn.softmax(s, axis=-1).astype(v.dtype)
    return jnp.einsum('bhqk,bkhe->bqhe', p, v)


def pool_mix(h, w, b, scale):
    bsz, n, _ = h.shape
    hf = h.astype(jnp.float32).reshape(bsz, n, POOL_GROUPS, POOL_G)
    cs = jnp.concatenate([jnp.zeros_like(hf[:, :1]), jnp.cumsum(hf, axis=1)], axis=1)
    t = jnp.arange(n)[:, None]
    win = jnp.array(POOL_WINDOWS, dtype=jnp.int32)[None, :]
    lo = jnp.clip(t - win // 2, 0, n)
    hi = jnp.clip(t - win // 2 + win, 0, n)
    grp = jnp.arange(POOL_GROUPS)[None, :]
    mean = (cs[:, hi, grp] - cs[:, lo, grp]) / (hi - lo).astype(jnp.float32)[None, :, :, None]
    d = (mean - hf).astype(h.dtype)
    y = jnp.einsum('bngi,gio->bngo', d, w) + b
    return y.reshape(h.shape) * scale


def diff_project(h, w_qkv, qk_g):
    b, n, _ = h.shape
    qkv = (h @ w_qkv).reshape(b, n, 3, DIFF_HEADS, 2, DIFF_DH)
    q = rms_norm(qkv[:, :, 0], qk_g[0])
    k = rms_norm(qkv[:, :, 1], qk_g[1])
    v = qkv[:, :, 2].reshape(b, n, DIFF_HEADS, 2 * DIFF_DH)
    return q, k, v


def diff_attend(q, k, v, lam):
    s = jnp.einsum('bqhcd,bkhcd->bhcqk', q, k).astype(jnp.float32) * (DIFF_DH ** -0.5)
    p = jax.nn.softmax(s, axis=-1)
    a = (p[:, :, 0] - lam * p[:, :, 1]).astype(v.dtype)
    return jnp.einsum('bhqk,bkhe->bqhe', a, v)


def diff_mixer(hl, hc, w_qkv, qk_g, lam_p, sub_g, w_o, lam_init, want_ctx):
    ql, kl, vl = diff_project(hl, w_qkv, qk_g)
    qc, kc, vc = diff_project(hc, w_qkv, qk_g)
    cos, sin = axial_rope(hl.shape[1], DIFF_DH, hl.dtype)
    ql, kl = apply_rope(ql, cos, sin), apply_rope(kl, cos, sin)
    lp = lam_p.astype(jnp.float32)
    lam = jnp.exp(jnp.sum(lp[0] * lp[1])) - jnp.exp(jnp.sum(lp[2] * lp[3])) + lam_init
    k_all = jnp.concatenate([kl, kc], axis=1)
    v_all = jnp.concatenate([vl, vc], axis=1)

    def finish(o):
        o = rms_norm(o, sub_g) * (1.0 - lam_init)
        return o.reshape(o.shape[0], o.shape[1], D_MODEL) @ w_o

    yl = finish(over_query_blocks(lambda qb: diff_attend(qb, k_all, v_all, lam), ql))
    yc = finish(diff_attend(qc, kc, vc, lam)) if want_ctx else None
    return yl, yc


def mla_project(h, w_dq, q_g, w_uq, w_dkv, kv_g, w_ukv, qk_g):
    b, n, _ = h.shape
    q = (rms_norm(h @ w_dq, q_g) @ w_uq).reshape(b, n, MLA_HEADS, MLA_NOPE + MLA_ROPE)
    q = jnp.concatenate([rms_norm(q[..., :MLA_NOPE], qk_g[0, :MLA_NOPE]),
                         rms_norm(q[..., MLA_NOPE:], qk_g[0, MLA_NOPE:])], axis=-1)
    ckv, k_pe = jnp.split(h @ w_dkv, [MLA_KV_RANK], axis=-1)
    kv = (rms_norm(ckv, kv_g) @ w_ukv).reshape(b, n, MLA_HEADS, MLA_NOPE + MLA_V)
    k_nope = rms_norm(kv[..., :MLA_NOPE], qk_g[1, :MLA_NOPE])
    k_pe = rms_norm(k_pe, qk_g[1, MLA_NOPE:])
    return q, k_nope, k_pe, kv[..., MLA_NOPE:]


def mla_attend(q, k_nope, k_pe, v):
    s = (jnp.einsum('bqhd,bkhd->bhqk', q[..., :MLA_NOPE], k_nope)
         + jnp.einsum('bqhr,bkr->bhqk', q[..., MLA_NOPE:], k_pe)).astype(jnp.float32)
    p = jax.nn.softmax(s * ((MLA_NOPE + MLA_ROPE) ** -0.5), axis=-1).astype(v.dtype)
    return jnp.einsum('bhqk,bkhe->bqhe', p, v)


def mla_mixer(hl, hc, w_dq, q_g, w_uq, w_dkv, kv_g, w_ukv, qk_g, w_o, want_ctx):
    ql, kl_nope, kl_pe, vl = mla_project(hl, w_dq, q_g, w_uq, w_dkv, kv_g, w_ukv, qk_g)
    qc, kc_nope, kc_pe, vc = mla_project(hc, w_dq, q_g, w_uq, w_dkv, kv_g, w_ukv, qk_g)
    cos, sin = axial_rope(hl.shape[1], MLA_ROPE, hl.dtype)
    ql = jnp.concatenate([ql[..., :MLA_NOPE], apply_rope(ql[..., MLA_NOPE:], cos, sin)], axis=-1)
    kl_pe = apply_rope(kl_pe, cos, sin)
    k_nope = jnp.concatenate([kl_nope, kc_nope], axis=1)
    k_pe = jnp.concatenate([kl_pe, kc_pe], axis=1)
    v = jnp.concatenate([vl, vc], axis=1)
    b, n = hl.shape[:2]
    ol = over_query_blocks(lambda qb: mla_attend(qb, k_nope, k_pe, v), ql)
    yl = ol.reshape(b, n, MLA_HEADS * MLA_V) @ w_o
    yc = None
    if want_ctx:
        yc = mla_attend(qc, kc_nope, kc_pe, vc).reshape(b, hc.shape[1], MLA_HEADS * MLA_V) @ w_o
    return yl, yc


def na_project(h, w_qkv, qk_g):
    b, n, _ = h.shape
    qkv = (h @ w_qkv).reshape(b, n, 3, NA_HEADS, NA_DH)
    return rms_norm(qkv[:, :, 0], qk_g[0]), rms_norm(qkv[:, :, 1], qk_g[1]), qkv[:, :, 2]


def na_mixer(hl, hc, w_qkv, qk_g, rpb, w_o, want_ctx):
    ql, kl, vl = na_project(hl, w_qkv, qk_g)
    qc, kc, vc = na_project(hc, w_qkv, qk_g)
    b, n = hl.shape[:2]
    rows = n // GRID_W
    kh = min(NA_KH, rows)
    scale = NA_DH ** -0.5
    qg = ql.reshape(b, rows, GRID_W, NA_HEADS, NA_DH)
    kg = kl.reshape(b, rows, GRID_W, NA_HEADS, NA_DH)
    vg = vl.reshape(b, rows, GRID_W, NA_HEADS, NA_DH)
    col = jnp.arange(GRID_W)
    col_start = jnp.clip(col - NA_KW // 2, 0, GRID_W - NA_KW)
    col_mask = (col[None, :] >= col_start[:, None]) & (col[None, :] < col_start[:, None] + NA_KW)
    col_idx = jnp.clip(col[None, :] - col[:, None] + NA_KW - 1, 0, 2 * NA_KW - 2)

    def row_block(r):
        rs = jnp.clip(r - kh // 2, 0, rows - kh)
        q_r = lax.dynamic_index_in_dim(qg, r, axis=1, keepdims=False)
        k_r = lax.dynamic_slice_in_dim(kg, rs, kh, axis=1)
        v_r = lax.dynamic_slice_in_dim(vg, rs, kh, axis=1).reshape(b, kh * GRID_W, NA_HEADS, NA_DH)
        s_loc = jnp.einsum('bqhd,bikhd->bhqik', q_r, k_r).astype(jnp.float32) * scale
        row_idx = rs + jnp.arange(kh) - r + NA_KH - 1
        bias = rpb[:, row_idx[None, :, None], col_idx[:, None, :]]
        s_loc = jnp.where(col_mask[:, None, :], s_loc + bias.astype(jnp.float32), NEG_INF)
        s_loc = s_loc.reshape(b, NA_HEADS, GRID_W, kh * GRID_W)
        s_ctx = jnp.einsum('bqhd,bkhd->bhqk', q_r, kc).astype(jnp.float32) * scale
        p = jax.nn.softmax(jnp.concatenate([s_loc, s_ctx], axis=-1), axis=-1).astype(vl.dtype)
        return (jnp.einsum('bhqk,bkhe->bqhe', p[..., :kh * GRID_W], v_r)
                + jnp.einsum('bhqk,bkhe->bqhe', p[..., kh * GRID_W:], vc))

    out = lax.map(row_block, jnp.arange(rows, dtype=jnp.int32))
    yl = jnp.moveaxis(out, 0, 1).reshape(b, n, D_MODEL) @ w_o
    yc = None
    if want_ctx:
        yc = softmax_attend(qc, kc, vc, scale).reshape(b, hc.shape[1], D_MODEL) @ w_o
    return yl, yc


def setup_inputs(seed: int = 0) -> dict:
    key = jax.random.key(seed)
    ks = iter(jax.random.split(key, 40))
    D = D_MODEL

    def nrm(shape, s):
        return jax.random.normal(next(ks), shape, jnp.float32) * s

    def gain(shape):
        return 1.0 + nrm(shape, 0.05)

    nA, nB, nC, nD = (n_layers_of(m) for m in range(N_MIXERS))
    return {
        'x': nrm((BATCH, SEQ, D), 1.0),
        'c': nrm((BATCH, D), 1.0),
        'ctx': nrm((BATCH, CTX_LEN, D), 1.0),
        'c_ctx': nrm((D,), 1.0),
        'ada_w': nrm((DEPTH, D, N_MOD * D), 0.5 * D ** -0.5),
        'ada_b': nrm((DEPTH, N_MOD * D), 0.02),
        'norm_g': gain((DEPTH, 3, D)),
        'ffn_w_in': nrm((DEPTH, 2, D, 2 * D_FF), D ** -0.5),
        'ffn_w_out': nrm((DEPTH, 2, D_FF, D), D_FF ** -0.5),
        'pool_w': nrm((nA, POOL_GROUPS, POOL_G, POOL_G), POOL_G ** -0.5),
        'pool_b': nrm((nA, POOL_GROUPS, POOL_G), 0.02),
        'pool_scale': gain((nA, D)),
        'diff_w_qkv': nrm((nB, D, 3 * D), D ** -0.5),
        'diff_qk_g': gain((nB, 2, DIFF_DH)),
        'diff_lambda': nrm((nB, 4, DIFF_DH), 0.1),
        'diff_sub_g': gain((nB, 2 * DIFF_DH)),
        'diff_w_o': nrm((nB, D, D), D ** -0.5),
        'mla_w_dq': nrm((nC, D, MLA_Q_RANK), D ** -0.5),
        'mla_q_g': gain((nC, MLA_Q_RANK)),
        'mla_w_uq': nrm((nC, MLA_Q_RANK, MLA_HEADS * (MLA_NOPE + MLA_ROPE)), MLA_Q_RANK ** -0.5),
        'mla_w_dkv': nrm((nC, D, MLA_KV_RANK + MLA_ROPE), D ** -0.5),
        'mla_kv_g': gain((nC, MLA_KV_RANK)),
        'mla_w_ukv': nrm((nC, MLA_KV_RANK, MLA_HEADS * (MLA_NOPE + MLA_V)), MLA_KV_RANK ** -0.5),
        'mla_qk_g': gain((nC, 2, MLA_NOPE + MLA_ROPE)),
        'mla_w_o': nrm((nC, MLA_HEADS * MLA_V, D), (MLA_HEADS * MLA_V) ** -0.5),
        'na_w_qkv': nrm((nD, D, 3 * D), D ** -0.5),
        'na_qk_g': gain((nD, 2, NA_DH)),
        'na_rpb': nrm((nD, NA_HEADS, 2 * NA_KH - 1, 2 * NA_KW - 1), 0.5),
        'na_w_o': nrm((nD, D, D), D ** -0.5),
    }


def reference(x, c, ctx, c_ctx, ada_w, ada_b, norm_g, ffn_w_in, ffn_w_out,
              pool_w, pool_b, pool_scale,
              diff_w_qkv, diff_qk_g, diff_lambda, diff_sub_g, diff_w_o,
              mla_w_dq, mla_q_g, mla_w_uq, mla_w_dkv, mla_kv_g, mla_w_ukv, mla_qk_g, mla_w_o,
              na_w_qkv, na_qk_g, na_rpb, na_w_o):
    xl, xc = x, ctx
    for i in range(DEPTH):
        m, j = i % N_MIXERS, i // N_MIXERS
        want_ctx = i != DEPTH - 1
        ctx_needed = want_ctx or m != 0
        mod_l = jnp.moveaxis((jax.nn.silu(c) @ ada_w[i] + ada_b[i]).reshape(-1, N_MOD, 1, D_MODEL), 1, 0)
        xl = ffn_half_step(xl, mod_l, 0, norm_g[i, 0], ffn_w_in[i, 0], ffn_w_out[i, 0])
        hl = modulate(xl, norm_g[i, 1], mod_l[3], mod_l[4])
        hc = None
        if ctx_needed:
            mod_c = (jax.nn.silu(c_ctx) @ ada_w[i] + ada_b[i]).reshape(N_MOD, 1, D_MODEL)
            xc = ffn_half_step(xc, mod_c, 0, norm_g[i, 0], ffn_w_in[i, 0], ffn_w_out[i, 0])
            hc = modulate(xc, norm_g[i, 1], mod_c[3], mod_c[4])
        if m == 0:
            yl = pool_mix(hl, pool_w[j], pool_b[j], pool_scale[j])
            yc = pool_mix(hc, pool_w[j], pool_b[j], pool_scale[j]) if want_ctx else None
        elif m == 1:
            yl, yc = diff_mixer(hl, hc, diff_w_qkv[j], diff_qk_g[j], diff_lambda[j], diff_sub_g[j],
                                diff_w_o[j], 0.8 - 0.6 * math.exp(-0.3 * i), want_ctx)
        elif m == 2:
            yl, yc = mla_mixer(hl, hc, mla_w_dq[j], mla_q_g[j], mla_w_uq[j], mla_w_dkv[j], mla_kv_g[j],
                               mla_w_ukv[j], mla_qk_g[j], mla_w_o[j], want_ctx)
        else:
            yl, yc = na_mixer(hl, hc, na_w_qkv[j], na_qk_g[j], na_rpb[j], na_w_o[j], want_ctx)
        xl = xl + mod_l[5] * yl
        xl = ffn_half_step(xl, mod_l, 2, norm_g[i, 2], ffn_w_in[i, 1], ffn_w_out[i, 1])
        if want_ctx:
            xc = xc + mod_c[5] * yc
            xc = ffn_half_step(xc, mod_c, 2, norm_g[i, 2], ffn_w_in[i, 1], ffn_w_out[i, 1])
    return xl
```

```python
import functools
import math

import jax
import jax.numpy as jnp
from jax import lax
from jax.experimental import pallas as pl
from jax.experimental.pallas import tpu as pltpu

D_MODEL = 1024
DEPTH = 4
GRID_W = 64
N_MIXERS = 4
N_MOD = 9
EPS = 1e-6
ROPE_BASE = 10000.0
NEG_INF = -1e30
D_FF = 2816
POOL_WINDOWS = (2, 4, 8, 16)
POOL_G = D_MODEL // len(POOL_WINDOWS)
DIFF_HEADS = 8
DIFF_DH = 64
MLA_HEADS = 16
MLA_Q_RANK = 384
MLA_KV_RANK = 256
MLA_NOPE = 64
MLA_ROPE = 32
MLA_V = 64
NA_HEADS = 16
NA_DH = 64
NA_KH = 8
NA_KW = 16

LANES = 128
HALO = 8
TOKEN_TILE = 256
VMEM_LIMIT = 56 * 2**20

F32 = jnp.float32
BF16 = jnp.bfloat16


def _params(n_axes):
    return pltpu.CompilerParams(dimension_semantics=("arbitrary",) * n_axes,
                                vmem_limit_bytes=VMEM_LIMIT)


def _resident(shape):
    zeros = (0,) * len(shape)
    return pl.BlockSpec(shape, lambda *_: zeros, pipeline_mode=pl.Buffered(1))


def _dot(a, b):
    return jnp.dot(a, b, preferred_element_type=F32)


def _dot_t(a, b):
    return lax.dot_general(a, b, (((1,), (1,)), ((), ())), preferred_element_type=F32)


def _modulate(x, g, shift, scale):
    ms = jnp.mean(x * x, axis=-1, keepdims=True)
    return (x * lax.rsqrt(ms + EPS)) * (g * (1.0 + scale)) + shift


def _silu(x):
    return x * (1.0 / (1.0 + jnp.exp(-x)))


def _adaln_kernel(c_ref, w_ref, b_ref, o_ref):
    s = _silu(c_ref[...]).astype(BF16)
    o_ref[...] = _dot(s, w_ref[...].astype(BF16)) + b_ref[...]


def _adaln(cc, ada_w, ada_b):
    rows = cc.shape[0]
    n_col = N_MOD * D_MODEL // D_MODEL
    out = pl.pallas_call(
        _adaln_kernel,
        grid=(DEPTH, n_col),
        in_specs=[pl.BlockSpec((rows, D_MODEL), lambda i, n: (0, 0)),
                  pl.BlockSpec((None, D_MODEL, D_MODEL), lambda i, n: (i, 0, n)),
                  pl.BlockSpec((None, 1, D_MODEL), lambda i, n: (i, 0, n))],
        out_specs=pl.BlockSpec((None, rows, D_MODEL), lambda i, n: (i, 0, n)),
        out_shape=jax.ShapeDtypeStruct((DEPTH, rows, N_MOD * D_MODEL), F32),
        compiler_params=_params(2),
        name="adaln",
    )(cc, ada_w, ada_b.reshape(DEPTH, 1, N_MOD * D_MODEL))
    return out.reshape(DEPTH, rows, N_MOD, D_MODEL)


def _ffn_core(x, mod_ref, k, g_ref, win_ref, wout_ref):
    shift = mod_ref[3 * k:3 * k + 1, :]
    scale = mod_ref[3 * k + 1:3 * k + 2, :]
    gate = mod_ref[3 * k + 2:3 * k + 3, :]
    h = _modulate(x, g_ref[...], shift, scale).astype(BF16)
    u = _dot(h, win_ref[...])
    a = (_silu(u[:, :D_FF]) * u[:, D_FF:]).astype(BF16)
    return x + (0.5 * gate) * _dot(a, wout_ref[...])


def _ffn_kernel(x_ref, mod_ref, g_ref, win_ref, wout_ref, o_ref, *, k):
    o_ref[...] = _ffn_core(x_ref[...], mod_ref, k, g_ref, win_ref, wout_ref)


def _mixout_ffn_kernel(x_ref, a_ref, mod_ref, g_ref, wo_ref, win_ref, wout_ref, o_ref):
    x = x_ref[...] + mod_ref[5:6, :] * _dot(a_ref[...], wo_ref[...])
    o_ref[...] = _ffn_core(x, mod_ref, 2, g_ref, win_ref, wout_ref)


class _Stream:
    def __init__(self, batch, seq, ctx):
        assert seq % TOKEN_TILE == 0 and ctx == TOKEN_TILE and seq % GRID_W == 0
        self.batch, self.seq, self.ctx = batch, seq, ctx
        self.tokens = seq + ctx
        self.lat_tiles = seq // TOKEN_TILE
        self.tiles = self.tokens // TOKEN_TILE

    def tile_spec(self, width):
        return pl.BlockSpec((None, TOKEN_TILE, width), lambda b, j: (b, j, 0))

    def mod_spec(self):
        lat, ctx_row = self.lat_tiles, self.batch
        return pl.BlockSpec((None, N_MOD, D_MODEL),
                            lambda b, j: (jnp.where(j < lat, b, ctx_row), 0, 0))


def _ffn(st, xs, mod, k, g, w_in, w_out):
    return pl.pallas_call(
        functools.partial(_ffn_kernel, k=k),
        grid=(st.batch, st.tiles),
        in_specs=[st.tile_spec(D_MODEL), st.mod_spec(), _resident((1, D_MODEL)),
                  _resident((D_MODEL, 2 * D_FF)), _resident((D_FF, D_MODEL))],
        out_specs=st.tile_spec(D_MODEL),
        out_shape=jax.ShapeDtypeStruct((st.batch, st.tokens, D_MODEL), F32),
        compiler_params=_params(2),
        name="ffn",
    )(xs, mod, g, w_in, w_out)


def _mixout_ffn(st, xs, attn, mod, g, w_o, w_in, w_out, latent_only):
    tiles = st.lat_tiles if latent_only else st.tiles
    tokens = st.seq if latent_only else st.tokens
    return pl.pallas_call(
        _mixout_ffn_kernel,
        grid=(st.batch, tiles),
        in_specs=[st.tile_spec(D_MODEL), st.tile_spec(D_MODEL), st.mod_spec(),
                  _resident((1, D_MODEL)), _resident((D_MODEL, D_MODEL)),
                  _resident((D_MODEL, 2 * D_FF)), _resident((D_FF, D_MODEL))],
        out_specs=st.tile_spec(D_MODEL),
        out_shape=jax.ShapeDtypeStruct((st.batch, tokens, D_MODEL), F32),
        compiler_params=_params(2),
        name="mixout_ffn",
    )(xs, attn, mod, g, w_o, w_in, w_out)


def _pool_kernel(x_ref, xp_ref, xn_ref, mod_ref, g_ref, w_ref, b_ref, s_ref, o_ref, hext_ref,
                 *, lat_tiles, tiles, seq, ctx):
    j = pl.program_id(1)
    g, shift, scale = g_ref[...], mod_ref[3:4, :], mod_ref[4:5, :]
    x = x_ref[...]
    h = _modulate(x, g, shift, scale)
    has_prev = jnp.logical_and(j != 0, j != lat_tiles)
    has_next = jnp.logical_and(j != lat_tiles - 1, j != tiles - 1)
    hext_ref[0:HALO, :] = jnp.where(has_prev, _modulate(xp_ref[...], g, shift, scale), 0.0)
    hext_ref[HALO:HALO + TOKEN_TILE, :] = h
    hext_ref[HALO + TOKEN_TILE:, :] = jnp.where(has_next, _modulate(xn_ref[...], g, shift, scale), 0.0)
    in_lat = j < lat_tiles
    t = (j - jnp.where(in_lat, 0, lat_tiles)) * TOKEN_TILE + lax.broadcasted_iota(jnp.int32, (TOKEN_TILE, 1), 0)
    n = jnp.where(in_lat, seq, ctx)
    for gi, win in enumerate(POOL_WINDOWS):
        cols = slice(gi * POOL_G, (gi + 1) * POOL_G)
        acc = hext_ref[HALO - win // 2:HALO - win // 2 + TOKEN_TILE, cols]
        for k in range(1 - win // 2, win - win // 2):
            acc = acc + hext_ref[HALO + k:HALO + k + TOKEN_TILE, cols]
        lo = jnp.maximum(t - win // 2, 0)
        hi = jnp.minimum(t - win // 2 + win, n)
        mean = acc / (hi - lo).astype(F32)
        d = (mean - h[:, cols]).astype(BF16)
        y = (_dot(d, w_ref[gi]) + b_ref[:, cols]) * s_ref[:, cols]
        o_ref[:, cols] = x[:, cols] + mod_ref[5:6, cols] * y


def _pool(st, xs, mod, g, w, b, scale):
    per_tile = TOKEN_TILE // HALO
    last = st.tokens // HALO - 1
    kern = functools.partial(_pool_kernel, lat_tiles=st.lat_tiles, tiles=st.tiles, seq=st.seq, ctx=st.ctx)
    return pl.pallas_call(
        kern,
        grid=(st.batch, st.tiles),
        in_specs=[st.tile_spec(D_MODEL),
                  pl.BlockSpec((None, HALO, D_MODEL), lambda b_, j: (b_, jnp.maximum(j * per_tile - 1, 0), 0)),
                  pl.BlockSpec((None, HALO, D_MODEL), lambda b_, j: (b_, jnp.minimum((j + 1) * per_tile, last), 0)),
                  st.mod_spec(), _resident((1, D_MODEL)),
                  _resident((len(POOL_WINDOWS), POOL_G, POOL_G)),
                  _resident((1, D_MODEL)), _resident((1, D_MODEL))],
        out_specs=st.tile_spec(D_MODEL),
        out_shape=jax.ShapeDtypeStruct((st.batch, st.tokens, D_MODEL), F32),
        scratch_shapes=[pltpu.VMEM((TOKEN_TILE + 2 * HALO, D_MODEL), F32)],
        compiler_params=_params(2),
        name="pool",
    )(xs, xs, xs, mod, g, w, b, scale)


def _lane_mask(lo, hi):
    lane = lax.broadcasted_iota(jnp.int32, (1, LANES), 1)
    return jnp.logical_and(lane >= lo, lane < hi)


def _segment_rsqrt(x, segments):
    sq = x * x
    r = None
    for lo, hi in segments:
        m = _lane_mask(lo, hi)
        ms = jnp.sum(jnp.where(m, sq, 0.0), axis=-1, keepdims=True) * (1.0 / (hi - lo))
        rs = lax.rsqrt(ms + EPS)
        r = rs if r is None else jnp.where(m, rs, r)
    return r


def _rotate(y, cos, sin_up, sin_dn, shift):
    return (y * cos + pltpu.roll(y, LANES - shift, axis=1) * sin_up
            + pltpu.roll(y, shift, axis=1) * sin_dn)


def _rope_tables(st, dim, lane_lo, period):
    nf = dim // 4
    inv = ROPE_BASE ** (-jnp.arange(nf, dtype=F32) / nf)
    t = jnp.arange(st.seq, dtype=jnp.int32)
    pos = jnp.stack([t // GRID_W, t % GRID_W], axis=-1).astype(F32)
    ang = jnp.broadcast_to(pos[:, :, None, None] * inv, (st.seq, 2, 2, nf)).reshape(st.seq, dim)
    cos, sin = jnp.cos(ang), jnp.sin(ang)
    first_half = (jnp.arange(dim) % (2 * nf)) < nf
    sin_up = jnp.where(first_half, -sin, 0.0)
    sin_dn = jnp.where(first_half, 0.0, sin)

    def place(tab, fill):
        blk = jnp.full((st.seq, period), fill, F32).at[:, lane_lo:lane_lo + dim].set(tab)
        blk = jnp.tile(blk, (1, LANES // period))
        return jnp.concatenate([blk, jnp.full((st.ctx, LANES), fill, F32)], axis=0)

    return place(cos, 1.0), place(sin_up, 0.0), place(sin_dn, 0.0)


def _qkv_pre_kernel(*refs, rope):
    if rope:
        x_ref, mod_ref, g_ref, w_ref, qkg_ref, cos_ref, su_ref, sd_ref, q_ref, k_ref, v_ref = refs
    else:
        x_ref, mod_ref, g_ref, w_ref, qkg_ref, q_ref, k_ref, v_ref = refs
    h = _modulate(x_ref[...], g_ref[...], mod_ref[3:4, :], mod_ref[4:5, :]).astype(BF16)
    qkv = _dot(h, w_ref[...])
    dh = DIFF_DH
    segments = ((0, dh), (dh, 2 * dh))
    for t, (dst, post) in enumerate(((q_ref, dh ** -0.5), (k_ref, 1.0))):
        gain = qkg_ref[t:t + 1, :]
        for c in range(D_MODEL // LANES):
            xb = qkv[:, t * D_MODEL + c * LANES:t * D_MODEL + (c + 1) * LANES]
            y = xb * _segment_rsqrt(xb, segments) * gain
            if rope:
                y = _rotate(y, cos_ref[...], su_ref[...], sd_ref[...], dh // 4)
            dst[:, c * LANES:(c + 1) * LANES] = (y * post).astype(BF16)
    v_ref[...] = qkv[:, 2 * D_MODEL:].astype(BF16)


def _qkv_pre(st, xs, mod, g, w_qkv, qk_gain, rope_tabs):
    rope = rope_tabs is not None
    tab_spec = pl.BlockSpec((TOKEN_TILE, LANES), lambda b, j: (j, 0))
    in_specs = [st.tile_spec(D_MODEL), st.mod_spec(), _resident((1, D_MODEL)),
                _resident((D_MODEL, 3 * D_MODEL)), _resident((2, LANES))]
    args = [xs, mod, g, w_qkv, jnp.tile(qk_gain, (1, 2))]
    if rope:
        in_specs += [tab_spec] * 3
        args += list(rope_tabs)
    shape = jax.ShapeDtypeStruct((st.batch, st.tokens, D_MODEL), BF16)
    return pl.pallas_call(
        functools.partial(_qkv_pre_kernel, rope=rope),
        grid=(st.batch, st.tiles),
        in_specs=in_specs,
        out_specs=[st.tile_spec(D_MODEL)] * 3,
        out_shape=[shape] * 3,
        compiler_params=_params(2),
        name="qkv_pre",
    )(*args)


def _diff_attn_kernel(q_ref, k_ref, v_ref, lam_ref, subg_ref, o_ref, *, lat_tiles, seq, lam_init):
    lp = lam_ref[...]
    lam = (jnp.exp(jnp.sum(lp[0:1] * lp[1:2], axis=-1, keepdims=True))
           - jnp.exp(jnp.sum(lp[2:3] * lp[3:4], axis=-1, keepdims=True)) + lam_init)

    def attend(k, v):
        q = q_ref[...]
        probs = []
        for c in range(2):
            qc = jnp.where(_lane_mask(c * DIFF_DH, (c + 1) * DIFF_DH), q, jnp.zeros_like(q))
            s = _dot_t(qc, k)
            p = jnp.exp(s - jnp.max(s, axis=-1, keepdims=True))
            probs.append((p, 1.0 / jnp.sum(p, axis=-1, keepdims=True)))
        (p0, r0), (p1, r1) = probs
        a = (p0 * r0 - p1 * (lam * r1)).astype(BF16)
        o = _dot(a, v)
        ms = jnp.mean(o * o, axis=-1, keepdims=True)
        o_ref[...] = ((o * lax.rsqrt(ms + EPS)) * subg_ref[...] * (1.0 - lam_init)).astype(BF16)

    j = pl.program_id(2)

    @pl.when(j < lat_tiles)
    def _():
        attend(k_ref[...], v_ref[...])

    @pl.when(j >= lat_tiles)
    def _():
        attend(k_ref[seq:, :], v_ref[seq:, :])


def _diff_attn(st, q, k, v, lam_p, sub_g, lam_init):
    head = lambda b, h, j: (b, j, h)
    keys = lambda b, h, j: (b, 0, h)
    kern = functools.partial(_diff_attn_kernel, lat_tiles=st.lat_tiles, seq=st.seq, lam_init=lam_init)
    return pl.pallas_call(
        kern,
        grid=(st.batch, DIFF_HEADS, st.tiles),
        in_specs=[pl.BlockSpec((None, TOKEN_TILE, LANES), head),
                  pl.BlockSpec((None, st.tokens, LANES), keys),
                  pl.BlockSpec((None, st.tokens, LANES), keys),
                  _resident((4, DIFF_DH)), _resident((1, LANES))],
        out_specs=pl.BlockSpec((None, TOKEN_TILE, LANES), head),
        out_shape=jax.ShapeDtypeStruct((st.batch, st.tokens, D_MODEL), BF16),
        compiler_params=_params(3),
        name="diff_attn",
    )(q, k, v, lam_p, sub_g.reshape(1, LANES))


MLA_SLOT_SEGMENTS = ((0, MLA_NOPE), (MLA_NOPE, MLA_NOPE + MLA_ROPE))
MLA_DOWN = MLA_Q_RANK + MLA_KV_RANK + LANES


def _mla_pre_kernel(x_ref, mod_ref, g_ref, wd_ref, qg_ref, kvg_ref, wuq_ref, wukv_ref, qkg_ref,
                    cos_ref, su_ref, sd_ref, q_ref, k_ref, v_ref):
    h = _modulate(x_ref[...], g_ref[...], mod_ref[3:4, :], mod_ref[4:5, :]).astype(BF16)
    down = _dot(h, wd_ref[...])
    cos, su, sd = cos_ref[...], su_ref[...], sd_ref[...]

    def full_norm(z, gain):
        ms = jnp.mean(z * z, axis=-1, keepdims=True)
        return ((z * lax.rsqrt(ms + EPS)) * gain).astype(BF16)

    q = _dot(full_norm(down[:, :MLA_Q_RANK], qg_ref[...]), wuq_ref[...])
    kv = _dot(full_norm(down[:, MLA_Q_RANK:MLA_Q_RANK + MLA_KV_RANK], kvg_ref[...]), wukv_ref[...])
    q_gain, k_gain = qkg_ref[0:1, :], qkg_ref[1:2, :]
    scale = (MLA_NOPE + MLA_ROPE) ** -0.5
    kpe = down[:, MLA_Q_RANK + MLA_KV_RANK:]
    kpe = kpe * _segment_rsqrt(kpe, MLA_SLOT_SEGMENTS) * k_gain
    kpe = _rotate(kpe, cos, su, sd, MLA_ROPE // 4)
    rope_lanes = _lane_mask(MLA_NOPE, MLA_NOPE + MLA_ROPE)
    for hd in range(MLA_HEADS):
        cols = slice(hd * LANES, (hd + 1) * LANES)
        qb = q[:, cols]
        qb = qb * _segment_rsqrt(qb, MLA_SLOT_SEGMENTS) * q_gain
        q_ref[:, cols] = (_rotate(qb, cos, su, sd, MLA_ROPE // 4) * scale).astype(BF16)
        kb = kv[:, cols]
        kb = kb * _segment_rsqrt(kb, MLA_SLOT_SEGMENTS) * k_gain
        k_ref[:, cols] = jnp.where(rope_lanes, kpe, kb).astype(BF16)
    v_ref[...] = kv[:, MLA_HEADS * LANES:].astype(BF16)


def _mla_pre(st, xs, mod, g, w_down, q_g, kv_g, w_uq, w_ukv, qk_gain, rope_tabs):
    tab_spec = pl.BlockSpec((TOKEN_TILE, LANES), lambda b, j: (j, 0))
    slots = MLA_HEADS * LANES
    return pl.pallas_call(
        _mla_pre_kernel,
        grid=(st.batch, st.tiles),
        in_specs=[st.tile_spec(D_MODEL), st.mod_spec(), _resident((1, D_MODEL)),
                  _resident((D_MODEL, MLA_DOWN)), _resident((1, MLA_Q_RANK)), _resident((1, MLA_KV_RANK)),
                  _resident((MLA_Q_RANK, slots)), _resident((MLA_KV_RANK, slots + D_MODEL)),
                  _resident((2, LANES)), tab_spec, tab_spec, tab_spec],
        out_specs=[st.tile_spec(slots), st.tile_spec(slots), st.tile_spec(D_MODEL)],
        out_shape=[jax.ShapeDtypeStruct((st.batch, st.tokens, slots), BF16),
                   jax.ShapeDtypeStruct((st.batch, st.tokens, slots), BF16),
                   jax.ShapeDtypeStruct((st.batch, st.tokens, D_MODEL), BF16)],
        compiler_params=_params(2),
        name="mla_pre",
    )(xs, mod, g, w_down, q_g, kv_g, w_uq, w_ukv, qk_gain, *rope_tabs)


def _mla_attn_kernel(q_ref, k_ref, v_ref, o_ref, *, lat_tiles, seq):
    def attend(k, v):
        out = None
        for hh in range(2):
            s = _dot_t(q_ref[:, hh * LANES:(hh + 1) * LANES], k[:, hh * LANES:(hh + 1) * LANES])
            p = jnp.exp(s - jnp.max(s, axis=-1, keepdims=True))
            r = 1.0 / jnp.sum(p, axis=-1, keepdims=True)
            vh = jnp.where(_lane_mask(hh * MLA_V, (hh + 1) * MLA_V), v, jnp.zeros_like(v))
            o = _dot(p.astype(BF16), vh) * r
            out = o if out is None else out + o
        o_ref[...] = out.astype(BF16)

    j = pl.program_id(2)

    @pl.when(j < lat_tiles)
    def _():
        attend(k_ref[...], v_ref[...])

    @pl.when(j >= lat_tiles)
    def _():
        attend(k_ref[seq:, :], v_ref[seq:, :])


def _mla_attn(st, q, k, v):
    kern = functools.partial(_mla_attn_kernel, lat_tiles=st.lat_tiles, seq=st.seq)
    return pl.pallas_call(
        kern,
        grid=(st.batch, MLA_HEADS // 2, st.tiles),
        in_specs=[pl.BlockSpec((None, TOKEN_TILE, 2 * LANES), lambda b, h, j: (b, j, h)),
                  pl.BlockSpec((None, st.tokens, 2 * LANES), lambda b, h, j: (b, 0, h)),
                  pl.BlockSpec((None, st.tokens, LANES), lambda b, h, j: (b, 0, h))],
        out_specs=pl.BlockSpec((None, TOKEN_TILE, LANES), lambda b, h, j: (b, j, h)),
        out_shape=jax.ShapeDtypeStruct((st.batch, st.tokens, D_MODEL), BF16),
        compiler_params=_params(3),
        name="mla_attn",
    )(q, k, v)


def _na_row_start(r, rows):
    return jnp.clip(r - NA_KH // 2, 0, rows - NA_KH)


def _na_attn_kernel(q_ref, k_ref, v_ref, bias_ref, o_ref, *, rows, seq):
    r = pl.program_id(1)
    start = pl.multiple_of(_na_row_start(r, rows) * GRID_W, GRID_W)
    n_loc = NA_KH * GRID_W
    qi = lax.broadcasted_iota(jnp.int32, (GRID_W, n_loc), 0)
    kcol = lax.broadcasted_iota(jnp.int32, (GRID_W, n_loc), 1) % GRID_W
    col_start = jnp.clip(qi - NA_KW // 2, 0, GRID_W - NA_KW)
    col_mask = jnp.logical_and(kcol >= col_start, kcol < col_start + NA_KW)
    for pair in range(NA_HEADS // 2):
        cols = slice(pair * LANES, (pair + 1) * LANES)
        q = q_ref[:, cols]
        k_loc, v_loc = k_ref[pl.ds(start, n_loc), cols], v_ref[pl.ds(start, n_loc), cols]
        k_ctx, v_ctx = k_ref[seq:, cols], v_ref[seq:, cols]
        out = None
        for hh in range(2):
            head_lanes = _lane_mask(hh * NA_DH, (hh + 1) * NA_DH)
            qh = jnp.where(head_lanes, q, jnp.zeros_like(q))
            s_loc = jnp.where(col_mask, _dot_t(qh, k_loc) + bias_ref[2 * pair + hh], NEG_INF)
            s_ctx = _dot_t(qh, k_ctx)
            m = jnp.maximum(jnp.max(s_loc, axis=-1, keepdims=True), jnp.max(s_ctx, axis=-1, keepdims=True))
            p_loc, p_ctx = jnp.exp(s_loc - m), jnp.exp(s_ctx - m)
            rcp = 1.0 / (jnp.sum(p_loc, axis=-1, keepdims=True) + jnp.sum(p_ctx, axis=-1, keepdims=True))
            o = (_dot(p_loc.astype(BF16), jnp.where(head_lanes, v_loc, jnp.zeros_like(v_loc)))
                 + _dot(p_ctx.astype(BF16), jnp.where(head_lanes, v_ctx, jnp.zeros_like(v_ctx)))) * rcp
            out = o if out is None else out + o
        o_ref[:, cols] = out.astype(BF16)


def _na_bias_table(rpb):
    col = jnp.arange(GRID_W)
    col_idx = jnp.clip(col[None, :] - col[:, None] + NA_KW - 1, 0, 2 * NA_KW - 2)
    off = jnp.arange(NA_KH)[:, None]
    row_idx = jnp.arange(NA_KH)[None, :] - off + NA_KH - 1
    tab = rpb[:, row_idx[:, None, :, None], col_idx[None, :, None, :]]
    return jnp.moveaxis(tab, 1, 0).reshape(NA_KH, NA_HEADS, GRID_W, NA_KH * GRID_W)


def _na_attn(st, q, k, v, bias_tab):
    rows = st.seq // GRID_W
    assert rows >= NA_KH
    kern = functools.partial(_na_attn_kernel, rows=rows, seq=st.seq)
    return pl.pallas_call(
        kern,
        grid=(st.batch, rows),
        in_specs=[pl.BlockSpec((None, GRID_W, D_MODEL), lambda b, r: (b, r, 0)),
                  pl.BlockSpec((None, st.tokens, D_MODEL), lambda b, r: (b, 0, 0)),
                  pl.BlockSpec((None, st.tokens, D_MODEL), lambda b, r: (b, 0, 0)),
                  pl.BlockSpec((None, NA_HEADS, GRID_W, NA_KH * GRID_W),
                               lambda b, r: (r - _na_row_start(r, rows), 0, 0, 0))],
        out_specs=pl.BlockSpec((None, GRID_W, D_MODEL), lambda b, r: (b, r, 0)),
        out_shape=jax.ShapeDtypeStruct((st.batch, st.seq, D_MODEL), BF16),
        compiler_params=_params(2),
        name="na_attn",
    )(q, k, v, bias_tab)


def _slot_columns(w, width):
    kdim = w.shape[0]
    w = w.reshape(kdim, -1, width)
    return jnp.pad(w, ((0, 0), (0, 0), (0, LANES - width))).reshape(kdim, -1)


def _slot_gain(g):
    return jnp.pad(g, ((0, 0), (0, LANES - g.shape[1])))


def kernel(x, c, ctx, c_ctx, ada_w, ada_b, norm_g, ffn_w_in, ffn_w_out, pool_w, pool_b, pool_scale,
           diff_w_qkv, diff_qk_g, diff_lambda, diff_sub_g, diff_w_o,
           mla_w_dq, mla_q_g, mla_w_uq, mla_w_dkv, mla_kv_g, mla_w_ukv, mla_qk_g, mla_w_o,
           na_w_qkv, na_qk_g, na_rpb, na_w_o):
    assert DEPTH == N_MIXERS, "one layer per mixer: the context stream ends after the last softmax mixer's keys"
    batch, seq, _ = x.shape
    st = _Stream(batch, seq, ctx.shape[1])
    xs = jnp.concatenate([x, ctx], axis=1)

    rows = -(-(batch + 1) // 16) * 16
    cc = jnp.zeros((rows, D_MODEL), F32).at[:batch].set(c).at[batch].set(c_ctx)
    mods = _adaln(cc, ada_w, ada_b)

    w_in = ffn_w_in.astype(BF16)
    w_out = ffn_w_out.astype(BF16)
    gains = norm_g.reshape(DEPTH, 3, 1, D_MODEL)

    for i in range(DEPTH):
        mod = mods[i]
        xs = _ffn(st, xs, mod, 0, gains[i, 0], w_in[i, 0], w_out[i, 0])
        if i == 0:
            xs = _pool(st, xs, mod, gains[i, 1], pool_w[0].astype(BF16), pool_b[0].reshape(1, D_MODEL),
                       pool_scale[0].reshape(1, D_MODEL))
            xs = _ffn(st, xs, mod, 2, gains[i, 2], w_in[i, 1], w_out[i, 1])
            continue
        if i == 1:
            tabs = _rope_tables(st, DIFF_DH, 0, DIFF_DH)
            q, k, v = _qkv_pre(st, xs, mod, gains[i, 1], diff_w_qkv[0].astype(BF16), diff_qk_g[0], tabs)
            lam_init = 0.8 - 0.6 * math.exp(-0.3 * i)
            attn = _diff_attn(st, q, k, v, diff_lambda[0], diff_sub_g[0], lam_init)
            w_o = diff_w_o[0]
        elif i == 2:
            tabs = _rope_tables(st, MLA_ROPE, MLA_NOPE, LANES)
            kpe_cols = jnp.pad(mla_w_dkv[0][:, MLA_KV_RANK:], ((0, 0), (MLA_NOPE, LANES - MLA_NOPE - MLA_ROPE)))
            w_down = jnp.concatenate([mla_w_dq[0], mla_w_dkv[0][:, :MLA_KV_RANK], kpe_cols], axis=1)
            ukv = mla_w_ukv[0].reshape(MLA_KV_RANK, MLA_HEADS, MLA_NOPE + MLA_V)
            w_ukv = jnp.concatenate([_slot_columns(ukv[:, :, :MLA_NOPE].reshape(MLA_KV_RANK, -1), MLA_NOPE),
                                     ukv[:, :, MLA_NOPE:].reshape(MLA_KV_RANK, -1)], axis=1)
            q, k, v = _mla_pre(st, xs, mod, gains[i, 1], w_down.astype(BF16),
                               mla_q_g[0].reshape(1, -1), mla_kv_g[0].reshape(1, -1),
                               _slot_columns(mla_w_uq[0], MLA_NOPE + MLA_ROPE).astype(BF16),
                               w_ukv.astype(BF16), _slot_gain(mla_qk_g[0]), tabs)
            attn = _mla_attn(st, q, k, v)
            w_o = mla_w_o[0]
        else:
            q, k, v = _qkv_pre(st, xs, mod, gains[i, 1], na_w_qkv[0].astype(BF16), na_qk_g[0], None)
            attn = _na_attn(st, q, k, v, _na_bias_table(na_rpb[0]))
            w_o = na_w_o[0]
        xs = _mixout_ffn(st, xs, attn, mod, gains[i, 2], w_o.astype(BF16), w_in[i, 1], w_out[i, 1],
                         latent_only=(i == DEPTH - 1))
    return xs
```

```python
import functools
import math

import jax
import jax.numpy as jnp
from jax import lax
from jax.experimental import pallas as pl
from jax.experimental.pallas import tpu as pltpu

D_MODEL = 1024
DEPTH = 4
GRID_W = 64
N_MIXERS = 4
N_MOD = 9
EPS = 1e-6
LOG2E = math.log2(math.e)
ROPE_BASE = 10000.0
NEG_INF = -1e30
D_FF = 2816
POOL_WINDOWS = (2, 4, 8, 16)
POOL_G = D_MODEL // len(POOL_WINDOWS)
DIFF_HEADS = 8
DIFF_DH = 64
MLA_HEADS = 16
MLA_Q_RANK = 384
MLA_KV_RANK = 256
MLA_NOPE = 64
MLA_ROPE = 32
MLA_V = 64
NA_HEADS = 16
NA_DH = 64
NA_KH = 8
NA_KW = 16

LANES = 128
HALO = 8
TOKEN_TILE = 256
VMEM_LIMIT = 56 * 2**20

F32 = jnp.float32
BF16 = jnp.bfloat16


def _params(n_axes):
    return pltpu.CompilerParams(dimension_semantics=("arbitrary",) * n_axes,
                                vmem_limit_bytes=VMEM_LIMIT)


def _resident(shape):
    zeros = (0,) * len(shape)
    return pl.BlockSpec(shape, lambda *_: zeros, pipeline_mode=pl.Buffered(1))


def _dot(a, b):
    return jnp.dot(a, b, preferred_element_type=F32)


def _dot_t(a, b):
    return lax.dot_general(a, b, (((1,), (1,)), ((), ())), preferred_element_type=F32)


def _modulate(x, g, shift, scale):
    ms = jnp.mean(x * x, axis=-1, keepdims=True)
    return (x * lax.rsqrt(ms + EPS)) * (g * (1.0 + scale)) + shift


def _silu(x):
    return x * (1.0 / (1.0 + jnp.exp(-x)))


def _adaln_kernel(c_ref, w_ref, b_ref, o_ref):
    s = _silu(c_ref[...]).astype(BF16)
    o_ref[...] = _dot(s, w_ref[...].astype(BF16)) + b_ref[...]


def _adaln(cc, ada_w, ada_b):
    rows = cc.shape[0]
    n_col = N_MOD * D_MODEL // D_MODEL
    out = pl.pallas_call(
        _adaln_kernel,
        grid=(DEPTH, n_col),
        in_specs=[pl.BlockSpec((rows, D_MODEL), lambda i, n: (0, 0)),
                  pl.BlockSpec((None, D_MODEL, D_MODEL), lambda i, n: (i, 0, n)),
                  pl.BlockSpec((None, 1, D_MODEL), lambda i, n: (i, 0, n))],
        out_specs=pl.BlockSpec((None, rows, D_MODEL), lambda i, n: (i, 0, n)),
        out_shape=jax.ShapeDtypeStruct((DEPTH, rows, N_MOD * D_MODEL), F32),
        compiler_params=_params(2),
        name="adaln",
    )(cc, ada_w, ada_b.reshape(DEPTH, 1, N_MOD * D_MODEL))
    return out.reshape(DEPTH, rows, N_MOD, D_MODEL)


def _ffn_core(x, mod_ref, k, g_ref, win_ref, wout_ref):
    shift = mod_ref[3 * k:3 * k + 1, :]
    scale = mod_ref[3 * k + 1:3 * k + 2, :]
    gate = mod_ref[3 * k + 2:3 * k + 3, :]
    h = _modulate(x, g_ref[...], shift, scale).astype(BF16)
    u = _dot(h, win_ref[...])
    a = (_silu(u[:, :D_FF]) * u[:, D_FF:]).astype(BF16)
    return x + (0.5 * gate) * _dot(a, wout_ref[...])


def _ffn_kernel(x_ref, mod_ref, g_ref, win_ref, wout_ref, o_ref, *, k):
    o_ref[...] = _ffn_core(x_ref[...], mod_ref, k, g_ref, win_ref, wout_ref)


def _mixout_ffn_kernel(x_ref, a_ref, mod_ref, g_ref, wo_ref, win_ref, wout_ref, o_ref):
    x = x_ref[...] + mod_ref[5:6, :] * _dot(a_ref[...], wo_ref[...])
    o_ref[...] = _ffn_core(x, mod_ref, 2, g_ref, win_ref, wout_ref)


class _Stream:
    def __init__(self, batch, seq, ctx):
        assert seq % TOKEN_TILE == 0 and ctx == TOKEN_TILE and seq % GRID_W == 0
        self.batch, self.seq, self.ctx = batch, seq, ctx
        self.tokens = seq + ctx
        self.lat_tiles = seq // TOKEN_TILE
        self.tiles = self.tokens // TOKEN_TILE

    def tile_spec(self, width):
        return pl.BlockSpec((None, TOKEN_TILE, width), lambda b, j: (b, j, 0))

    def mod_spec(self):
        lat, ctx_row = self.lat_tiles, self.batch
        return pl.BlockSpec((None, N_MOD, D_MODEL),
                            lambda b, j: (jnp.where(j < lat, b, ctx_row), 0, 0))


def _ffn(st, xs, mod, k, g, w_in, w_out):
    return pl.pallas_call(
        functools.partial(_ffn_kernel, k=k),
        grid=(st.batch, st.tiles),
        in_specs=[st.tile_spec(D_MODEL), st.mod_spec(), _resident((1, D_MODEL)),
                  _resident((D_MODEL, 2 * D_FF)), _resident((D_FF, D_MODEL))],
        out_specs=st.tile_spec(D_MODEL),
        out_shape=jax.ShapeDtypeStruct((st.batch, st.tokens, D_MODEL), F32),
        compiler_params=_params(2),
        name="ffn",
    )(xs, mod, g, w_in, w_out)


def _mixout_ffn(st, xs, attn, mod, g, w_o, w_in, w_out, latent_only):
    tiles = st.lat_tiles if latent_only else st.tiles
    tokens = st.seq if latent_only else st.tokens
    return pl.pallas_call(
        _mixout_ffn_kernel,
        grid=(st.batch, tiles),
        in_specs=[st.tile_spec(D_MODEL), st.tile_spec(D_MODEL), st.mod_spec(),
                  _resident((1, D_MODEL)), _resident((D_MODEL, D_MODEL)),
                  _resident((D_MODEL, 2 * D_FF)), _resident((D_FF, D_MODEL))],
        out_specs=st.tile_spec(D_MODEL),
        out_shape=jax.ShapeDtypeStruct((st.batch, tokens, D_MODEL), F32),
        compiler_params=_params(2),
        name="mixout_ffn",
    )(xs, attn, mod, g, w_o, w_in, w_out)


def _pool_kernel(x_ref, xp_ref, xn_ref, mod_ref, g_ref, w_ref, b_ref, s_ref, o_ref, hext_ref,
                 *, lat_tiles, tiles, seq, ctx):
    j = pl.program_id(1)
    g, shift, scale = g_ref[...], mod_ref[3:4, :], mod_ref[4:5, :]
    x = x_ref[...]
    h = _modulate(x, g, shift, scale)
    has_prev = jnp.logical_and(j != 0, j != lat_tiles)
    has_next = jnp.logical_and(j != lat_tiles - 1, j != tiles - 1)
    hext_ref[0:HALO, :] = jnp.where(has_prev, _modulate(xp_ref[...], g, shift, scale), 0.0)
    hext_ref[HALO:HALO + TOKEN_TILE, :] = h
    hext_ref[HALO + TOKEN_TILE:, :] = jnp.where(has_next, _modulate(xn_ref[...], g, shift, scale), 0.0)
    in_lat = j < lat_tiles
    t = (j - jnp.where(in_lat, 0, lat_tiles)) * TOKEN_TILE + lax.broadcasted_iota(jnp.int32, (TOKEN_TILE, 1), 0)
    n = jnp.where(in_lat, seq, ctx)
    for gi, win in enumerate(POOL_WINDOWS):
        cols = slice(gi * POOL_G, (gi + 1) * POOL_G)
        acc = hext_ref[HALO - win // 2:HALO - win // 2 + TOKEN_TILE, cols]
        for k in range(1 - win // 2, win - win // 2):
            acc = acc + hext_ref[HALO + k:HALO + k + TOKEN_TILE, cols]
        lo = jnp.maximum(t - win // 2, 0)
        hi = jnp.minimum(t - win // 2 + win, n)
        mean = acc / (hi - lo).astype(F32)
        d = (mean - h[:, cols]).astype(BF16)
        y = (_dot(d, w_ref[gi]) + b_ref[:, cols]) * s_ref[:, cols]
        o_ref[:, cols] = x[:, cols] + mod_ref[5:6, cols] * y


def _pool(st, xs, mod, g, w, b, scale):
    per_tile = TOKEN_TILE // HALO
    last = st.tokens // HALO - 1
    kern = functools.partial(_pool_kernel, lat_tiles=st.lat_tiles, tiles=st.tiles, seq=st.seq, ctx=st.ctx)
    return pl.pallas_call(
        kern,
        grid=(st.batch, st.tiles),
        in_specs=[st.tile_spec(D_MODEL),
                  pl.BlockSpec((None, HALO, D_MODEL), lambda b_, j: (b_, jnp.maximum(j * per_tile - 1, 0), 0)),
                  pl.BlockSpec((None, HALO, D_MODEL), lambda b_, j: (b_, jnp.minimum((j + 1) * per_tile, last), 0)),
                  st.mod_spec(), _resident((1, D_MODEL)),
                  _resident((len(POOL_WINDOWS), POOL_G, POOL_G)),
                  _resident((1, D_MODEL)), _resident((1, D_MODEL))],
        out_specs=st.tile_spec(D_MODEL),
        out_shape=jax.ShapeDtypeStruct((st.batch, st.tokens, D_MODEL), F32),
        scratch_shapes=[pltpu.VMEM((TOKEN_TILE + 2 * HALO, D_MODEL), F32)],
        compiler_params=_params(2),
        name="pool",
    )(xs, xs, xs, mod, g, w, b, scale)


def _lane_mask(lo, hi):
    lane = lax.broadcasted_iota(jnp.int32, (1, LANES), 1)
    return jnp.logical_and(lane >= lo, lane < hi)


def _segment_rsqrt(x, segments):
    sq = x * x
    r = None
    for lo, hi in segments:
        m = _lane_mask(lo, hi)
        ms = jnp.sum(jnp.where(m, sq, 0.0), axis=-1, keepdims=True) * (1.0 / (hi - lo))
        rs = lax.rsqrt(ms + EPS)
        r = rs if r is None else jnp.where(m, rs, r)
    return r


def _rotate(y, cos, sin_up, sin_dn, shift):
    return (y * cos + pltpu.roll(y, LANES - shift, axis=1) * sin_up
            + pltpu.roll(y, shift, axis=1) * sin_dn)


def _rope_tables(st, dim, lane_lo, period):
    nf = dim // 4
    inv = ROPE_BASE ** (-jnp.arange(nf, dtype=F32) / nf)
    t = jnp.arange(st.seq, dtype=jnp.int32)
    pos = jnp.stack([t // GRID_W, t % GRID_W], axis=-1).astype(F32)
    ang = jnp.broadcast_to(pos[:, :, None, None] * inv, (st.seq, 2, 2, nf)).reshape(st.seq, dim)
    cos, sin = jnp.cos(ang), jnp.sin(ang)
    first_half = (jnp.arange(dim) % (2 * nf)) < nf
    sin_up = jnp.where(first_half, -sin, 0.0)
    sin_dn = jnp.where(first_half, 0.0, sin)

    def place(tab, fill):
        blk = jnp.full((st.seq, period), fill, F32).at[:, lane_lo:lane_lo + dim].set(tab)
        blk = jnp.tile(blk, (1, LANES // period))
        return jnp.concatenate([blk, jnp.full((st.ctx, LANES), fill, F32)], axis=0)

    return place(cos, 1.0), place(sin_up, 0.0), place(sin_dn, 0.0)


def _qkv_pre_kernel(*refs, rope):
    if rope:
        x_ref, mod_ref, g_ref, w_ref, qkg_ref, cos_ref, su_ref, sd_ref, q_ref, k_ref, v_ref = refs
    else:
        x_ref, mod_ref, g_ref, w_ref, qkg_ref, q_ref, k_ref, v_ref = refs
    h = _modulate(x_ref[...], g_ref[...], mod_ref[3:4, :], mod_ref[4:5, :]).astype(BF16)
    qkv = _dot(h, w_ref[...])
    dh = DIFF_DH
    segments = ((0, dh), (dh, 2 * dh))
    for t, (dst, post) in enumerate(((q_ref, dh ** -0.5 * LOG2E), (k_ref, 1.0))):
        gain = qkg_ref[t:t + 1, :]
        for c in range(D_MODEL // LANES):
            xb = qkv[:, t * D_MODEL + c * LANES:t * D_MODEL + (c + 1) * LANES]
            y = xb * _segment_rsqrt(xb, segments) * gain
            if rope:
                y = _rotate(y, cos_ref[...], su_ref[...], sd_ref[...], dh // 4)
            dst[:, c * LANES:(c + 1) * LANES] = (y * post).astype(BF16)
    v_ref[...] = qkv[:, 2 * D_MODEL:].astype(BF16)


def _qkv_pre(st, xs, mod, g, w_qkv, qk_gain, rope_tabs):
    rope = rope_tabs is not None
    tab_spec = pl.BlockSpec((TOKEN_TILE, LANES), lambda b, j: (j, 0))
    in_specs = [st.tile_spec(D_MODEL), st.mod_spec(), _resident((1, D_MODEL)),
                _resident((D_MODEL, 3 * D_MODEL)), _resident((2, LANES))]
    args = [xs, mod, g, w_qkv, jnp.tile(qk_gain, (1, 2))]
    if rope:
        in_specs += [tab_spec] * 3
        args += list(rope_tabs)
    shape = jax.ShapeDtypeStruct((st.batch, st.tokens, D_MODEL), BF16)
    return pl.pallas_call(
        functools.partial(_qkv_pre_kernel, rope=rope),
        grid=(st.batch, st.tiles),
        in_specs=in_specs,
        out_specs=[st.tile_spec(D_MODEL)] * 3,
        out_shape=[shape] * 3,
        compiler_params=_params(2),
        name="qkv_pre",
    )(*args)


DIFF_HEADS_PER_STEP = 4


def _diff_attn_kernel(q_ref, k_ref, v_ref, lam_ref, subg_ref, o_ref, *, lat_tiles, seq, lam_init):
    lp = lam_ref[...]
    lam = (jnp.exp(jnp.sum(lp[0:1] * lp[1:2], axis=-1, keepdims=True))
           - jnp.exp(jnp.sum(lp[2:3] * lp[3:4], axis=-1, keepdims=True)) + lam_init)

    def attend(key_rows):
        for hd in range(DIFF_HEADS_PER_STEP):
            cols = slice(hd * LANES, (hd + 1) * LANES)
            q, k, v = q_ref[:, cols], k_ref[key_rows, cols], v_ref[key_rows, cols]
            probs = []
            for c in range(2):
                qc = jnp.where(_lane_mask(c * DIFF_DH, (c + 1) * DIFF_DH), q, jnp.zeros_like(q))
                s = _dot_t(qc, k)
                p = jnp.exp2(s - jnp.max(s, axis=-1, keepdims=True))
                probs.append((p, 1.0 / jnp.sum(p, axis=-1, keepdims=True)))
            (p0, r0), (p1, r1) = probs
            a = (p0 * r0 - p1 * (lam * r1)).astype(BF16)
            o = _dot(a, v)
            ms = jnp.mean(o * o, axis=-1, keepdims=True)
            o_ref[:, cols] = ((o * lax.rsqrt(ms + EPS)) * subg_ref[...] * (1.0 - lam_init)).astype(BF16)

    j = pl.program_id(2)

    @pl.when(j < lat_tiles)
    def _():
        attend(slice(None))

    @pl.when(j >= lat_tiles)
    def _():
        attend(slice(seq, None))


def _diff_attn(st, q, k, v, lam_p, sub_g, lam_init):
    width = DIFF_HEADS_PER_STEP * LANES
    head = lambda b, h, j: (b, j, h)
    keys = lambda b, h, j: (b, 0, h)
    kern = functools.partial(_diff_attn_kernel, lat_tiles=st.lat_tiles, seq=st.seq, lam_init=lam_init)
    return pl.pallas_call(
        kern,
        grid=(st.batch, DIFF_HEADS // DIFF_HEADS_PER_STEP, st.tiles),
        in_specs=[pl.BlockSpec((None, TOKEN_TILE, width), head),
                  pl.BlockSpec((None, st.tokens, width), keys),
                  pl.BlockSpec((None, st.tokens, width), keys),
                  _resident((4, DIFF_DH)), _resident((1, LANES))],
        out_specs=pl.BlockSpec((None, TOKEN_TILE, width), head),
        out_shape=jax.ShapeDtypeStruct((st.batch, st.tokens, D_MODEL), BF16),
        compiler_params=_params(3),
        name="diff_attn",
    )(q, k, v, lam_p, sub_g.reshape(1, LANES))


MLA_SLOT_SEGMENTS = ((0, MLA_NOPE), (MLA_NOPE, MLA_NOPE + MLA_ROPE))
MLA_DOWN = MLA_Q_RANK + MLA_KV_RANK + LANES


def _mla_pre_kernel(x_ref, mod_ref, g_ref, wd_ref, qg_ref, kvg_ref, wuq_ref, wukv_ref, qkg_ref,
                    cos_ref, su_ref, sd_ref, q_ref, k_ref, v_ref):
    h = _modulate(x_ref[...], g_ref[...], mod_ref[3:4, :], mod_ref[4:5, :]).astype(BF16)
    down = _dot(h, wd_ref[...])
    cos, su, sd = cos_ref[...], su_ref[...], sd_ref[...]

    def full_norm(z, gain):
        ms = jnp.mean(z * z, axis=-1, keepdims=True)
        return ((z * lax.rsqrt(ms + EPS)) * gain).astype(BF16)

    q = _dot(full_norm(down[:, :MLA_Q_RANK], qg_ref[...]), wuq_ref[...])
    kv = _dot(full_norm(down[:, MLA_Q_RANK:MLA_Q_RANK + MLA_KV_RANK], kvg_ref[...]), wukv_ref[...])
    q_gain, k_gain = qkg_ref[0:1, :], qkg_ref[1:2, :]
    scale = (MLA_NOPE + MLA_ROPE) ** -0.5 * LOG2E
    kpe = down[:, MLA_Q_RANK + MLA_KV_RANK:]
    kpe = kpe * _segment_rsqrt(kpe, MLA_SLOT_SEGMENTS) * k_gain
    kpe = _rotate(kpe, cos, su, sd, MLA_ROPE // 4)
    rope_lanes = _lane_mask(MLA_NOPE, MLA_NOPE + MLA_ROPE)
    for hd in range(MLA_HEADS):
        cols = slice(hd * LANES, (hd + 1) * LANES)
        qb = q[:, cols]
        qb = qb * _segment_rsqrt(qb, MLA_SLOT_SEGMENTS) * q_gain
        q_ref[:, cols] = (_rotate(qb, cos, su, sd, MLA_ROPE // 4) * scale).astype(BF16)
        kb = kv[:, cols]
        kb = kb * _segment_rsqrt(kb, MLA_SLOT_SEGMENTS) * k_gain
        k_ref[:, cols] = jnp.where(rope_lanes, kpe, kb).astype(BF16)
    v_ref[...] = kv[:, MLA_HEADS * LANES:].astype(BF16)


def _mla_pre(st, xs, mod, g, w_down, q_g, kv_g, w_uq, w_ukv, qk_gain, rope_tabs):
    tab_spec = pl.BlockSpec((TOKEN_TILE, LANES), lambda b, j: (j, 0))
    slots = MLA_HEADS * LANES
    return pl.pallas_call(
        _mla_pre_kernel,
        grid=(st.batch, st.tiles),
        in_specs=[st.tile_spec(D_MODEL), st.mod_spec(), _resident((1, D_MODEL)),
                  _resident((D_MODEL, MLA_DOWN)), _resident((1, MLA_Q_RANK)), _resident((1, MLA_KV_RANK)),
                  _resident((MLA_Q_RANK, slots)), _resident((MLA_KV_RANK, slots + D_MODEL)),
                  _resident((2, LANES)), tab_spec, tab_spec, tab_spec],
        out_specs=[st.tile_spec(slots), st.tile_spec(slots), st.tile_spec(D_MODEL)],
        out_shape=[jax.ShapeDtypeStruct((st.batch, st.tokens, slots), BF16),
                   jax.ShapeDtypeStruct((st.batch, st.tokens, slots), BF16),
                   jax.ShapeDtypeStruct((st.batch, st.tokens, D_MODEL), BF16)],
        compiler_params=_params(2),
        name="mla_pre",
    )(xs, mod, g, w_down, q_g, kv_g, w_uq, w_ukv, qk_gain, *rope_tabs)


MLA_PAIRS_PER_STEP = 4


def _mla_attn_kernel(q_ref, k_ref, v_ref, o_ref, *, lat_tiles, seq):
    def attend(key_rows):
        for pair in range(MLA_PAIRS_PER_STEP):
            v = v_ref[key_rows, pair * LANES:(pair + 1) * LANES]
            out = None
            for hh in range(2):
                slot = slice((2 * pair + hh) * LANES, (2 * pair + hh + 1) * LANES)
                s = _dot_t(q_ref[:, slot], k_ref[key_rows, slot])
                p = jnp.exp2(s - jnp.max(s, axis=-1, keepdims=True))
                r = 1.0 / jnp.sum(p, axis=-1, keepdims=True)
                vh = jnp.where(_lane_mask(hh * MLA_V, (hh + 1) * MLA_V), v, jnp.zeros_like(v))
                o = _dot(p.astype(BF16), vh) * r
                out = o if out is None else out + o
            o_ref[:, pair * LANES:(pair + 1) * LANES] = out.astype(BF16)

    j = pl.program_id(2)

    @pl.when(j < lat_tiles)
    def _():
        attend(slice(None))

    @pl.when(j >= lat_tiles)
    def _():
        attend(slice(seq, None))


def _mla_attn(st, q, k, v):
    slots = 2 * MLA_PAIRS_PER_STEP * LANES
    vals = MLA_PAIRS_PER_STEP * LANES
    kern = functools.partial(_mla_attn_kernel, lat_tiles=st.lat_tiles, seq=st.seq)
    return pl.pallas_call(
        kern,
        grid=(st.batch, MLA_HEADS // (2 * MLA_PAIRS_PER_STEP), st.tiles),
        in_specs=[pl.BlockSpec((None, TOKEN_TILE, slots), lambda b, h, j: (b, j, h)),
                  pl.BlockSpec((None, st.tokens, slots), lambda b, h, j: (b, 0, h)),
                  pl.BlockSpec((None, st.tokens, vals), lambda b, h, j: (b, 0, h))],
        out_specs=pl.BlockSpec((None, TOKEN_TILE, vals), lambda b, h, j: (b, j, h)),
        out_shape=jax.ShapeDtypeStruct((st.batch, st.tokens, D_MODEL), BF16),
        compiler_params=_params(3),
        name="mla_attn",
    )(q, k, v)


NA_QROWS = TOKEN_TILE // GRID_W
NA_WROWS = NA_QROWS + NA_KH
NA_PAIRS = NA_WROWS // 2
NA_BIAS_LO = -(NA_QROWS - 1) - (NA_WROWS - NA_QROWS - NA_KH // 2) + NA_KH - 1
NA_BIAS_N = (NA_WROWS - 2) + NA_KH - 1 - NA_BIAS_LO + 1


def _na_attn_kernel(q_ref, k_ref, v_ref, bias_ref, o_ref, mask_ref, *, rows, seq):
    r0 = pl.program_id(1) * NA_QROWS
    w0 = jnp.clip(r0 - NA_KH // 2, 0, rows - NA_WROWS)
    start = pl.multiple_of(w0 * GRID_W, GRID_W)
    n_loc = NA_WROWS * GRID_W
    qidx = lax.broadcasted_iota(jnp.int32, (TOKEN_TILE, n_loc), 0)
    kidx = lax.broadcasted_iota(jnp.int32, (TOKEN_TILE, n_loc), 1)
    q_row, q_col = r0 + qidx // GRID_W, qidx % GRID_W
    k_row, k_col = w0 + kidx // GRID_W, kidx % GRID_W
    row_start = jnp.clip(q_row - NA_KH // 2, 0, rows - NA_KH)
    col_start = jnp.clip(q_col - NA_KW // 2, 0, GRID_W - NA_KW)
    valid = ((k_row >= row_start) & (k_row < row_start + NA_KH)
             & (k_col >= col_start) & (k_col < col_start + NA_KW))
    mask_ref[...] = valid.astype(jnp.int32)
    base = w0 - r0 + NA_KH - 1 - NA_BIAS_LO
    for pair in range(NA_HEADS // 2):
        cols = slice(pair * LANES, (pair + 1) * LANES)
        q = q_ref[:, cols]
        k_loc, v_loc = k_ref[pl.ds(start, n_loc), cols], v_ref[pl.ds(start, n_loc), cols]
        k_ctx, v_ctx = k_ref[seq:, cols], v_ref[seq:, cols]
        out = None
        for hh in range(2):
            head = 2 * pair + hh
            head_lanes = _lane_mask(hh * NA_DH, (hh + 1) * NA_DH)
            qh = jnp.where(head_lanes, q, jnp.zeros_like(q))
            bias = jnp.concatenate(
                [jnp.concatenate([bias_ref[head, base + 2 * p - a] for p in range(NA_PAIRS)], axis=1)
                 for a in range(NA_QROWS)], axis=0)
            s_loc = jnp.where(mask_ref[...] != 0, _dot_t(qh, k_loc) + bias, NEG_INF)
            s_ctx = _dot_t(qh, k_ctx)
            m = jnp.maximum(jnp.max(s_loc, axis=-1, keepdims=True), jnp.max(s_ctx, axis=-1, keepdims=True))
            p_loc, p_ctx = jnp.exp2(s_loc - m), jnp.exp2(s_ctx - m)
            rcp = 1.0 / (jnp.sum(p_loc, axis=-1, keepdims=True) + jnp.sum(p_ctx, axis=-1, keepdims=True))
            o = (_dot(p_loc.astype(BF16), jnp.where(head_lanes, v_loc, jnp.zeros_like(v_loc)))
                 + _dot(p_ctx.astype(BF16), jnp.where(head_lanes, v_ctx, jnp.zeros_like(v_ctx)))) * rcp
            out = o if out is None else out + o
        o_ref[:, cols] = out.astype(BF16)


def _na_bias_table(rpb):
    n_rel, n_col = 2 * NA_KH - 1, 2 * NA_KW - 1
    left = GRID_W - NA_KW
    v = jnp.pad(rpb * LOG2E, ((0, 0), (0, 0), (left, 2 * GRID_W - left - n_col)))
    v = jnp.tile(v, (1, 1, GRID_W))[:, :, :GRID_W * (2 * GRID_W - 1)]
    toep = v.reshape(NA_HEADS, n_rel, GRID_W, 2 * GRID_W - 1)[..., GRID_W - 1:]
    toep = jnp.pad(toep, ((0, 0), (-NA_BIAS_LO, NA_BIAS_LO + NA_BIAS_N + 1 - n_rel), (0, 0), (0, 0)))
    return jnp.concatenate([toep[:, :-1], toep[:, 1:]], axis=-1)


def _na_attn(st, q, k, v, bias_tab):
    rows = st.seq // GRID_W
    assert rows >= NA_WROWS and rows % NA_QROWS == 0
    kern = functools.partial(_na_attn_kernel, rows=rows, seq=st.seq)
    return pl.pallas_call(
        kern,
        grid=(st.batch, st.lat_tiles),
        in_specs=[st.tile_spec(D_MODEL),
                  pl.BlockSpec((None, st.tokens, D_MODEL), lambda b, j: (b, 0, 0)),
                  pl.BlockSpec((None, st.tokens, D_MODEL), lambda b, j: (b, 0, 0)),
                  _resident((NA_HEADS, NA_BIAS_N, GRID_W, 2 * GRID_W))],
        out_specs=st.tile_spec(D_MODEL),
        out_shape=jax.ShapeDtypeStruct((st.batch, st.seq, D_MODEL), BF16),
        scratch_shapes=[pltpu.VMEM((TOKEN_TILE, NA_WROWS * GRID_W), jnp.int32)],
        compiler_params=_params(2),
        name="na_attn",
    )(q, k, v, bias_tab)


def _slot_columns(w, width):
    kdim = w.shape[0]
    w = w.reshape(kdim, -1, width)
    return jnp.pad(w, ((0, 0), (0, 0), (0, LANES - width))).reshape(kdim, -1)


def _slot_gain(g):
    return jnp.pad(g, ((0, 0), (0, LANES - g.shape[1])))


def kernel(x, c, ctx, c_ctx, ada_w, ada_b, norm_g, ffn_w_in, ffn_w_out, pool_w, pool_b, pool_scale,
           diff_w_qkv, diff_qk_g, diff_lambda, diff_sub_g, diff_w_o,
           mla_w_dq, mla_q_g, mla_w_uq, mla_w_dkv, mla_kv_g, mla_w_ukv, mla_qk_g, mla_w_o,
           na_w_qkv, na_qk_g, na_rpb, na_w_o):
    assert DEPTH == N_MIXERS, "one layer per mixer: the context stream ends after the last softmax mixer's keys"
    batch, seq, _ = x.shape
    st = _Stream(batch, seq, ctx.shape[1])
    xs = jnp.concatenate([x, ctx], axis=1)

    rows = -(-(batch + 1) // 16) * 16
    cc = jnp.zeros((rows, D_MODEL), F32).at[:batch].set(c).at[batch].set(c_ctx)
    mods = _adaln(cc, ada_w, ada_b)

    w_in = ffn_w_in.astype(BF16)
    w_out = ffn_w_out.astype(BF16)
    gains = norm_g.reshape(DEPTH, 3, 1, D_MODEL)

    for i in range(DEPTH):
        mod = mods[i]
        xs = _ffn(st, xs, mod, 0, gains[i, 0], w_in[i, 0], w_out[i, 0])
        if i == 0:
            xs = _pool(st, xs, mod, gains[i, 1], pool_w[0].astype(BF16), pool_b[0].reshape(1, D_MODEL),
                       pool_scale[0].reshape(1, D_MODEL))
            xs = _ffn(st, xs, mod, 2, gains[i, 2], w_in[i, 1], w_out[i, 1])
            continue
        if i == 1:
            tabs = _rope_tables(st, DIFF_DH, 0, DIFF_DH)
            q, k, v = _qkv_pre(st, xs, mod, gains[i, 1], diff_w_qkv[0].astype(BF16), diff_qk_g[0], tabs)
            lam_init = 0.8 - 0.6 * math.exp(-0.3 * i)
            attn = _diff_attn(st, q, k, v, diff_lambda[0], diff_sub_g[0], lam_init)
            w_o = diff_w_o[0]
        elif i == 2:
            tabs = _rope_tables(st, MLA_ROPE, MLA_NOPE, LANES)
            kpe_cols = jnp.pad(mla_w_dkv[0][:, MLA_KV_RANK:], ((0, 0), (MLA_NOPE, LANES - MLA_NOPE - MLA_ROPE)))
            w_down = jnp.concatenate([mla_w_dq[0], mla_w_dkv[0][:, :MLA_KV_RANK], kpe_cols], axis=1)
            ukv = mla_w_ukv[0].reshape(MLA_KV_RANK, MLA_HEADS, MLA_NOPE + MLA_V)
            w_ukv = jnp.concatenate([_slot_columns(ukv[:, :, :MLA_NOPE].reshape(MLA_KV_RANK, -1), MLA_NOPE),
                                     ukv[:, :, MLA_NOPE:].reshape(MLA_KV_RANK, -1)], axis=1)
            q, k, v = _mla_pre(st, xs, mod, gains[i, 1], w_down.astype(BF16),
                               mla_q_g[0].reshape(1, -1), mla_kv_g[0].reshape(1, -1),
                               _slot_columns(mla_w_uq[0], MLA_NOPE + MLA_ROPE).astype(BF16),
                               w_ukv.astype(BF16), _slot_gain(mla_qk_g[0]), tabs)
            attn = _mla_attn(st, q, k, v)
            w_o = mla_w_o[0]
        else:
            q, k, v = _qkv_pre(st, xs, mod, gains[i, 1], na_w_qkv[0].astype(BF16), na_qk_g[0], None)
            attn = _na_attn(st, q, k, v, _na_bias_table(na_rpb[0]))
            w_o = na_w_o[0]
        xs = _mixout_ffn(st, xs, attn, mod, gains[i, 2], w_o.astype(BF16), w_in[i, 1], w_out[i, 1],
                         latent_only=(i == DEPTH - 1))
    return xs
```

```python
import functools
import math

import jax
import jax.numpy as jnp
from jax import lax
from jax.experimental import pallas as pl
from jax.experimental.pallas import tpu as pltpu

D_MODEL = 1024
DEPTH = 4
GRID_W = 64
N_MIXERS = 4
N_MOD = 9
EPS = 1e-6
LOG2E = math.log2(math.e)
ROPE_BASE = 10000.0
NEG_INF = -1e30
D_FF = 2816
POOL_WINDOWS = (2, 4, 8, 16)
POOL_G = D_MODEL // len(POOL_WINDOWS)
DIFF_HEADS = 8
DIFF_DH = 64
MLA_HEADS = 16
MLA_Q_RANK = 384
MLA_KV_RANK = 256
MLA_NOPE = 64
MLA_ROPE = 32
MLA_V = 64
NA_HEADS = 16
NA_DH = 64
NA_KH = 8
NA_KW = 16

LANES = 128
HALO = 8
TOKEN_TILE = 256
ATTN_TILE = 512
VMEM_LIMIT = 56 * 2**20

F32 = jnp.float32
BF16 = jnp.bfloat16


def _params(n_axes):
    return pltpu.CompilerParams(dimension_semantics=("arbitrary",) * n_axes,
                                vmem_limit_bytes=VMEM_LIMIT)


def _resident(shape):
    zeros = (0,) * len(shape)
    return pl.BlockSpec(shape, lambda *_: zeros, pipeline_mode=pl.Buffered(1))


def _dot(a, b):
    return jnp.dot(a, b, preferred_element_type=F32)


def _dot_t(a, b):
    return lax.dot_general(a, b, (((1,), (1,)), ((), ())), preferred_element_type=F32)


def _modulate(x, g, shift, scale):
    ms = jnp.mean(x * x, axis=-1, keepdims=True)
    return (x * lax.rsqrt(ms + EPS)) * (g * (1.0 + scale)) + shift


def _silu(x):
    return x * (1.0 / (1.0 + jnp.exp(-x)))


def _adaln_kernel(c_ref, w_ref, b_ref, o_ref):
    s = _silu(c_ref[...]).astype(BF16)
    o_ref[...] = _dot(s, w_ref[...].astype(BF16)) + b_ref[...]


def _adaln(cc, ada_w, ada_b):
    rows = cc.shape[0]
    n_col = N_MOD * D_MODEL // D_MODEL
    out = pl.pallas_call(
        _adaln_kernel,
        grid=(DEPTH, n_col),
        in_specs=[pl.BlockSpec((rows, D_MODEL), lambda i, n: (0, 0)),
                  pl.BlockSpec((None, D_MODEL, D_MODEL), lambda i, n: (i, 0, n)),
                  pl.BlockSpec((None, 1, D_MODEL), lambda i, n: (i, 0, n))],
        out_specs=pl.BlockSpec((None, rows, D_MODEL), lambda i, n: (i, 0, n)),
        out_shape=jax.ShapeDtypeStruct((DEPTH, rows, N_MOD * D_MODEL), F32),
        compiler_params=_params(2),
        name="adaln",
    )(cc, ada_w, ada_b.reshape(DEPTH, 1, N_MOD * D_MODEL))
    return out.reshape(DEPTH, rows, N_MOD, D_MODEL)


def _ffn_core(x, mod_ref, k, g_ref, win_ref, wout_ref):
    shift = mod_ref[3 * k:3 * k + 1, :]
    scale = mod_ref[3 * k + 1:3 * k + 2, :]
    gate = mod_ref[3 * k + 2:3 * k + 3, :]
    h = _modulate(x, g_ref[...], shift, scale).astype(BF16)
    u = _dot(h, win_ref[...])
    a = (_silu(u[:, :D_FF]) * u[:, D_FF:]).astype(BF16)
    return x + (0.5 * gate) * _dot(a, wout_ref[...])


def _ffn_kernel(x_ref, mod_ref, g_ref, win_ref, wout_ref, o_ref, *, k):
    o_ref[...] = _ffn_core(x_ref[...], mod_ref, k, g_ref, win_ref, wout_ref)


def _mixout_ffn_kernel(x_ref, a_ref, mod_ref, g_ref, wo_ref, win_ref, wout_ref, o_ref):
    x = x_ref[...] + mod_ref[5:6, :] * _dot(a_ref[...], wo_ref[...])
    o_ref[...] = _ffn_core(x, mod_ref, 2, g_ref, win_ref, wout_ref)


class _Stream:
    def __init__(self, batch, seq, ctx):
        assert seq % TOKEN_TILE == 0 and ctx == TOKEN_TILE and seq % GRID_W == 0
        self.batch, self.seq, self.ctx = batch, seq, ctx
        self.tokens = seq + ctx
        self.lat_tiles = seq // TOKEN_TILE
        self.tiles = self.tokens // TOKEN_TILE
        assert seq % ATTN_TILE == 0 and ATTN_TILE >= ctx
        self.attn_steps = seq // ATTN_TILE + 1

    def tile_spec(self, width):
        return pl.BlockSpec((None, TOKEN_TILE, width), lambda b, j: (b, j, 0))

    def mod_spec(self):
        lat, ctx_row = self.lat_tiles, self.batch
        return pl.BlockSpec((None, N_MOD, D_MODEL),
                            lambda b, j: (jnp.where(j < lat, b, ctx_row), 0, 0))


def _ffn(st, xs, mod, k, g, w_in, w_out):
    return pl.pallas_call(
        functools.partial(_ffn_kernel, k=k),
        grid=(st.batch, st.tiles),
        in_specs=[st.tile_spec(D_MODEL), st.mod_spec(), _resident((1, D_MODEL)),
                  _resident((D_MODEL, 2 * D_FF)), _resident((D_FF, D_MODEL))],
        out_specs=st.tile_spec(D_MODEL),
        out_shape=jax.ShapeDtypeStruct((st.batch, st.tokens, D_MODEL), F32),
        compiler_params=_params(2),
        name="ffn",
    )(xs, mod, g, w_in, w_out)


def _mixout_ffn(st, xs, attn, mod, g, w_o, w_in, w_out, latent_only):
    tiles = st.lat_tiles if latent_only else st.tiles
    tokens = st.seq if latent_only else st.tokens
    return pl.pallas_call(
        _mixout_ffn_kernel,
        grid=(st.batch, tiles),
        in_specs=[st.tile_spec(D_MODEL), st.tile_spec(D_MODEL), st.mod_spec(),
                  _resident((1, D_MODEL)), _resident((D_MODEL, D_MODEL)),
                  _resident((D_MODEL, 2 * D_FF)), _resident((D_FF, D_MODEL))],
        out_specs=st.tile_spec(D_MODEL),
        out_shape=jax.ShapeDtypeStruct((st.batch, tokens, D_MODEL), F32),
        compiler_params=_params(2),
        name="mixout_ffn",
    )(xs, attn, mod, g, w_o, w_in, w_out)


def _pool_kernel(x_ref, xp_ref, xn_ref, mod_ref, g_ref, w_ref, b_ref, s_ref, o_ref, hext_ref,
                 *, lat_tiles, tiles, seq, ctx):
    j = pl.program_id(1)
    g, shift, scale = g_ref[...], mod_ref[3:4, :], mod_ref[4:5, :]
    x = x_ref[...]
    h = _modulate(x, g, shift, scale)
    has_prev = jnp.logical_and(j != 0, j != lat_tiles)
    has_next = jnp.logical_and(j != lat_tiles - 1, j != tiles - 1)
    hext_ref[0:HALO, :] = jnp.where(has_prev, _modulate(xp_ref[...], g, shift, scale), 0.0)
    hext_ref[HALO:HALO + TOKEN_TILE, :] = h
    hext_ref[HALO + TOKEN_TILE:, :] = jnp.where(has_next, _modulate(xn_ref[...], g, shift, scale), 0.0)
    in_lat = j < lat_tiles
    t = (j - jnp.where(in_lat, 0, lat_tiles)) * TOKEN_TILE + lax.broadcasted_iota(jnp.int32, (TOKEN_TILE, 1), 0)
    n = jnp.where(in_lat, seq, ctx)
    for gi, win in enumerate(POOL_WINDOWS):
        cols = slice(gi * POOL_G, (gi + 1) * POOL_G)
        acc = hext_ref[HALO - win // 2:HALO - win // 2 + TOKEN_TILE, cols]
        for k in range(1 - win // 2, win - win // 2):
            acc = acc + hext_ref[HALO + k:HALO + k + TOKEN_TILE, cols]
        lo = jnp.maximum(t - win // 2, 0)
        hi = jnp.minimum(t - win // 2 + win, n)
        mean = acc / (hi - lo).astype(F32)
        d = (mean - h[:, cols]).astype(BF16)
        y = (_dot(d, w_ref[gi]) + b_ref[:, cols]) * s_ref[:, cols]
        o_ref[:, cols] = x[:, cols] + mod_ref[5:6, cols] * y


def _pool(st, xs, mod, g, w, b, scale):
    per_tile = TOKEN_TILE // HALO
    last = st.tokens // HALO - 1
    kern = functools.partial(_pool_kernel, lat_tiles=st.lat_tiles, tiles=st.tiles, seq=st.seq, ctx=st.ctx)
    return pl.pallas_call(
        kern,
        grid=(st.batch, st.tiles),
        in_specs=[st.tile_spec(D_MODEL),
                  pl.BlockSpec((None, HALO, D_MODEL), lambda b_, j: (b_, jnp.maximum(j * per_tile - 1, 0), 0)),
                  pl.BlockSpec((None, HALO, D_MODEL), lambda b_, j: (b_, jnp.minimum((j + 1) * per_tile, last), 0)),
                  st.mod_spec(), _resident((1, D_MODEL)),
                  _resident((len(POOL_WINDOWS), POOL_G, POOL_G)),
                  _resident((1, D_MODEL)), _resident((1, D_MODEL))],
        out_specs=st.tile_spec(D_MODEL),
        out_shape=jax.ShapeDtypeStruct((st.batch, st.tokens, D_MODEL), F32),
        scratch_shapes=[pltpu.VMEM((TOKEN_TILE + 2 * HALO, D_MODEL), F32)],
        compiler_params=_params(2),
        name="pool",
    )(xs, xs, xs, mod, g, w, b, scale)


def _lane_mask(lo, hi):
    lane = lax.broadcasted_iota(jnp.int32, (1, LANES), 1)
    return jnp.logical_and(lane >= lo, lane < hi)


def _segment_rsqrt(x, segments):
    sq = x * x
    r = None
    for lo, hi in segments:
        m = _lane_mask(lo, hi)
        ms = jnp.sum(jnp.where(m, sq, 0.0), axis=-1, keepdims=True) * (1.0 / (hi - lo))
        rs = lax.rsqrt(ms + EPS)
        r = rs if r is None else jnp.where(m, rs, r)
    return r


def _rotate(y, cos, sin_up, sin_dn, shift):
    return (y * cos + pltpu.roll(y, LANES - shift, axis=1) * sin_up
            + pltpu.roll(y, shift, axis=1) * sin_dn)


def _rope_tables(st, dim, lane_lo, period):
    nf = dim // 4
    inv = ROPE_BASE ** (-jnp.arange(nf, dtype=F32) / nf)
    t = jnp.arange(st.seq, dtype=jnp.int32)
    pos = jnp.stack([t // GRID_W, t % GRID_W], axis=-1).astype(F32)
    ang = jnp.broadcast_to(pos[:, :, None, None] * inv, (st.seq, 2, 2, nf)).reshape(st.seq, dim)
    cos, sin = jnp.cos(ang), jnp.sin(ang)
    first_half = (jnp.arange(dim) % (2 * nf)) < nf
    sin_up = jnp.where(first_half, -sin, 0.0)
    sin_dn = jnp.where(first_half, 0.0, sin)

    def place(tab, fill):
        blk = jnp.full((st.seq, period), fill, F32).at[:, lane_lo:lane_lo + dim].set(tab)
        blk = jnp.tile(blk, (1, LANES // period))
        return jnp.concatenate([blk, jnp.full((st.ctx, LANES), fill, F32)], axis=0)

    return place(cos, 1.0), place(sin_up, 0.0), place(sin_dn, 0.0)


def _qkv_pre_kernel(*refs, rope):
    if rope:
        x_ref, mod_ref, g_ref, w_ref, qkg_ref, cos_ref, su_ref, sd_ref, q_ref, k_ref, v_ref = refs
    else:
        x_ref, mod_ref, g_ref, w_ref, qkg_ref, q_ref, k_ref, v_ref = refs
    h = _modulate(x_ref[...], g_ref[...], mod_ref[3:4, :], mod_ref[4:5, :]).astype(BF16)
    qkv = _dot(h, w_ref[...])
    dh = DIFF_DH
    segments = ((0, dh), (dh, 2 * dh))
    for t, (dst, post) in enumerate(((q_ref, dh ** -0.5 * LOG2E), (k_ref, 1.0))):
        gain = qkg_ref[t:t + 1, :]
        for c in range(D_MODEL // LANES):
            xb = qkv[:, t * D_MODEL + c * LANES:t * D_MODEL + (c + 1) * LANES]
            y = xb * _segment_rsqrt(xb, segments) * gain
            if rope:
                y = _rotate(y, cos_ref[...], su_ref[...], sd_ref[...], dh // 4)
            dst[:, c * LANES:(c + 1) * LANES] = (y * post).astype(BF16)
    v_ref[...] = qkv[:, 2 * D_MODEL:].astype(BF16)


def _qkv_pre(st, xs, mod, g, w_qkv, qk_gain, rope_tabs):
    rope = rope_tabs is not None
    tab_spec = pl.BlockSpec((TOKEN_TILE, LANES), lambda b, j: (j, 0))
    in_specs = [st.tile_spec(D_MODEL), st.mod_spec(), _resident((1, D_MODEL)),
                _resident((D_MODEL, 3 * D_MODEL)), _resident((2, LANES))]
    args = [xs, mod, g, w_qkv, jnp.tile(qk_gain, (1, 2))]
    if rope:
        in_specs += [tab_spec] * 3
        args += list(rope_tabs)
    shape = jax.ShapeDtypeStruct((st.batch, st.tokens, D_MODEL), BF16)
    return pl.pallas_call(
        functools.partial(_qkv_pre_kernel, rope=rope),
        grid=(st.batch, st.tiles),
        in_specs=in_specs,
        out_specs=[st.tile_spec(D_MODEL)] * 3,
        out_shape=[shape] * 3,
        compiler_params=_params(2),
        name="qkv_pre",
    )(*args)


DIFF_HEADS_PER_STEP = 4


def _attn_query_branches(st_seq, attend):
    j = pl.program_id(2)
    lat_steps = st_seq // ATTN_TILE

    @pl.when(j < lat_steps)
    def _():
        attend(slice(None), slice(None))

    @pl.when(j >= lat_steps)
    def _():
        attend(slice(0, TOKEN_TILE), slice(st_seq, None))


def _diff_attn_kernel(q_ref, k_ref, v_ref, lam_ref, subg_ref, o_ref, *, seq, lam_init):
    lp = lam_ref[...]
    lam = (jnp.exp(jnp.sum(lp[0:1] * lp[1:2], axis=-1, keepdims=True))
           - jnp.exp(jnp.sum(lp[2:3] * lp[3:4], axis=-1, keepdims=True)) + lam_init)

    def attend(query_rows, key_rows):
        def logits(hd):
            cols = slice(hd * LANES, (hd + 1) * LANES)
            q, k = q_ref[query_rows, cols], k_ref[key_rows, cols]
            return [_dot_t(jnp.where(_lane_mask(c * DIFF_DH, (c + 1) * DIFF_DH), q, jnp.zeros_like(q)), k)
                    for c in range(2)]

        s_next = logits(0)
        for hd in range(DIFF_HEADS_PER_STEP):
            cols = slice(hd * LANES, (hd + 1) * LANES)
            s_cur = s_next
            if hd + 1 < DIFF_HEADS_PER_STEP:
                s_next = logits(hd + 1)
            probs = []
            for s in s_cur:
                p = jnp.exp2(s - jnp.max(s, axis=-1, keepdims=True))
                probs.append((p, 1.0 / jnp.sum(p, axis=-1, keepdims=True)))
            (p0, r0), (p1, r1) = probs
            a = (p0 * r0 - p1 * (lam * r1)).astype(BF16)
            o = _dot(a, v_ref[key_rows, cols])
            ms = jnp.mean(o * o, axis=-1, keepdims=True)
            o_ref[query_rows, cols] = ((o * lax.rsqrt(ms + EPS)) * subg_ref[...] * (1.0 - lam_init)).astype(BF16)

    _attn_query_branches(seq, attend)


def _diff_attn(st, q, k, v, lam_p, sub_g, lam_init):
    width = DIFF_HEADS_PER_STEP * LANES
    head = lambda b, h, j: (b, j, h)
    keys = lambda b, h, j: (b, 0, h)
    kern = functools.partial(_diff_attn_kernel, seq=st.seq, lam_init=lam_init)
    return pl.pallas_call(
        kern,
        grid=(st.batch, DIFF_HEADS // DIFF_HEADS_PER_STEP, st.attn_steps),
        in_specs=[pl.BlockSpec((None, ATTN_TILE, width), head),
                  pl.BlockSpec((None, st.tokens, width), keys),
                  pl.BlockSpec((None, st.tokens, width), keys),
                  _resident((4, DIFF_DH)), _resident((1, LANES))],
        out_specs=pl.BlockSpec((None, ATTN_TILE, width), head),
        out_shape=jax.ShapeDtypeStruct((st.batch, st.tokens, D_MODEL), BF16),
        compiler_params=_params(3),
        name="diff_attn",
    )(q, k, v, lam_p, sub_g.reshape(1, LANES))


MLA_SLOT_SEGMENTS = ((0, MLA_NOPE), (MLA_NOPE, MLA_NOPE + MLA_ROPE))
MLA_DOWN = MLA_Q_RANK + MLA_KV_RANK + LANES


def _mla_pre_kernel(x_ref, mod_ref, g_ref, wd_ref, qg_ref, kvg_ref, wuq_ref, wukv_ref, qkg_ref,
                    cos_ref, su_ref, sd_ref, q_ref, k_ref, v_ref):
    h = _modulate(x_ref[...], g_ref[...], mod_ref[3:4, :], mod_ref[4:5, :]).astype(BF16)
    down = _dot(h, wd_ref[...])
    cos, su, sd = cos_ref[...], su_ref[...], sd_ref[...]

    def full_norm(z, gain):
        ms = jnp.mean(z * z, axis=-1, keepdims=True)
        return ((z * lax.rsqrt(ms + EPS)) * gain).astype(BF16)

    q = _dot(full_norm(down[:, :MLA_Q_RANK], qg_ref[...]), wuq_ref[...])
    kv = _dot(full_norm(down[:, MLA_Q_RANK:MLA_Q_RANK + MLA_KV_RANK], kvg_ref[...]), wukv_ref[...])
    q_gain, k_gain = qkg_ref[0:1, :], qkg_ref[1:2, :]
    scale = (MLA_NOPE + MLA_ROPE) ** -0.5 * LOG2E
    kpe = down[:, MLA_Q_RANK + MLA_KV_RANK:]
    kpe = kpe * _segment_rsqrt(kpe, MLA_SLOT_SEGMENTS) * k_gain
    kpe = _rotate(kpe, cos, su, sd, MLA_ROPE // 4)
    rope_lanes = _lane_mask(MLA_NOPE, MLA_NOPE + MLA_ROPE)
    for hd in range(MLA_HEADS):
        cols = slice(hd * LANES, (hd + 1) * LANES)
        qb = q[:, cols]
        qb = qb * _segment_rsqrt(qb, MLA_SLOT_SEGMENTS) * q_gain
        q_ref[:, cols] = (_rotate(qb, cos, su, sd, MLA_ROPE // 4) * scale).astype(BF16)
        kb = kv[:, cols]
        kb = kb * _segment_rsqrt(kb, MLA_SLOT_SEGMENTS) * k_gain
        k_ref[:, cols] = jnp.where(rope_lanes, kpe, kb).astype(BF16)
    v_ref[...] = kv[:, MLA_HEADS * LANES:].astype(BF16)


def _mla_pre(st, xs, mod, g, w_down, q_g, kv_g, w_uq, w_ukv, qk_gain, rope_tabs):
    tab_spec = pl.BlockSpec((TOKEN_TILE, LANES), lambda b, j: (j, 0))
    slots = MLA_HEADS * LANES
    return pl.pallas_call(
        _mla_pre_kernel,
        grid=(st.batch, st.tiles),
        in_specs=[st.tile_spec(D_MODEL), st.mod_spec(), _resident((1, D_MODEL)),
                  _resident((D_MODEL, MLA_DOWN)), _resident((1, MLA_Q_RANK)), _resident((1, MLA_KV_RANK)),
                  _resident((MLA_Q_RANK, slots)), _resident((MLA_KV_RANK, slots + D_MODEL)),
                  _resident((2, LANES)), tab_spec, tab_spec, tab_spec],
        out_specs=[st.tile_spec(slots), st.tile_spec(slots), st.tile_spec(D_MODEL)],
        out_shape=[jax.ShapeDtypeStruct((st.batch, st.tokens, slots), BF16),
                   jax.ShapeDtypeStruct((st.batch, st.tokens, slots), BF16),
                   jax.ShapeDtypeStruct((st.batch, st.tokens, D_MODEL), BF16)],
        compiler_params=_params(2),
        name="mla_pre",
    )(xs, mod, g, w_down, q_g, kv_g, w_uq, w_ukv, qk_gain, *rope_tabs)


MLA_PAIRS_PER_STEP = 4


def _mla_attn_kernel(q_ref, k_ref, v_ref, o_ref, *, seq):
    def attend(query_rows, key_rows):
        def logits(head):
            slot = slice(head * LANES, (head + 1) * LANES)
            return _dot_t(q_ref[query_rows, slot], k_ref[key_rows, slot])

        s_next = logits(0)
        for pair in range(MLA_PAIRS_PER_STEP):
            v = v_ref[key_rows, pair * LANES:(pair + 1) * LANES]
            out = None
            for hh in range(2):
                head = 2 * pair + hh
                s = s_next
                if head + 1 < 2 * MLA_PAIRS_PER_STEP:
                    s_next = logits(head + 1)
                p = jnp.exp2(s - jnp.max(s, axis=-1, keepdims=True))
                r = 1.0 / jnp.sum(p, axis=-1, keepdims=True)
                vh = jnp.where(_lane_mask(hh * MLA_V, (hh + 1) * MLA_V), v, jnp.zeros_like(v))
                o = _dot(p.astype(BF16), vh) * r
                out = o if out is None else out + o
            o_ref[query_rows, pair * LANES:(pair + 1) * LANES] = out.astype(BF16)

    _attn_query_branches(seq, attend)


def _mla_attn(st, q, k, v):
    slots = 2 * MLA_PAIRS_PER_STEP * LANES
    vals = MLA_PAIRS_PER_STEP * LANES
    kern = functools.partial(_mla_attn_kernel, seq=st.seq)
    return pl.pallas_call(
        kern,
        grid=(st.batch, MLA_HEADS // (2 * MLA_PAIRS_PER_STEP), st.attn_steps),
        in_specs=[pl.BlockSpec((None, ATTN_TILE, slots), lambda b, h, j: (b, j, h)),
                  pl.BlockSpec((None, st.tokens, slots), lambda b, h, j: (b, 0, h)),
                  pl.BlockSpec((None, st.tokens, vals), lambda b, h, j: (b, 0, h))],
        out_specs=pl.BlockSpec((None, ATTN_TILE, vals), lambda b, h, j: (b, j, h)),
        out_shape=jax.ShapeDtypeStruct((st.batch, st.tokens, D_MODEL), BF16),
        compiler_params=_params(3),
        name="mla_attn",
    )(q, k, v)


NA_QROWS = TOKEN_TILE // GRID_W
NA_WROWS = NA_QROWS + NA_KH
NA_PAIRS = NA_WROWS // 2
NA_BIAS_LO = -(NA_QROWS - 1) - (NA_WROWS - NA_QROWS - NA_KH // 2) + NA_KH - 1
NA_BIAS_N = (NA_WROWS - 2) + NA_KH - 1 - NA_BIAS_LO + 1


def _na_attn_kernel(q_ref, k_ref, v_ref, bias_ref, o_ref, mask_ref, *, rows, seq):
    r0 = pl.program_id(1) * NA_QROWS
    w0 = jnp.clip(r0 - NA_KH // 2, 0, rows - NA_WROWS)
    start = pl.multiple_of(w0 * GRID_W, GRID_W)
    n_loc = NA_WROWS * GRID_W
    qidx = lax.broadcasted_iota(jnp.int32, (TOKEN_TILE, n_loc), 0)
    kidx = lax.broadcasted_iota(jnp.int32, (TOKEN_TILE, n_loc), 1)
    q_row, q_col = r0 + qidx // GRID_W, qidx % GRID_W
    k_row, k_col = w0 + kidx // GRID_W, kidx % GRID_W
    row_start = jnp.clip(q_row - NA_KH // 2, 0, rows - NA_KH)
    col_start = jnp.clip(q_col - NA_KW // 2, 0, GRID_W - NA_KW)
    valid = ((k_row >= row_start) & (k_row < row_start + NA_KH)
             & (k_col >= col_start) & (k_col < col_start + NA_KW))
    mask_ref[...] = valid.astype(jnp.int32)
    base = w0 - r0 + NA_KH - 1 - NA_BIAS_LO
    for pair in range(NA_HEADS // 2):
        cols = slice(pair * LANES, (pair + 1) * LANES)
        q = q_ref[:, cols]
        k_loc, v_loc = k_ref[pl.ds(start, n_loc), cols], v_ref[pl.ds(start, n_loc), cols]
        k_ctx, v_ctx = k_ref[seq:, cols], v_ref[seq:, cols]
        out = None
        for hh in range(2):
            head = 2 * pair + hh
            head_lanes = _lane_mask(hh * NA_DH, (hh + 1) * NA_DH)
            qh = jnp.where(head_lanes, q, jnp.zeros_like(q))
            bias = jnp.concatenate(
                [jnp.concatenate([bias_ref[head, base + 2 * p - a] for p in range(NA_PAIRS)], axis=1)
                 for a in range(NA_QROWS)], axis=0)
            s_loc = jnp.where(mask_ref[...] != 0, _dot_t(qh, k_loc) + bias, NEG_INF)
            s_ctx = _dot_t(qh, k_ctx)
            m = jnp.maximum(jnp.max(s_loc, axis=-1, keepdims=True), jnp.max(s_ctx, axis=-1, keepdims=True))
            p_loc, p_ctx = jnp.exp2(s_loc - m), jnp.exp2(s_ctx - m)
            rcp = 1.0 / (jnp.sum(p_loc, axis=-1, keepdims=True) + jnp.sum(p_ctx, axis=-1, keepdims=True))
            o = (_dot(p_loc.astype(BF16), jnp.where(head_lanes, v_loc, jnp.zeros_like(v_loc)))
                 + _dot(p_ctx.astype(BF16), jnp.where(head_lanes, v_ctx, jnp.zeros_like(v_ctx)))) * rcp
            out = o if out is None else out + o
        o_ref[:, cols] = out.astype(BF16)


def _na_bias_table(rpb):
    n_rel, n_col = 2 * NA_KH - 1, 2 * NA_KW - 1
    left = GRID_W - NA_KW
    v = jnp.pad(rpb * LOG2E, ((0, 0), (0, 0), (left, 2 * GRID_W - left - n_col)))
    v = jnp.tile(v, (1, 1, GRID_W))[:, :, :GRID_W * (2 * GRID_W - 1)]
    toep = v.reshape(NA_HEADS, n_rel, GRID_W, 2 * GRID_W - 1)[..., GRID_W - 1:]
    toep = jnp.pad(toep, ((0, 0), (-NA_BIAS_LO, NA_BIAS_LO + NA_BIAS_N + 1 - n_rel), (0, 0), (0, 0)))
    return jnp.concatenate([toep[:, :-1], toep[:, 1:]], axis=-1)


def _na_attn(st, q, k, v, bias_tab):
    rows = st.seq // GRID_W
    assert rows >= NA_WROWS and rows % NA_QROWS == 0
    kern = functools.partial(_na_attn_kernel, rows=rows, seq=st.seq)
    return pl.pallas_call(
        kern,
        grid=(st.batch, st.lat_tiles),
        in_specs=[st.tile_spec(D_MODEL),
                  pl.BlockSpec((None, st.tokens, D_MODEL), lambda b, j: (b, 0, 0)),
                  pl.BlockSpec((None, st.tokens, D_MODEL), lambda b, j: (b, 0, 0)),
                  _resident((NA_HEADS, NA_BIAS_N, GRID_W, 2 * GRID_W))],
        out_specs=st.tile_spec(D_MODEL),
        out_shape=jax.ShapeDtypeStruct((st.batch, st.seq, D_MODEL), BF16),
        scratch_shapes=[pltpu.VMEM((TOKEN_TILE, NA_WROWS * GRID_W), jnp.int32)],
        compiler_params=_params(2),
        name="na_attn",
    )(q, k, v, bias_tab)


def _slot_columns(w, width):
    kdim = w.shape[0]
    w = w.reshape(kdim, -1, width)
    return jnp.pad(w, ((0, 0), (0, 0), (0, LANES - width))).reshape(kdim, -1)


def _slot_gain(g):
    return jnp.pad(g, ((0, 0), (0, LANES - g.shape[1])))


def kernel(x, c, ctx, c_ctx, ada_w, ada_b, norm_g, ffn_w_in, ffn_w_out, pool_w, pool_b, pool_scale,
           diff_w_qkv, diff_qk_g, diff_lambda, diff_sub_g, diff_w_o,
           mla_w_dq, mla_q_g, mla_w_uq, mla_w_dkv, mla_kv_g, mla_w_ukv, mla_qk_g, mla_w_o,
           na_w_qkv, na_qk_g, na_rpb, na_w_o):
    assert DEPTH == N_MIXERS, "one layer per mixer: the context stream ends after the last softmax mixer's keys"
    batch, seq, _ = x.shape
    st = _Stream(batch, seq, ctx.shape[1])
    xs = jnp.concatenate([x, ctx], axis=1)

    rows = -(-(batch + 1) // 16) * 16
    cc = jnp.zeros((rows, D_MODEL), F32).at[:batch].set(c).at[batch].set(c_ctx)
    mods = _adaln(cc, ada_w, ada_b)

    w_in = ffn_w_in.astype(BF16)
    w_out = ffn_w_out.astype(BF16)
    gains = norm_g.reshape(DEPTH, 3, 1, D_MODEL)

    for i in range(DEPTH):
        mod = mods[i]
        xs = _ffn(st, xs, mod, 0, gains[i, 0], w_in[i, 0], w_out[i, 0])
        if i == 0:
            xs = _pool(st, xs, mod, gains[i, 1], pool_w[0].astype(BF16), pool_b[0].reshape(1, D_MODEL),
                       pool_scale[0].reshape(1, D_MODEL))
            xs = _ffn(st, xs, mod, 2, gains[i, 2], w_in[i, 1], w_out[i, 1])
            continue
        if i == 1:
            tabs = _rope_tables(st, DIFF_DH, 0, DIFF_DH)
            q, k, v = _qkv_pre(st, xs, mod, gains[i, 1], diff_w_qkv[0].astype(BF16), diff_qk_g[0], tabs)
            lam_init = 0.8 - 0.6 * math.exp(-0.3 * i)
            attn = _diff_attn(st, q, k, v, diff_lambda[0], diff_sub_g[0], lam_init)
            w_o = diff_w_o[0]
        elif i == 2:
            tabs = _rope_tables(st, MLA_ROPE, MLA_NOPE, LANES)
            kpe_cols = jnp.pad(mla_w_dkv[0][:, MLA_KV_RANK:], ((0, 0), (MLA_NOPE, LANES - MLA_NOPE - MLA_ROPE)))
            w_down = jnp.concatenate([mla_w_dq[0], mla_w_dkv[0][:, :MLA_KV_RANK], kpe_cols], axis=1)
            ukv = mla_w_ukv[0].reshape(MLA_KV_RANK, MLA_HEADS, MLA_NOPE + MLA_V)
            w_ukv = jnp.concatenate([_slot_columns(ukv[:, :, :MLA_NOPE].reshape(MLA_KV_RANK, -1), MLA_NOPE),
                                     ukv[:, :, MLA_NOPE:].reshape(MLA_KV_RANK, -1)], axis=1)
            q, k, v = _mla_pre(st, xs, mod, gains[i, 1], w_down.astype(BF16),
                               mla_q_g[0].reshape(1, -1), mla_kv_g[0].reshape(1, -1),
                               _slot_columns(mla_w_uq[0], MLA_NOPE + MLA_ROPE).astype(BF16),
                               w_ukv.astype(BF16), _slot_gain(mla_qk_g[0]), tabs)
            attn = _mla_attn(st, q, k, v)
            w_o = mla_w_o[0]
        else:
            q, k, v = _qkv_pre(st, xs, mod, gains[i, 1], na_w_qkv[0].astype(BF16), na_qk_g[0], None)
            attn = _na_attn(st, q, k, v, _na_bias_table(na_rpb[0]))
            w_o = na_w_o[0]
        xs = _mixout_ffn(st, xs, attn, mod, gains[i, 2], w_o.astype(BF16), w_in[i, 1], w_out[i, 1],
                         latent_only=(i == DEPTH - 1))
    return xs
```

```python
import functools
import math

import jax
import jax.numpy as jnp
from jax import lax
from jax.experimental import pallas as pl
from jax.experimental.pallas import tpu as pltpu

D_MODEL = 1024
DEPTH = 4
GRID_W = 64
N_MIXERS = 4
N_MOD = 9
EPS = 1e-6
LOG2E = math.log2(math.e)
ROPE_BASE = 10000.0
NEG_INF = -1e30
D_FF = 2816
POOL_WINDOWS = (2, 4, 8, 16)
POOL_G = D_MODEL // len(POOL_WINDOWS)
DIFF_HEADS = 8
DIFF_DH = 64
MLA_HEADS = 16
MLA_Q_RANK = 384
MLA_KV_RANK = 256
MLA_NOPE = 64
MLA_ROPE = 32
MLA_V = 64
NA_HEADS = 16
NA_DH = 64
NA_KH = 8
NA_KW = 16

LANES = 128
HALO = 8
TOKEN_TILE = 256
ATTN_TILE = 512
VMEM_LIMIT = 56 * 2**20

F32 = jnp.float32
BF16 = jnp.bfloat16


def _params(n_axes):
    return pltpu.CompilerParams(dimension_semantics=("arbitrary",) * n_axes,
                                vmem_limit_bytes=VMEM_LIMIT)


def _resident(shape):
    zeros = (0,) * len(shape)
    return pl.BlockSpec(shape, lambda *_: zeros, pipeline_mode=pl.Buffered(1))


def _dot(a, b):
    return jnp.dot(a, b, preferred_element_type=F32)


def _dot_t(a, b):
    return lax.dot_general(a, b, (((1,), (1,)), ((), ())), preferred_element_type=F32)


def _modulate(x, g, shift, scale):
    ms = jnp.mean(x * x, axis=-1, keepdims=True)
    return (x * lax.rsqrt(ms + EPS)) * (g * (1.0 + scale)) + shift


def _silu(x):
    return x * (1.0 / (1.0 + jnp.exp(-x)))


def _adaln_kernel(c_ref, w_ref, b_ref, o_ref):
    s = _silu(c_ref[...]).astype(BF16)
    o_ref[...] = _dot(s, w_ref[...].astype(BF16)) + b_ref[...]


def _adaln(cc, ada_w, ada_b):
    rows = cc.shape[0]
    n_col = N_MOD * D_MODEL // D_MODEL
    out = pl.pallas_call(
        _adaln_kernel,
        grid=(DEPTH, n_col),
        in_specs=[pl.BlockSpec((rows, D_MODEL), lambda i, n: (0, 0)),
                  pl.BlockSpec((None, D_MODEL, D_MODEL), lambda i, n: (i, 0, n)),
                  pl.BlockSpec((None, 1, D_MODEL), lambda i, n: (i, 0, n))],
        out_specs=pl.BlockSpec((None, rows, D_MODEL), lambda i, n: (i, 0, n)),
        out_shape=jax.ShapeDtypeStruct((DEPTH, rows, N_MOD * D_MODEL), F32),
        compiler_params=_params(2),
        name="adaln",
    )(cc, ada_w, ada_b.reshape(DEPTH, 1, N_MOD * D_MODEL))
    return out.reshape(DEPTH, rows, N_MOD, D_MODEL)


def _ffn_core(x, mod_ref, k, g_ref, win_ref, wout_ref):
    shift = mod_ref[3 * k:3 * k + 1, :]
    scale = mod_ref[3 * k + 1:3 * k + 2, :]
    gate = mod_ref[3 * k + 2:3 * k + 3, :]
    h = _modulate(x, g_ref[...], shift, scale).astype(BF16)
    u = _dot(h, win_ref[...])
    a = (_silu(u[:, :D_FF]) * u[:, D_FF:]).astype(BF16)
    return x + (0.5 * gate) * _dot(a, wout_ref[...])


def _wide_tile_branches(axis, seq, body, with_ctx=True):
    j = pl.program_id(axis)
    lat_steps = seq // ATTN_TILE

    @pl.when(j < lat_steps)
    def _():
        body(slice(None), slice(None))

    if with_ctx:
        @pl.when(j >= lat_steps)
        def _():
            body(slice(0, TOKEN_TILE), slice(seq, None))


def _ffn_kernel(x_ref, mod_ref, g_ref, win_ref, wout_ref, o_ref, *, k, seq):
    def step(rows, _):
        o_ref[rows, :] = _ffn_core(x_ref[rows, :], mod_ref, k, g_ref, win_ref, wout_ref)

    _wide_tile_branches(1, seq, step)


def _mixout_ffn_kernel(x_ref, a_ref, mod_ref, g_ref, wo_ref, win_ref, wout_ref, o_ref, *, seq, with_ctx):
    def step(rows, _):
        x = x_ref[rows, :] + mod_ref[5:6, :] * _dot(a_ref[rows, :], wo_ref[...])
        o_ref[rows, :] = _ffn_core(x, mod_ref, 2, g_ref, win_ref, wout_ref)

    _wide_tile_branches(1, seq, step, with_ctx)


class _Stream:
    def __init__(self, batch, seq, ctx):
        assert seq % TOKEN_TILE == 0 and ctx == TOKEN_TILE and seq % GRID_W == 0
        self.batch, self.seq, self.ctx = batch, seq, ctx
        self.tokens = seq + ctx
        self.lat_tiles = seq // TOKEN_TILE
        self.tiles = self.tokens // TOKEN_TILE
        assert seq % ATTN_TILE == 0 and ATTN_TILE >= ctx
        self.attn_steps = seq // ATTN_TILE + 1

    def tile_spec(self, width):
        return pl.BlockSpec((None, TOKEN_TILE, width), lambda b, j: (b, j, 0))

    def wide_spec(self, width):
        return pl.BlockSpec((None, ATTN_TILE, width), lambda b, j: (b, j, 0))

    def mod_spec(self, rows=TOKEN_TILE):
        lat, ctx_row = self.seq // rows, self.batch
        return pl.BlockSpec((None, N_MOD, D_MODEL),
                            lambda b, j: (jnp.where(j < lat, b, ctx_row), 0, 0))


def _ffn(st, xs, mod, k, g, w_in, w_out):
    return pl.pallas_call(
        functools.partial(_ffn_kernel, k=k, seq=st.seq),
        grid=(st.batch, st.attn_steps),
        in_specs=[st.wide_spec(D_MODEL), st.mod_spec(ATTN_TILE), _resident((1, D_MODEL)),
                  _resident((D_MODEL, 2 * D_FF)), _resident((D_FF, D_MODEL))],
        out_specs=st.wide_spec(D_MODEL),
        out_shape=jax.ShapeDtypeStruct((st.batch, st.tokens, D_MODEL), F32),
        compiler_params=_params(2),
        name="ffn",
    )(xs, mod, g, w_in, w_out)


def _mixout_ffn(st, xs, attn, mod, g, w_o, w_in, w_out, latent_only):
    steps = st.attn_steps - 1 if latent_only else st.attn_steps
    tokens = st.seq if latent_only else st.tokens
    return pl.pallas_call(
        functools.partial(_mixout_ffn_kernel, seq=st.seq, with_ctx=not latent_only),
        grid=(st.batch, steps),
        in_specs=[st.wide_spec(D_MODEL), st.wide_spec(D_MODEL), st.mod_spec(ATTN_TILE),
                  _resident((1, D_MODEL)), _resident((D_MODEL, D_MODEL)),
                  _resident((D_MODEL, 2 * D_FF)), _resident((D_FF, D_MODEL))],
        out_specs=st.wide_spec(D_MODEL),
        out_shape=jax.ShapeDtypeStruct((st.batch, tokens, D_MODEL), F32),
        compiler_params=_params(2),
        name="mixout_ffn",
    )(xs, attn, mod, g, w_o, w_in, w_out)


def _pool_kernel(x_ref, xp_ref, xn_ref, mod_ref, g_ref, w_ref, b_ref, s_ref, o_ref, hext_ref,
                 *, lat_tiles, tiles, seq, ctx):
    j = pl.program_id(1)
    g, shift, scale = g_ref[...], mod_ref[3:4, :], mod_ref[4:5, :]
    x = x_ref[...]
    h = _modulate(x, g, shift, scale)
    has_prev = jnp.logical_and(j != 0, j != lat_tiles)
    has_next = jnp.logical_and(j != lat_tiles - 1, j != tiles - 1)
    hext_ref[0:HALO, :] = jnp.where(has_prev, _modulate(xp_ref[...], g, shift, scale), 0.0)
    hext_ref[HALO:HALO + TOKEN_TILE, :] = h
    hext_ref[HALO + TOKEN_TILE:, :] = jnp.where(has_next, _modulate(xn_ref[...], g, shift, scale), 0.0)
    in_lat = j < lat_tiles
    t = (j - jnp.where(in_lat, 0, lat_tiles)) * TOKEN_TILE + lax.broadcasted_iota(jnp.int32, (TOKEN_TILE, 1), 0)
    n = jnp.where(in_lat, seq, ctx)
    for gi, win in enumerate(POOL_WINDOWS):
        cols = slice(gi * POOL_G, (gi + 1) * POOL_G)
        acc = hext_ref[HALO - win // 2:HALO - win // 2 + TOKEN_TILE, cols]
        for k in range(1 - win // 2, win - win // 2):
            acc = acc + hext_ref[HALO + k:HALO + k + TOKEN_TILE, cols]
        lo = jnp.maximum(t - win // 2, 0)
        hi = jnp.minimum(t - win // 2 + win, n)
        mean = acc / (hi - lo).astype(F32)
        d = (mean - h[:, cols]).astype(BF16)
        y = (_dot(d, w_ref[gi]) + b_ref[:, cols]) * s_ref[:, cols]
        o_ref[:, cols] = x[:, cols] + mod_ref[5:6, cols] * y


def _pool(st, xs, mod, g, w, b, scale):
    per_tile = TOKEN_TILE // HALO
    last = st.tokens // HALO - 1
    kern = functools.partial(_pool_kernel, lat_tiles=st.lat_tiles, tiles=st.tiles, seq=st.seq, ctx=st.ctx)
    return pl.pallas_call(
        kern,
        grid=(st.batch, st.tiles),
        in_specs=[st.tile_spec(D_MODEL),
                  pl.BlockSpec((None, HALO, D_MODEL), lambda b_, j: (b_, jnp.maximum(j * per_tile - 1, 0), 0)),
                  pl.BlockSpec((None, HALO, D_MODEL), lambda b_, j: (b_, jnp.minimum((j + 1) * per_tile, last), 0)),
                  st.mod_spec(), _resident((1, D_MODEL)),
                  _resident((len(POOL_WINDOWS), POOL_G, POOL_G)),
                  _resident((1, D_MODEL)), _resident((1, D_MODEL))],
        out_specs=st.tile_spec(D_MODEL),
        out_shape=jax.ShapeDtypeStruct((st.batch, st.tokens, D_MODEL), F32),
        scratch_shapes=[pltpu.VMEM((TOKEN_TILE + 2 * HALO, D_MODEL), F32)],
        compiler_params=_params(2),
        name="pool",
    )(xs, xs, xs, mod, g, w, b, scale)


def _lane_mask(lo, hi):
    lane = lax.broadcasted_iota(jnp.int32, (1, LANES), 1)
    return jnp.logical_and(lane >= lo, lane < hi)


def _segment_mean_matrix(lengths):
    assert sum(lengths) == LANES and all(n & (n - 1) == 0 for n in lengths)
    ids = jnp.repeat(jnp.arange(len(lengths)), jnp.array(lengths), total_repeat_length=LANES)
    inv = 1.0 / jnp.array(lengths, F32)[ids]
    m = jnp.where(ids[:, None] == ids[None, :], inv[None, :], 0.0)
    return jnp.tile(m, (2, 1)).astype(BF16)


def _segment_inv_rms(x, seg_ref):
    sq = x * x
    hi = sq.astype(BF16)
    lo = (sq - hi.astype(F32)).astype(BF16)
    return lax.rsqrt(_dot(jnp.concatenate([hi, lo], axis=1), seg_ref[...]) + EPS)


def _halves_inv_rms(x):
    sq = x * x
    lower = _lane_mask(0, LANES // 2)
    low = jnp.sum(jnp.where(lower, sq, 0.0), axis=-1, keepdims=True)
    high = jnp.sum(jnp.where(lower, 0.0, sq), axis=-1, keepdims=True)
    inv_n = 2.0 / LANES
    return jnp.where(lower, lax.rsqrt(low * inv_n + EPS), lax.rsqrt(high * inv_n + EPS))


def _rotate(z, tabs, shift):
    c_ref, u_ref, d_ref = tabs
    return (z * c_ref[...] + pltpu.roll(z, LANES - shift, axis=1) * u_ref[...]
            + pltpu.roll(z, shift, axis=1) * d_ref[...])


def _rope_tables(st, dim, lane_lo, period, gain, post):
    nf = dim // 4
    inv = ROPE_BASE ** (-jnp.arange(nf, dtype=F32) / nf)
    t = jnp.arange(st.seq, dtype=jnp.int32)
    pos = jnp.stack([t // GRID_W, t % GRID_W], axis=-1).astype(F32)
    ang = jnp.broadcast_to(pos[:, :, None, None] * inv, (st.seq, 2, 2, nf)).reshape(st.seq, dim)
    cos, sin = jnp.cos(ang), jnp.sin(ang)
    first_half = (jnp.arange(dim) % (2 * nf)) < nf
    sin_up = jnp.where(first_half, -sin, 0.0)
    sin_dn = jnp.where(first_half, 0.0, sin)

    def place(tab, fill, lane_gain):
        blk = jnp.full((st.seq, period), fill, F32).at[:, lane_lo:lane_lo + dim].set(tab)
        blk = jnp.tile(blk, (1, LANES // period))
        return jnp.concatenate([blk, jnp.full((st.ctx, LANES), fill, F32)], axis=0) * (lane_gain * post)

    return (place(cos, 1.0, gain), place(sin_up, 0.0, jnp.roll(gain, -nf)), place(sin_dn, 0.0, jnp.roll(gain, nf)))


def _qkv_pre_kernel(*refs, rope):
    if rope:
        x_ref, mod_ref, g_ref, w_ref, seg_ref, *tabs, q_ref, k_ref, v_ref = refs
    else:
        x_ref, mod_ref, g_ref, w_ref, gain_ref, q_ref, k_ref, v_ref = refs
    h = _modulate(x_ref[...], g_ref[...], mod_ref[3:4, :], mod_ref[4:5, :]).astype(BF16)
    qkv = _dot(h, w_ref[...])
    for t, dst in enumerate((q_ref, k_ref)):
        for c in range(D_MODEL // LANES):
            xb = qkv[:, t * D_MODEL + c * LANES:t * D_MODEL + (c + 1) * LANES]
            if rope:
                y = _rotate(xb * _segment_inv_rms(xb, seg_ref), tabs[3 * t:3 * t + 3], DIFF_DH // 4)
            else:
                y = xb * _halves_inv_rms(xb) * gain_ref[t:t + 1, :]
            dst[:, c * LANES:(c + 1) * LANES] = y.astype(BF16)
    v_ref[...] = qkv[:, 2 * D_MODEL:].astype(BF16)


def _qkv_pre(st, xs, mod, g, w_qkv, qk_gain, rope):
    tab_spec = pl.BlockSpec((TOKEN_TILE, LANES), lambda b, j: (j, 0))
    in_specs = [st.tile_spec(D_MODEL), st.mod_spec(), _resident((1, D_MODEL)), _resident((D_MODEL, 3 * D_MODEL))]
    args = [xs, mod, g, w_qkv]
    lane_gain = jnp.tile(qk_gain, (1, 2))
    post = (DIFF_DH ** -0.5 * LOG2E, 1.0)
    if rope:
        in_specs += [_resident((2 * LANES, LANES))] + [tab_spec] * 6
        args += [_segment_mean_matrix((DIFF_DH, DIFF_DH))]
        for t in range(2):
            args += list(_rope_tables(st, DIFF_DH, 0, DIFF_DH, lane_gain[t], post[t]))
    else:
        in_specs += [_resident((2, LANES))]
        args += [lane_gain * jnp.array(post, F32)[:, None]]
    shape = jax.ShapeDtypeStruct((st.batch, st.tokens, D_MODEL), BF16)
    return pl.pallas_call(
        functools.partial(_qkv_pre_kernel, rope=rope),
        grid=(st.batch, st.tiles),
        in_specs=in_specs,
        out_specs=[st.tile_spec(D_MODEL)] * 3,
        out_shape=[shape] * 3,
        compiler_params=_params(2),
        name="qkv_pre",
    )(*args)


DIFF_HEADS_PER_STEP = 4


def _diff_attn_kernel(q_ref, k_ref, v_ref, lam_ref, subg_ref, o_ref, *, seq, lam_init):
    lp = lam_ref[...]
    lam = (jnp.exp(jnp.sum(lp[0:1] * lp[1:2], axis=-1, keepdims=True))
           - jnp.exp(jnp.sum(lp[2:3] * lp[3:4], axis=-1, keepdims=True)) + lam_init)

    def attend(query_rows, key_rows):
        def logits(hd):
            cols = slice(hd * LANES, (hd + 1) * LANES)
            q, k = q_ref[query_rows, cols], k_ref[key_rows, cols]
            return [_dot_t(jnp.where(_lane_mask(c * DIFF_DH, (c + 1) * DIFF_DH), q, jnp.zeros_like(q)), k)
                    for c in range(2)]

        s_next = logits(0)
        for hd in range(DIFF_HEADS_PER_STEP):
            cols = slice(hd * LANES, (hd + 1) * LANES)
            s_cur = s_next
            if hd + 1 < DIFF_HEADS_PER_STEP:
                s_next = logits(hd + 1)
            probs = []
            for s in s_cur:
                p = jnp.exp2(s - jnp.max(s, axis=-1, keepdims=True))
                probs.append((p, 1.0 / jnp.sum(p, axis=-1, keepdims=True)))
            (p0, r0), (p1, r1) = probs
            a = (p0 * r0 - p1 * (lam * r1)).astype(BF16)
            o = _dot(a, v_ref[key_rows, cols])
            ms = jnp.mean(o * o, axis=-1, keepdims=True)
            o_ref[query_rows, cols] = ((o * lax.rsqrt(ms + EPS)) * subg_ref[...] * (1.0 - lam_init)).astype(BF16)

    _wide_tile_branches(2, seq, attend)


def _diff_attn(st, q, k, v, lam_p, sub_g, lam_init):
    width = DIFF_HEADS_PER_STEP * LANES
    head = lambda b, h, j: (b, j, h)
    keys = lambda b, h, j: (b, 0, h)
    kern = functools.partial(_diff_attn_kernel, seq=st.seq, lam_init=lam_init)
    return pl.pallas_call(
        kern,
        grid=(st.batch, DIFF_HEADS // DIFF_HEADS_PER_STEP, st.attn_steps),
        in_specs=[pl.BlockSpec((None, ATTN_TILE, width), head),
                  pl.BlockSpec((None, st.tokens, width), keys),
                  pl.BlockSpec((None, st.tokens, width), keys),
                  _resident((4, DIFF_DH)), _resident((1, LANES))],
        out_specs=pl.BlockSpec((None, ATTN_TILE, width), head),
        out_shape=jax.ShapeDtypeStruct((st.batch, st.tokens, D_MODEL), BF16),
        compiler_params=_params(3),
        name="diff_attn",
    )(q, k, v, lam_p, sub_g.reshape(1, LANES))


MLA_DOWN = MLA_Q_RANK + MLA_KV_RANK + LANES


def _mla_pre_kernel(x_ref, mod_ref, g_ref, wd_ref, qg_ref, kvg_ref, wuq_ref, wukv_ref, seg_ref, kgain_ref,
                    qc_ref, qu_ref, qd_ref, kc_ref, ku_ref, kd_ref, q_ref, k_ref, v_ref):
    h = _modulate(x_ref[...], g_ref[...], mod_ref[3:4, :], mod_ref[4:5, :]).astype(BF16)
    down = _dot(h, wd_ref[...])

    def full_norm(z, gain):
        ms = jnp.mean(z * z, axis=-1, keepdims=True)
        return ((z * lax.rsqrt(ms + EPS)) * gain).astype(BF16)

    q = _dot(full_norm(down[:, :MLA_Q_RANK], qg_ref[...]), wuq_ref[...])
    kv = _dot(full_norm(down[:, MLA_Q_RANK:MLA_Q_RANK + MLA_KV_RANK], kvg_ref[...]), wukv_ref[...])
    kpe = down[:, MLA_Q_RANK + MLA_KV_RANK:]
    kpe = _rotate(kpe * _segment_inv_rms(kpe, seg_ref), (kc_ref, ku_ref, kd_ref), MLA_ROPE // 4)
    for hd in range(MLA_HEADS):
        cols = slice(hd * LANES, (hd + 1) * LANES)
        qb = q[:, cols]
        q_ref[:, cols] = _rotate(qb * _segment_inv_rms(qb, seg_ref), (qc_ref, qu_ref, qd_ref),
                                 MLA_ROPE // 4).astype(BF16)
        kb = kv[:, cols]
        k_ref[:, cols] = (kb * _segment_inv_rms(kb, seg_ref) * kgain_ref[...] + kpe).astype(BF16)
    v_ref[...] = kv[:, MLA_HEADS * LANES:].astype(BF16)


def _mla_pre(st, xs, mod, g, w_down, q_g, kv_g, w_uq, w_ukv, qk_gain):
    tab_spec = pl.BlockSpec((TOKEN_TILE, LANES), lambda b, j: (j, 0))
    slots = MLA_HEADS * LANES
    scale = (MLA_NOPE + MLA_ROPE) ** -0.5 * LOG2E
    q_tabs = _rope_tables(st, MLA_ROPE, MLA_NOPE, LANES, qk_gain[0], scale)
    k_tabs = _rope_tables(st, MLA_ROPE, MLA_NOPE, LANES, qk_gain[1], 1.0)
    seg = _segment_mean_matrix((MLA_NOPE, MLA_ROPE, LANES - MLA_NOPE - MLA_ROPE))
    return pl.pallas_call(
        _mla_pre_kernel,
        grid=(st.batch, st.tiles),
        in_specs=[st.tile_spec(D_MODEL), st.mod_spec(), _resident((1, D_MODEL)),
                  _resident((D_MODEL, MLA_DOWN)), _resident((1, MLA_Q_RANK)), _resident((1, MLA_KV_RANK)),
                  _resident((MLA_Q_RANK, slots)), _resident((MLA_KV_RANK, slots + D_MODEL)),
                  _resident((2 * LANES, LANES)), _resident((1, LANES))] + [tab_spec] * 6,
        out_specs=[st.tile_spec(slots), st.tile_spec(slots), st.tile_spec(D_MODEL)],
        out_shape=[jax.ShapeDtypeStruct((st.batch, st.tokens, slots), BF16),
                   jax.ShapeDtypeStruct((st.batch, st.tokens, slots), BF16),
                   jax.ShapeDtypeStruct((st.batch, st.tokens, D_MODEL), BF16)],
        compiler_params=_params(2),
        name="mla_pre",
    )(xs, mod, g, w_down, q_g, kv_g, w_uq, w_ukv, seg, qk_gain[1:2], *q_tabs, *k_tabs)


MLA_PAIRS_PER_STEP = 4


def _mla_attn_kernel(q_ref, k_ref, v_ref, o_ref, *, seq):
    def attend(query_rows, key_rows):
        def logits(head):
            slot = slice(head * LANES, (head + 1) * LANES)
            return _dot_t(q_ref[query_rows, slot], k_ref[key_rows, slot])

        s_next = logits(0)
        for pair in range(MLA_PAIRS_PER_STEP):
            v = v_ref[key_rows, pair * LANES:(pair + 1) * LANES]
            out = None
            for hh in range(2):
                head = 2 * pair + hh
                s = s_next
                if head + 1 < 2 * MLA_PAIRS_PER_STEP:
                    s_next = logits(head + 1)
                p = jnp.exp2(s - jnp.max(s, axis=-1, keepdims=True))
                r = 1.0 / jnp.sum(p, axis=-1, keepdims=True)
                vh = jnp.where(_lane_mask(hh * MLA_V, (hh + 1) * MLA_V), v, jnp.zeros_like(v))
                o = _dot(p.astype(BF16), vh) * r
                out = o if out is None else out + o
            o_ref[query_rows, pair * LANES:(pair + 1) * LANES] = out.astype(BF16)

    _wide_tile_branches(2, seq, attend)


def _mla_attn(st, q, k, v):
    slots = 2 * MLA_PAIRS_PER_STEP * LANES
    vals = MLA_PAIRS_PER_STEP * LANES
    kern = functools.partial(_mla_attn_kernel, seq=st.seq)
    return pl.pallas_call(
        kern,
        grid=(st.batch, MLA_HEADS // (2 * MLA_PAIRS_PER_STEP), st.attn_steps),
        in_specs=[pl.BlockSpec((None, ATTN_TILE, slots), lambda b, h, j: (b, j, h)),
                  pl.BlockSpec((None, st.tokens, slots), lambda b, h, j: (b, 0, h)),
                  pl.BlockSpec((None, st.tokens, vals), lambda b, h, j: (b, 0, h))],
        out_specs=pl.BlockSpec((None, ATTN_TILE, vals), lambda b, h, j: (b, j, h)),
        out_shape=jax.ShapeDtypeStruct((st.batch, st.tokens, D_MODEL), BF16),
        compiler_params=_params(3),
        name="mla_attn",
    )(q, k, v)


NA_QROWS = TOKEN_TILE // GRID_W
NA_WROWS = NA_QROWS + NA_KH
NA_PAIRS = NA_WROWS // 2
NA_BIAS_LO = -(NA_QROWS - 1) - (NA_WROWS - NA_QROWS - NA_KH // 2) + NA_KH - 1
NA_BIAS_N = (NA_WROWS - 2) + NA_KH - 1 - NA_BIAS_LO + 1


def _na_attn_kernel(q_ref, k_ref, v_ref, bias_ref, o_ref, cap_ref, *, rows, seq):
    r0 = pl.program_id(1) * NA_QROWS
    w0 = jnp.clip(r0 - NA_KH // 2, 0, rows - NA_WROWS)
    start = pl.multiple_of(w0 * GRID_W, GRID_W)
    n_loc = NA_WROWS * GRID_W
    qidx = lax.broadcasted_iota(jnp.int32, (TOKEN_TILE, n_loc), 0)
    kidx = lax.broadcasted_iota(jnp.int32, (TOKEN_TILE, n_loc), 1)
    q_row, q_col = r0 + qidx // GRID_W, qidx % GRID_W
    k_row, k_col = w0 + kidx // GRID_W, kidx % GRID_W
    row_start = jnp.clip(q_row - NA_KH // 2, 0, rows - NA_KH)
    col_start = jnp.clip(q_col - NA_KW // 2, 0, GRID_W - NA_KW)
    valid = ((k_row >= row_start) & (k_row < row_start + NA_KH)
             & (k_col >= col_start) & (k_col < col_start + NA_KW))
    cap_ref[...] = jnp.where(valid, jnp.inf, NEG_INF)
    base = w0 - r0 + NA_KH - 1 - NA_BIAS_LO

    def logits(head):
        pair, hh = divmod(head, 2)
        cols = slice(pair * LANES, (pair + 1) * LANES)
        q = q_ref[:, cols]
        qh = jnp.where(_lane_mask(hh * NA_DH, (hh + 1) * NA_DH), q, jnp.zeros_like(q))
        return _dot_t(qh, k_ref[pl.ds(start, n_loc), cols]), _dot_t(qh, k_ref[seq:, cols])

    s_next = logits(0)
    for pair in range(NA_HEADS // 2):
        cols = slice(pair * LANES, (pair + 1) * LANES)
        v_loc, v_ctx = v_ref[pl.ds(start, n_loc), cols], v_ref[seq:, cols]
        out = None
        for hh in range(2):
            head = 2 * pair + hh
            head_lanes = _lane_mask(hh * NA_DH, (hh + 1) * NA_DH)
            s_loc, s_ctx = s_next
            if head + 1 < NA_HEADS:
                s_next = logits(head + 1)
            bias = jnp.concatenate(
                [jnp.concatenate([bias_ref[head, base + 2 * p - a] for p in range(NA_PAIRS)], axis=1)
                 for a in range(NA_QROWS)], axis=0)
            s_loc = jnp.minimum(s_loc + bias, cap_ref[...])
            m = jnp.maximum(jnp.max(s_loc, axis=-1, keepdims=True), jnp.max(s_ctx, axis=-1, keepdims=True))
            p_loc, p_ctx = jnp.exp2(s_loc - m), jnp.exp2(s_ctx - m)
            rcp = 1.0 / (jnp.sum(p_loc, axis=-1, keepdims=True) + jnp.sum(p_ctx, axis=-1, keepdims=True))
            o = (_dot(p_loc.astype(BF16), jnp.where(head_lanes, v_loc, jnp.zeros_like(v_loc)))
                 + _dot(p_ctx.astype(BF16), jnp.where(head_lanes, v_ctx, jnp.zeros_like(v_ctx)))) * rcp
            out = o if out is None else out + o
        o_ref[:, cols] = out.astype(BF16)


def _na_bias_table(rpb):
    n_rel, n_col = 2 * NA_KH - 1, 2 * NA_KW - 1
    left = GRID_W - NA_KW
    v = jnp.pad(rpb * LOG2E, ((0, 0), (0, 0), (left, 2 * GRID_W - left - n_col)))
    v = jnp.tile(v, (1, 1, GRID_W))[:, :, :GRID_W * (2 * GRID_W - 1)]
    toep = v.reshape(NA_HEADS, n_rel, GRID_W, 2 * GRID_W - 1)[..., GRID_W - 1:]
    toep = jnp.pad(toep, ((0, 0), (-NA_BIAS_LO, NA_BIAS_LO + NA_BIAS_N + 1 - n_rel), (0, 0), (0, 0)))
    return jnp.concatenate([toep[:, :-1], toep[:, 1:]], axis=-1)


def _na_attn(st, q, k, v, bias_tab):
    rows = st.seq // GRID_W
    assert rows >= NA_WROWS and rows % NA_QROWS == 0
    kern = functools.partial(_na_attn_kernel, rows=rows, seq=st.seq)
    return pl.pallas_call(
        kern,
        grid=(st.batch, st.lat_tiles),
        in_specs=[st.tile_spec(D_MODEL),
                  pl.BlockSpec((None, st.tokens, D_MODEL), lambda b, j: (b, 0, 0)),
                  pl.BlockSpec((None, st.tokens, D_MODEL), lambda b, j: (b, 0, 0)),
                  _resident((NA_HEADS, NA_BIAS_N, GRID_W, 2 * GRID_W))],
        out_specs=st.tile_spec(D_MODEL),
        out_shape=jax.ShapeDtypeStruct((st.batch, st.seq, D_MODEL), BF16),
        scratch_shapes=[pltpu.VMEM((TOKEN_TILE, NA_WROWS * GRID_W), F32)],
        compiler_params=_params(2),
        name="na_attn",
    )(q, k, v, bias_tab)


def _slot_columns(w, width):
    kdim = w.shape[0]
    w = w.reshape(kdim, -1, width)
    return jnp.pad(w, ((0, 0), (0, 0), (0, LANES - width))).reshape(kdim, -1)


def _slot_gain(g):
    return jnp.pad(g, ((0, 0), (0, LANES - g.shape[1])))


def kernel(x, c, ctx, c_ctx, ada_w, ada_b, norm_g, ffn_w_in, ffn_w_out, pool_w, pool_b, pool_scale,
           diff_w_qkv, diff_qk_g, diff_lambda, diff_sub_g, diff_w_o,
           mla_w_dq, mla_q_g, mla_w_uq, mla_w_dkv, mla_kv_g, mla_w_ukv, mla_qk_g, mla_w_o,
           na_w_qkv, na_qk_g, na_rpb, na_w_o):
    assert DEPTH == N_MIXERS, "one layer per mixer: the context stream ends after the last softmax mixer's keys"
    batch, seq, _ = x.shape
    st = _Stream(batch, seq, ctx.shape[1])
    xs = jnp.concatenate([x, ctx], axis=1)

    rows = -(-(batch + 1) // 16) * 16
    cc = jnp.zeros((rows, D_MODEL), F32).at[:batch].set(c).at[batch].set(c_ctx)
    mods = _adaln(cc, ada_w, ada_b)

    w_in = ffn_w_in.astype(BF16)
    w_out = ffn_w_out.astype(BF16)
    gains = norm_g.reshape(DEPTH, 3, 1, D_MODEL)

    for i in range(DEPTH):
        mod = mods[i]
        xs = _ffn(st, xs, mod, 0, gains[i, 0], w_in[i, 0], w_out[i, 0])
        if i == 0:
            xs = _pool(st, xs, mod, gains[i, 1], pool_w[0].astype(BF16), pool_b[0].reshape(1, D_MODEL),
                       pool_scale[0].reshape(1, D_MODEL))
            xs = _ffn(st, xs, mod, 2, gains[i, 2], w_in[i, 1], w_out[i, 1])
            continue
        if i == 1:
            q, k, v = _qkv_pre(st, xs, mod, gains[i, 1], diff_w_qkv[0].astype(BF16), diff_qk_g[0], True)
            lam_init = 0.8 - 0.6 * math.exp(-0.3 * i)
            attn = _diff_attn(st, q, k, v, diff_lambda[0], diff_sub_g[0], lam_init)
            w_o = diff_w_o[0]
        elif i == 2:
            kpe_cols = jnp.pad(mla_w_dkv[0][:, MLA_KV_RANK:], ((0, 0), (MLA_NOPE, LANES - MLA_NOPE - MLA_ROPE)))
            w_down = jnp.concatenate([mla_w_dq[0], mla_w_dkv[0][:, :MLA_KV_RANK], kpe_cols], axis=1)
            ukv = mla_w_ukv[0].reshape(MLA_KV_RANK, MLA_HEADS, MLA_NOPE + MLA_V)
            w_ukv = jnp.concatenate([_slot_columns(ukv[:, :, :MLA_NOPE].reshape(MLA_KV_RANK, -1), MLA_NOPE),
                                     ukv[:, :, MLA_NOPE:].reshape(MLA_KV_RANK, -1)], axis=1)
            q, k, v = _mla_pre(st, xs, mod, gains[i, 1], w_down.astype(BF16),
                               mla_q_g[0].reshape(1, -1), mla_kv_g[0].reshape(1, -1),
                               _slot_columns(mla_w_uq[0], MLA_NOPE + MLA_ROPE).astype(BF16),
                               w_ukv.astype(BF16), _slot_gain(mla_qk_g[0]))
            attn = _mla_attn(st, q, k, v)
            w_o = mla_w_o[0]
        else:
            q, k, v = _qkv_pre(st, xs, mod, gains[i, 1], na_w_qkv[0].astype(BF16), na_qk_g[0], False)
            attn = _na_attn(st, q, k, v, _na_bias_table(na_rpb[0]))
            w_o = na_w_o[0]
        xs = _mixout_ffn(st, xs, attn, mod, gains[i, 2], w_o.astype(BF16), w_in[i, 1], w_out[i, 1],
                         latent_only=(i == DEPTH - 1))
    return xs
```

```python
import functools
import math

import jax
import jax.numpy as jnp
from jax import lax
from jax.experimental import pallas as pl
from jax.experimental.pallas import tpu as pltpu

D_MODEL = 1024
DEPTH = 4
GRID_W = 64
N_MIXERS = 4
N_MOD = 9
EPS = 1e-6
LOG2E = math.log2(math.e)
ROPE_BASE = 10000.0
NEG_INF = -1e30
D_FF = 2816
POOL_WINDOWS = (2, 4, 8, 16)
POOL_G = D_MODEL // len(POOL_WINDOWS)
DIFF_HEADS = 8
DIFF_DH = 64
MLA_HEADS = 16
MLA_Q_RANK = 384
MLA_KV_RANK = 256
MLA_NOPE = 64
MLA_ROPE = 32
MLA_V = 64
NA_HEADS = 16
NA_DH = 64
NA_KH = 8
NA_KW = 16

LANES = 128
HALO = 8
TOKEN_TILE = 256
ATTN_TILE = 512
VMEM_LIMIT = 56 * 2**20

F32 = jnp.float32
BF16 = jnp.bfloat16


def _params(n_axes):
    return pltpu.CompilerParams(dimension_semantics=("arbitrary",) * n_axes,
                                vmem_limit_bytes=VMEM_LIMIT)


def _resident(shape, stack_index=()):
    index = tuple(stack_index) + (0,) * len(shape)
    return pl.BlockSpec((None,) * len(stack_index) + tuple(shape), lambda *_: index, pipeline_mode=pl.Buffered(1))


def _dot(a, b):
    return jnp.dot(a, b, preferred_element_type=F32)


def _dot_t(a, b):
    return lax.dot_general(a, b, (((1,), (1,)), ((), ())), preferred_element_type=F32)


def _modulate(x, g, shift, scale):
    ms = jnp.mean(x * x, axis=-1, keepdims=True)
    return (x * lax.rsqrt(ms + EPS)) * (g * (1.0 + scale)) + shift


def _silu(x):
    return x * (1.0 / (1.0 + jnp.exp(-x)))


def _adaln_kernel(c_ref, w_ref, b_ref, o_ref):
    s = _silu(c_ref[...]).astype(BF16)
    o_ref[...] = _dot(s, w_ref[...].astype(BF16)) + b_ref[...]


def _adaln(cc, ada_w, ada_b):
    rows = cc.shape[0]
    n_col = N_MOD * D_MODEL // D_MODEL
    out = pl.pallas_call(
        _adaln_kernel,
        grid=(DEPTH, n_col),
        in_specs=[pl.BlockSpec((rows, D_MODEL), lambda i, n: (0, 0)),
                  pl.BlockSpec((None, D_MODEL, D_MODEL), lambda i, n: (i, 0, n)),
                  pl.BlockSpec((None, 1, D_MODEL), lambda i, n: (i, 0, n))],
        out_specs=pl.BlockSpec((None, rows, D_MODEL), lambda i, n: (i, 0, n)),
        out_shape=jax.ShapeDtypeStruct((DEPTH, rows, N_MOD * D_MODEL), F32),
        compiler_params=_params(2),
        name="adaln",
    )(cc, ada_w, ada_b.reshape(DEPTH, 1, N_MOD * D_MODEL))
    return out.reshape(DEPTH, rows, N_MOD, D_MODEL)


def _ffn_core(x, mod_ref, k, g_ref, win_ref, wout_ref):
    shift = mod_ref[3 * k:3 * k + 1, :]
    scale = mod_ref[3 * k + 1:3 * k + 2, :]
    gate = mod_ref[3 * k + 2:3 * k + 3, :]
    parts = [x[r:r + TOKEN_TILE] for r in range(0, x.shape[0], TOKEN_TILE)]
    ups = [_dot(_modulate(p, g_ref[...], shift, scale).astype(BF16), win_ref[...]) for p in parts]
    downs = [_dot((_silu(u[:, :D_FF]) * u[:, D_FF:]).astype(BF16), wout_ref[...]) for u in ups]
    outs = [p + (0.5 * gate) * y for p, y in zip(parts, downs)]
    return outs[0] if len(outs) == 1 else jnp.concatenate(outs, axis=0)


def _wide_tile_branches(axis, seq, body, with_ctx=True):
    j = pl.program_id(axis)
    lat_steps = seq // ATTN_TILE

    @pl.when(j < lat_steps)
    def _():
        body(slice(None), slice(None))

    if with_ctx:
        @pl.when(j >= lat_steps)
        def _():
            body(slice(0, TOKEN_TILE), slice(seq, None))


def _ffn_kernel(x_ref, mod_ref, g_ref, win_ref, wout_ref, o_ref, *, k, seq):
    def step(rows, _):
        o_ref[rows, :] = _ffn_core(x_ref[rows, :], mod_ref, k, g_ref, win_ref, wout_ref)

    _wide_tile_branches(1, seq, step)


def _ffn_join_kernel(x_ref, c_ref, mod_ref, g_ref, win_ref, wout_ref, o_ref, *, seq):
    j = pl.program_id(1)

    @pl.when(j < seq // ATTN_TILE)
    def _():
        o_ref[...] = _ffn_core(x_ref[...], mod_ref, 0, g_ref, win_ref, wout_ref)

    @pl.when(j >= seq // ATTN_TILE)
    def _():
        o_ref[0:TOKEN_TILE, :] = _ffn_core(c_ref[...], mod_ref, 0, g_ref, win_ref, wout_ref)


def _mixout_ffn_kernel(x_ref, a_ref, mod_ref, g_ref, wo_ref, win_ref, wout_ref, o_ref, *, seq, with_ctx):
    def step(rows, _):
        x = x_ref[rows, :] + mod_ref[5:6, :] * _dot(a_ref[rows, :], wo_ref[...])
        o_ref[rows, :] = _ffn_core(x, mod_ref, 2, g_ref, win_ref, wout_ref)

    _wide_tile_branches(1, seq, step, with_ctx)


class _Stream:
    def __init__(self, batch, seq, ctx):
        assert seq % TOKEN_TILE == 0 and ctx == TOKEN_TILE and seq % GRID_W == 0
        self.batch, self.seq, self.ctx = batch, seq, ctx
        self.tokens = seq + ctx
        self.lat_tiles = seq // TOKEN_TILE
        self.tiles = self.tokens // TOKEN_TILE
        assert seq % ATTN_TILE == 0 and ATTN_TILE >= ctx
        self.attn_steps = seq // ATTN_TILE + 1

    def tile_spec(self, width):
        return pl.BlockSpec((None, TOKEN_TILE, width), lambda b, j: (b, j, 0))

    def wide_spec(self, width):
        return pl.BlockSpec((None, ATTN_TILE, width), lambda b, j: (b, j, 0))

    def mod_spec(self, rows=TOKEN_TILE):
        lat, ctx_row = self.seq // rows, self.batch
        return pl.BlockSpec((None, N_MOD, D_MODEL),
                            lambda b, j: (jnp.where(j < lat, b, ctx_row), 0, 0))


def _ffn_weight_specs(layer, half):
    return [_resident((D_MODEL, 2 * D_FF), (layer, half)), _resident((D_FF, D_MODEL), (layer, half))]


def _ffn(st, xs, mod, k, g, w_in, w_out, layer):
    return pl.pallas_call(
        functools.partial(_ffn_kernel, k=k, seq=st.seq),
        grid=(st.batch, st.attn_steps),
        in_specs=[st.wide_spec(D_MODEL), st.mod_spec(ATTN_TILE), _resident((1, D_MODEL))]
        + _ffn_weight_specs(layer, k // 2),
        out_specs=st.wide_spec(D_MODEL),
        out_shape=jax.ShapeDtypeStruct((st.batch, st.tokens, D_MODEL), F32),
        compiler_params=_params(2),
        name="ffn",
    )(xs, mod, g, w_in, w_out)


def _ffn_join(st, x, ctx, mod, g, w_in, w_out):
    last_lat = st.attn_steps - 2
    return pl.pallas_call(
        functools.partial(_ffn_join_kernel, seq=st.seq),
        grid=(st.batch, st.attn_steps),
        in_specs=[pl.BlockSpec((None, ATTN_TILE, D_MODEL), lambda b, j: (b, jnp.minimum(j, last_lat), 0)),
                  pl.BlockSpec((None, st.ctx, D_MODEL), lambda b, j: (b, 0, 0)),
                  st.mod_spec(ATTN_TILE), _resident((1, D_MODEL))] + _ffn_weight_specs(0, 0),
        out_specs=st.wide_spec(D_MODEL),
        out_shape=jax.ShapeDtypeStruct((st.batch, st.tokens, D_MODEL), F32),
        compiler_params=_params(2),
        name="ffn_join",
    )(x, ctx, mod, g, w_in, w_out)


def _mixout_ffn(st, xs, attn, mod, g, w_o, w_in, w_out, layer, latent_only):
    steps = st.attn_steps - 1 if latent_only else st.attn_steps
    tokens = st.seq if latent_only else st.tokens
    return pl.pallas_call(
        functools.partial(_mixout_ffn_kernel, seq=st.seq, with_ctx=not latent_only),
        grid=(st.batch, steps),
        in_specs=[st.wide_spec(D_MODEL), st.wide_spec(D_MODEL), st.mod_spec(ATTN_TILE),
                  _resident((1, D_MODEL)), _resident((D_MODEL, D_MODEL))] + _ffn_weight_specs(layer, 1),
        out_specs=st.wide_spec(D_MODEL),
        out_shape=jax.ShapeDtypeStruct((st.batch, tokens, D_MODEL), F32),
        compiler_params=_params(2),
        name="mixout_ffn",
    )(xs, attn, mod, g, w_o, w_in, w_out)


def _pool_kernel(x_ref, xp_ref, xn_ref, mod_ref, g_ref, w_ref, b_ref, s_ref, o_ref, hext_ref,
                 *, lat_tiles, tiles, seq, ctx):
    j = pl.program_id(1)
    g, shift, scale = g_ref[...], mod_ref[3:4, :], mod_ref[4:5, :]
    x = x_ref[...]
    h = _modulate(x, g, shift, scale)
    has_prev = jnp.logical_and(j != 0, j != lat_tiles)
    has_next = jnp.logical_and(j != lat_tiles - 1, j != tiles - 1)
    hext_ref[0:HALO, :] = jnp.where(has_prev, _modulate(xp_ref[...], g, shift, scale), 0.0)
    hext_ref[HALO:HALO + TOKEN_TILE, :] = h
    hext_ref[HALO + TOKEN_TILE:, :] = jnp.where(has_next, _modulate(xn_ref[...], g, shift, scale), 0.0)
    in_lat = j < lat_tiles
    t = (j - jnp.where(in_lat, 0, lat_tiles)) * TOKEN_TILE + lax.broadcasted_iota(jnp.int32, (TOKEN_TILE, 1), 0)
    n = jnp.where(in_lat, seq, ctx)
    for gi, win in enumerate(POOL_WINDOWS):
        cols = slice(gi * POOL_G, (gi + 1) * POOL_G)
        acc = hext_ref[HALO - win // 2:HALO - win // 2 + TOKEN_TILE, cols]
        for k in range(1 - win // 2, win - win // 2):
            acc = acc + hext_ref[HALO + k:HALO + k + TOKEN_TILE, cols]
        lo = jnp.maximum(t - win // 2, 0)
        hi = jnp.minimum(t - win // 2 + win, n)
        mean = acc / (hi - lo).astype(F32)
        d = (mean - h[:, cols]).astype(BF16)
        y = (_dot(d, w_ref[gi]) + b_ref[:, cols]) * s_ref[:, cols]
        o_ref[:, cols] = x[:, cols] + mod_ref[5:6, cols] * y


def _pool(st, xs, mod, g, w, b, scale):
    per_tile = TOKEN_TILE // HALO
    last = st.tokens // HALO - 1
    kern = functools.partial(_pool_kernel, lat_tiles=st.lat_tiles, tiles=st.tiles, seq=st.seq, ctx=st.ctx)
    return pl.pallas_call(
        kern,
        grid=(st.batch, st.tiles),
        in_specs=[st.tile_spec(D_MODEL),
                  pl.BlockSpec((None, HALO, D_MODEL), lambda b_, j: (b_, jnp.maximum(j * per_tile - 1, 0), 0)),
                  pl.BlockSpec((None, HALO, D_MODEL), lambda b_, j: (b_, jnp.minimum((j + 1) * per_tile, last), 0)),
                  st.mod_spec(), _resident((1, D_MODEL)),
                  _resident((len(POOL_WINDOWS), POOL_G, POOL_G)),
                  _resident((1, D_MODEL)), _resident((1, D_MODEL))],
        out_specs=st.tile_spec(D_MODEL),
        out_shape=jax.ShapeDtypeStruct((st.batch, st.tokens, D_MODEL), F32),
        scratch_shapes=[pltpu.VMEM((TOKEN_TILE + 2 * HALO, D_MODEL), F32)],
        compiler_params=_params(2),
        name="pool",
    )(xs, xs, xs, mod, g, w, b, scale)


def _lane_mask(lo, hi):
    lane = lax.broadcasted_iota(jnp.int32, (1, LANES), 1)
    return jnp.logical_and(lane >= lo, lane < hi)


def _segment_mean_matrix(lengths):
    assert sum(lengths) == LANES and all(n & (n - 1) == 0 for n in lengths)
    ids = jnp.repeat(jnp.arange(len(lengths)), jnp.array(lengths), total_repeat_length=LANES)
    inv = 1.0 / jnp.array(lengths, F32)[ids]
    m = jnp.where(ids[:, None] == ids[None, :], inv[None, :], 0.0)
    return jnp.tile(m, (2, 1)).astype(BF16)


def _segment_inv_rms(x, seg_ref):
    sq = x * x
    hi = sq.astype(BF16)
    lo = (sq - hi.astype(F32)).astype(BF16)
    return lax.rsqrt(_dot(jnp.concatenate([hi, lo], axis=1), seg_ref[...]) + EPS)


def _halves_inv_rms(x):
    sq = x * x
    lower = _lane_mask(0, LANES // 2)
    low = jnp.sum(jnp.where(lower, sq, 0.0), axis=-1, keepdims=True)
    high = jnp.sum(jnp.where(lower, 0.0, sq), axis=-1, keepdims=True)
    inv_n = 2.0 / LANES
    return jnp.where(lower, lax.rsqrt(low * inv_n + EPS), lax.rsqrt(high * inv_n + EPS))


def _rotate(z, tabs, shift):
    c_ref, u_ref, d_ref = tabs
    return (z * c_ref[...] + pltpu.roll(z, LANES - shift, axis=1) * u_ref[...]
            + pltpu.roll(z, shift, axis=1) * d_ref[...])


def _rope_tables(st, dim, lane_lo, period, gain, post):
    nf = dim // 4
    inv = ROPE_BASE ** (-jnp.arange(nf, dtype=F32) / nf)
    t = jnp.arange(st.seq, dtype=jnp.int32)
    pos = jnp.stack([t // GRID_W, t % GRID_W], axis=-1).astype(F32)
    ang = jnp.broadcast_to(pos[:, :, None, None] * inv, (st.seq, 2, 2, nf)).reshape(st.seq, dim)
    cos, sin = jnp.cos(ang), jnp.sin(ang)
    first_half = (jnp.arange(dim) % (2 * nf)) < nf
    sin_up = jnp.where(first_half, -sin, 0.0)
    sin_dn = jnp.where(first_half, 0.0, sin)

    def place(tab, fill, lane_gain):
        blk = jnp.full((st.seq, period), fill, F32).at[:, lane_lo:lane_lo + dim].set(tab)
        blk = jnp.tile(blk, (1, LANES // period))
        return jnp.concatenate([blk, jnp.full((st.ctx, LANES), fill, F32)], axis=0) * (lane_gain * post)

    return (place(cos, 1.0, gain), place(sin_up, 0.0, jnp.roll(gain, -nf)), place(sin_dn, 0.0, jnp.roll(gain, nf)))


def _qkv_pre_kernel(*refs, rope):
    if rope:
        x_ref, mod_ref, g_ref, w_ref, seg_ref, *tabs, q_ref, k_ref, v_ref = refs
    else:
        x_ref, mod_ref, g_ref, w_ref, gain_ref, q_ref, k_ref, v_ref = refs
    h = _modulate(x_ref[...], g_ref[...], mod_ref[3:4, :], mod_ref[4:5, :]).astype(BF16)
    qkv = _dot(h, w_ref[...])
    for t, dst in enumerate((q_ref, k_ref)):
        for c in range(D_MODEL // LANES):
            xb = qkv[:, t * D_MODEL + c * LANES:t * D_MODEL + (c + 1) * LANES]
            if rope:
                y = _rotate(xb * _segment_inv_rms(xb, seg_ref), tabs[3 * t:3 * t + 3], DIFF_DH // 4)
            else:
                y = xb * _halves_inv_rms(xb) * gain_ref[t:t + 1, :]
            dst[:, c * LANES:(c + 1) * LANES] = y.astype(BF16)
    v_ref[...] = qkv[:, 2 * D_MODEL:].astype(BF16)


def _qkv_pre(st, xs, mod, g, w_qkv, qk_gain, rope):
    tab_spec = pl.BlockSpec((TOKEN_TILE, LANES), lambda b, j: (j, 0))
    in_specs = [st.tile_spec(D_MODEL), st.mod_spec(), _resident((1, D_MODEL)), _resident((D_MODEL, 3 * D_MODEL))]
    args = [xs, mod, g, w_qkv]
    lane_gain = jnp.tile(qk_gain, (1, 2))
    post = (DIFF_DH ** -0.5 * LOG2E, 1.0)
    if rope:
        in_specs += [_resident((2 * LANES, LANES))] + [tab_spec] * 6
        args += [_segment_mean_matrix((DIFF_DH, DIFF_DH))]
        for t in range(2):
            args += list(_rope_tables(st, DIFF_DH, 0, DIFF_DH, lane_gain[t], post[t]))
    else:
        in_specs += [_resident((2, LANES))]
        args += [lane_gain * jnp.array(post, F32)[:, None]]
    shape = jax.ShapeDtypeStruct((st.batch, st.tokens, D_MODEL), BF16)
    return pl.pallas_call(
        functools.partial(_qkv_pre_kernel, rope=rope),
        grid=(st.batch, st.tiles),
        in_specs=in_specs,
        out_specs=[st.tile_spec(D_MODEL)] * 3,
        out_shape=[shape] * 3,
        compiler_params=_params(2),
        name="qkv_pre",
    )(*args)


DIFF_HEADS_PER_STEP = 8


def _diff_attn_kernel(q_ref, k_ref, v_ref, lam_ref, subg_ref, o_ref, *, seq, lam_init):
    lp = lam_ref[...]
    lam = (jnp.exp(jnp.sum(lp[0:1] * lp[1:2], axis=-1, keepdims=True))
           - jnp.exp(jnp.sum(lp[2:3] * lp[3:4], axis=-1, keepdims=True)) + lam_init)

    def attend(query_rows, key_rows):
        def logits(hd):
            cols = slice(hd * LANES, (hd + 1) * LANES)
            q, k = q_ref[query_rows, cols], k_ref[key_rows, cols]
            return [_dot_t(jnp.where(_lane_mask(c * DIFF_DH, (c + 1) * DIFF_DH), q, jnp.zeros_like(q)), k)
                    for c in range(2)]

        s_next = logits(0)
        for hd in range(DIFF_HEADS_PER_STEP):
            cols = slice(hd * LANES, (hd + 1) * LANES)
            s_cur = s_next
            if hd + 1 < DIFF_HEADS_PER_STEP:
                s_next = logits(hd + 1)
            probs = []
            for s in s_cur:
                p = jnp.exp2(s - jnp.max(s, axis=-1, keepdims=True))
                probs.append((p, 1.0 / jnp.sum(p, axis=-1, keepdims=True)))
            (p0, r0), (p1, r1) = probs
            a = (p0 - p1 * (lam * r1 / r0)).astype(BF16)
            o = _dot(a, v_ref[key_rows, cols]) * r0
            ms = jnp.mean(o * o, axis=-1, keepdims=True)
            o_ref[query_rows, cols] = ((o * lax.rsqrt(ms + EPS)) * subg_ref[...] * (1.0 - lam_init)).astype(BF16)

    _wide_tile_branches(2, seq, attend)


def _diff_attn(st, q, k, v, lam_p, sub_g, lam_init):
    width = DIFF_HEADS_PER_STEP * LANES
    head = lambda b, h, j: (b, j, h)
    keys = lambda b, h, j: (b, 0, h)
    kern = functools.partial(_diff_attn_kernel, seq=st.seq, lam_init=lam_init)
    return pl.pallas_call(
        kern,
        grid=(st.batch, DIFF_HEADS // DIFF_HEADS_PER_STEP, st.attn_steps),
        in_specs=[pl.BlockSpec((None, ATTN_TILE, width), head),
                  pl.BlockSpec((None, st.tokens, width), keys),
                  pl.BlockSpec((None, st.tokens, width), keys),
                  _resident((4, DIFF_DH)), _resident((1, LANES))],
        out_specs=pl.BlockSpec((None, ATTN_TILE, width), head),
        out_shape=jax.ShapeDtypeStruct((st.batch, st.tokens, D_MODEL), BF16),
        compiler_params=_params(3),
        name="diff_attn",
    )(q, k, v, lam_p, sub_g.reshape(1, LANES))


MLA_DOWN = MLA_Q_RANK + MLA_KV_RANK + LANES


def _mla_pre_kernel(x_ref, mod_ref, g_ref, wd_ref, qg_ref, kvg_ref, wuq_ref, wukv_ref, seg_ref, kgain_ref,
                    qc_ref, qu_ref, qd_ref, kc_ref, ku_ref, kd_ref, q_ref, k_ref, v_ref):
    h = _modulate(x_ref[...], g_ref[...], mod_ref[3:4, :], mod_ref[4:5, :]).astype(BF16)
    down = _dot(h, wd_ref[...])

    def full_norm(z, gain):
        ms = jnp.mean(z * z, axis=-1, keepdims=True)
        return ((z * lax.rsqrt(ms + EPS)) * gain).astype(BF16)

    q = _dot(full_norm(down[:, :MLA_Q_RANK], qg_ref[...]), wuq_ref[...])
    kv = _dot(full_norm(down[:, MLA_Q_RANK:MLA_Q_RANK + MLA_KV_RANK], kvg_ref[...]), wukv_ref[...])
    kpe = down[:, MLA_Q_RANK + MLA_KV_RANK:]
    kpe = _rotate(kpe * _segment_inv_rms(kpe, seg_ref), (kc_ref, ku_ref, kd_ref), MLA_ROPE // 4)
    for hd in range(MLA_HEADS):
        cols = slice(hd * LANES, (hd + 1) * LANES)
        qb = q[:, cols]
        q_ref[:, cols] = _rotate(qb * _segment_inv_rms(qb, seg_ref), (qc_ref, qu_ref, qd_ref),
                                 MLA_ROPE // 4).astype(BF16)
        kb = kv[:, cols]
        k_ref[:, cols] = (kb * _segment_inv_rms(kb, seg_ref) * kgain_ref[...] + kpe).astype(BF16)
    v_ref[...] = kv[:, MLA_HEADS * LANES:].astype(BF16)


def _mla_pre(st, xs, mod, g, w_down, q_g, kv_g, w_uq, w_ukv, qk_gain):
    tab_spec = pl.BlockSpec((TOKEN_TILE, LANES), lambda b, j: (j, 0))
    slots = MLA_HEADS * LANES
    scale = (MLA_NOPE + MLA_ROPE) ** -0.5 * LOG2E
    q_tabs = _rope_tables(st, MLA_ROPE, MLA_NOPE, LANES, qk_gain[0], scale)
    k_tabs = _rope_tables(st, MLA_ROPE, MLA_NOPE, LANES, qk_gain[1], 1.0)
    seg = _segment_mean_matrix((MLA_NOPE, MLA_ROPE, LANES - MLA_NOPE - MLA_ROPE))
    return pl.pallas_call(
        _mla_pre_kernel,
        grid=(st.batch, st.tiles),
        in_specs=[st.tile_spec(D_MODEL), st.mod_spec(), _resident((1, D_MODEL)),
                  _resident((D_MODEL, MLA_DOWN)), _resident((1, MLA_Q_RANK)), _resident((1, MLA_KV_RANK)),
                  _resident((MLA_Q_RANK, slots)), _resident((MLA_KV_RANK, slots + D_MODEL)),
                  _resident((2 * LANES, LANES)), _resident((1, LANES))] + [tab_spec] * 6,
        out_specs=[st.tile_spec(slots), st.tile_spec(slots), st.tile_spec(D_MODEL)],
        out_shape=[jax.ShapeDtypeStruct((st.batch, st.tokens, slots), BF16),
                   jax.ShapeDtypeStruct((st.batch, st.tokens, slots), BF16),
                   jax.ShapeDtypeStruct((st.batch, st.tokens, D_MODEL), BF16)],
        compiler_params=_params(2),
        name="mla_pre",
    )(xs, mod, g, w_down, q_g, kv_g, w_uq, w_ukv, seg, qk_gain[1:2], *q_tabs, *k_tabs)


MLA_PAIRS_PER_STEP = 4


def _mla_attn_kernel(q_ref, k_ref, v_ref, o_ref, *, seq):
    def attend(query_rows, key_rows):
        def logits(head):
            slot = slice(head * LANES, (head + 1) * LANES)
            return _dot_t(q_ref[query_rows, slot], k_ref[key_rows, slot])

        s_next = logits(0)
        for pair in range(MLA_PAIRS_PER_STEP):
            v = v_ref[key_rows, pair * LANES:(pair + 1) * LANES]
            out = None
            for hh in range(2):
                head = 2 * pair + hh
                s = s_next
                if head + 1 < 2 * MLA_PAIRS_PER_STEP:
                    s_next = logits(head + 1)
                p = jnp.exp2(s - jnp.max(s, axis=-1, keepdims=True))
                r = 1.0 / jnp.sum(p, axis=-1, keepdims=True)
                vh = jnp.where(_lane_mask(hh * MLA_V, (hh + 1) * MLA_V), v, jnp.zeros_like(v))
                o = _dot(p.astype(BF16), vh) * r
                out = o if out is None else out + o
            o_ref[query_rows, pair * LANES:(pair + 1) * LANES] = out.astype(BF16)

    _wide_tile_branches(2, seq, attend)


def _mla_attn(st, q, k, v):
    slots = 2 * MLA_PAIRS_PER_STEP * LANES
    vals = MLA_PAIRS_PER_STEP * LANES
    kern = functools.partial(_mla_attn_kernel, seq=st.seq)
    return pl.pallas_call(
        kern,
        grid=(st.batch, MLA_HEADS // (2 * MLA_PAIRS_PER_STEP), st.attn_steps),
        in_specs=[pl.BlockSpec((None, ATTN_TILE, slots), lambda b, h, j: (b, j, h)),
                  pl.BlockSpec((None, st.tokens, slots), lambda b, h, j: (b, 0, h)),
                  pl.BlockSpec((None, st.tokens, vals), lambda b, h, j: (b, 0, h))],
        out_specs=pl.BlockSpec((None, ATTN_TILE, vals), lambda b, h, j: (b, j, h)),
        out_shape=jax.ShapeDtypeStruct((st.batch, st.tokens, D_MODEL), BF16),
        compiler_params=_params(3),
        name="mla_attn",
    )(q, k, v)


NA_QROWS = TOKEN_TILE // GRID_W
NA_WROWS = NA_QROWS + NA_KH
NA_PAIRS = NA_WROWS // 2
NA_BIAS_LO = -(NA_QROWS - 1) - (NA_WROWS - NA_QROWS - NA_KH // 2) + NA_KH - 1
NA_BIAS_N = (NA_WROWS - 2) + NA_KH - 1 - NA_BIAS_LO + 1


def _na_attn_kernel(q_ref, k_ref, v_ref, bias_ref, o_ref, cap_ref, *, rows, seq):
    r0 = pl.program_id(1) * NA_QROWS
    w0 = jnp.clip(r0 - NA_KH // 2, 0, rows - NA_WROWS)
    start = pl.multiple_of(w0 * GRID_W, GRID_W)
    n_loc = NA_WROWS * GRID_W
    qidx = lax.broadcasted_iota(jnp.int32, (TOKEN_TILE, n_loc), 0)
    kidx = lax.broadcasted_iota(jnp.int32, (TOKEN_TILE, n_loc), 1)
    q_row, q_col = r0 + qidx // GRID_W, qidx % GRID_W
    k_row, k_col = w0 + kidx // GRID_W, kidx % GRID_W
    row_start = jnp.clip(q_row - NA_KH // 2, 0, rows - NA_KH)
    col_start = jnp.clip(q_col - NA_KW // 2, 0, GRID_W - NA_KW)
    valid = ((k_row >= row_start) & (k_row < row_start + NA_KH)
             & (k_col >= col_start) & (k_col < col_start + NA_KW))
    cap_ref[...] = jnp.where(valid, jnp.inf, NEG_INF)
    base = w0 - r0 + NA_KH - 1 - NA_BIAS_LO

    def logits(head):
        pair, hh = divmod(head, 2)
        cols = slice(pair * LANES, (pair + 1) * LANES)
        q = q_ref[:, cols]
        qh = jnp.where(_lane_mask(hh * NA_DH, (hh + 1) * NA_DH), q, jnp.zeros_like(q))
        return _dot_t(qh, k_ref[pl.ds(start, n_loc), cols]), _dot_t(qh, k_ref[seq:, cols])

    s_next = logits(0)
    for pair in range(NA_HEADS // 2):
        cols = slice(pair * LANES, (pair + 1) * LANES)
        v_loc, v_ctx = v_ref[pl.ds(start, n_loc), cols], v_ref[seq:, cols]
        out = None
        for hh in range(2):
            head = 2 * pair + hh
            head_lanes = _lane_mask(hh * NA_DH, (hh + 1) * NA_DH)
            s_loc, s_ctx = s_next
            if head + 1 < NA_HEADS:
                s_next = logits(head + 1)
            bias = jnp.concatenate(
                [jnp.concatenate([bias_ref[head, base + 2 * p - a] for p in range(NA_PAIRS)], axis=1)
                 for a in range(NA_QROWS)], axis=0)
            s_loc = jnp.minimum(s_loc + bias, cap_ref[...])
            m = jnp.maximum(jnp.max(s_loc, axis=-1, keepdims=True), jnp.max(s_ctx, axis=-1, keepdims=True))
            p_loc, p_ctx = jnp.exp2(s_loc - m), jnp.exp2(s_ctx - m)
            rcp = 1.0 / (jnp.sum(p_loc, axis=-1, keepdims=True) + jnp.sum(p_ctx, axis=-1, keepdims=True))
            o = (_dot(p_loc.astype(BF16), jnp.where(head_lanes, v_loc, jnp.zeros_like(v_loc)))
                 + _dot(p_ctx.astype(BF16), jnp.where(head_lanes, v_ctx, jnp.zeros_like(v_ctx)))) * rcp
            out = o if out is None else out + o
        o_ref[:, cols] = out.astype(BF16)


def _na_bias_table(rpb):
    n_rel, n_col = 2 * NA_KH - 1, 2 * NA_KW - 1
    left = GRID_W - NA_KW
    v = jnp.pad(rpb * LOG2E, ((0, 0), (0, 0), (left, 2 * GRID_W - left - n_col)))
    v = jnp.tile(v, (1, 1, GRID_W))[:, :, :GRID_W * (2 * GRID_W - 1)]
    toep = v.reshape(NA_HEADS, n_rel, GRID_W, 2 * GRID_W - 1)[..., GRID_W - 1:]
    toep = jnp.pad(toep, ((0, 0), (-NA_BIAS_LO, NA_BIAS_LO + NA_BIAS_N + 1 - n_rel), (0, 0), (0, 0)))
    return jnp.concatenate([toep[:, :-1], toep[:, 1:]], axis=-1)


def _na_attn(st, q, k, v, bias_tab):
    rows = st.seq // GRID_W
    assert rows >= NA_WROWS and rows % NA_QROWS == 0
    kern = functools.partial(_na_attn_kernel, rows=rows, seq=st.seq)
    return pl.pallas_call(
        kern,
        grid=(st.batch, st.lat_tiles),
        in_specs=[st.tile_spec(D_MODEL),
                  pl.BlockSpec((None, st.tokens, D_MODEL), lambda b, j: (b, 0, 0)),
                  pl.BlockSpec((None, st.tokens, D_MODEL), lambda b, j: (b, 0, 0)),
                  _resident((NA_HEADS, NA_BIAS_N, GRID_W, 2 * GRID_W))],
        out_specs=st.tile_spec(D_MODEL),
        out_shape=jax.ShapeDtypeStruct((st.batch, st.seq, D_MODEL), BF16),
        scratch_shapes=[pltpu.VMEM((TOKEN_TILE, NA_WROWS * GRID_W), F32)],
        compiler_params=_params(2),
        name="na_attn",
    )(q, k, v, bias_tab)


def _slot_columns(w, width):
    kdim = w.shape[0]
    w = w.reshape(kdim, -1, width)
    return jnp.pad(w, ((0, 0), (0, 0), (0, LANES - width))).reshape(kdim, -1)


def _slot_gain(g):
    return jnp.pad(g, ((0, 0), (0, LANES - g.shape[1])))


def kernel(x, c, ctx, c_ctx, ada_w, ada_b, norm_g, ffn_w_in, ffn_w_out, pool_w, pool_b, pool_scale,
           diff_w_qkv, diff_qk_g, diff_lambda, diff_sub_g, diff_w_o,
           mla_w_dq, mla_q_g, mla_w_uq, mla_w_dkv, mla_kv_g, mla_w_ukv, mla_qk_g, mla_w_o,
           na_w_qkv, na_qk_g, na_rpb, na_w_o):
    assert DEPTH == N_MIXERS, "one layer per mixer: the context stream ends after the last softmax mixer's keys"
    batch, seq, _ = x.shape
    st = _Stream(batch, seq, ctx.shape[1])
    rows =-(-(batch + 1) // 16) * 16
    cc = jnp.zeros((rows, D_MODEL), F32).at[:batch].set(c).at[batch].set(c_ctx)
    mods = _adaln(cc, ada_w, ada_b)

    w_in = ffn_w_in.astype(BF16)
    w_out = ffn_w_out.astype(BF16)
    gains = norm_g.reshape(DEPTH, 3, 1, D_MODEL)

    for i in range(DEPTH):
        mod = mods[i]
        if i == 0:
            xs = _ffn_join(st, x, ctx, mod, gains[i, 0], w_in, w_out)
            xs = _pool(st, xs, mod, gains[i, 1], pool_w[0].astype(BF16), pool_b[0].reshape(1, D_MODEL),
                       pool_scale[0].reshape(1, D_MODEL))
            xs = _ffn(st, xs, mod, 2, gains[i, 2], w_in, w_out, i)
            continue
        xs = _ffn(st, xs, mod, 0, gains[i, 0], w_in, w_out, i)
        if i == 1:
            q, k, v = _qkv_pre(st, xs, mod, gains[i, 1], diff_w_qkv[0].astype(BF16), diff_qk_g[0], True)
            lam_init = 0.8 - 0.6 * math.exp(-0.3 * i)
            attn = _diff_attn(st, q, k, v, diff_lambda[0], diff_sub_g[0], lam_init)
            w_o = diff_w_o[0]
        elif i == 2:
            kpe_cols = jnp.pad(mla_w_dkv[0][:, MLA_KV_RANK:], ((0, 0), (MLA_NOPE, LANES - MLA_NOPE - MLA_ROPE)))
            w_down = jnp.concatenate([mla_w_dq[0], mla_w_dkv[0][:, :MLA_KV_RANK], kpe_cols], axis=1)
            ukv = mla_w_ukv[0].reshape(MLA_KV_RANK, MLA_HEADS, MLA_NOPE + MLA_V)
            w_ukv = jnp.concatenate([_slot_columns(ukv[:, :, :MLA_NOPE].reshape(MLA_KV_RANK, -1), MLA_NOPE),
                                     ukv[:, :, MLA_NOPE:].reshape(MLA_KV_RANK, -1)], axis=1)
            q, k, v = _mla_pre(st, xs, mod, gains[i, 1], w_down.astype(BF16),
                               mla_q_g[0].reshape(1, -1), mla_kv_g[0].reshape(1, -1),
                               _slot_columns(mla_w_uq[0], MLA_NOPE + MLA_ROPE).astype(BF16),
                               w_ukv.astype(BF16), _slot_gain(mla_qk_g[0]))
            attn = _mla_attn(st, q, k, v)
            w_o = mla_w_o[0]
        else:
            q, k, v = _qkv_pre(st, xs, mod, gains[i, 1], na_w_qkv[0].astype(BF16), na_qk_g[0], False)
            attn = _na_attn(st, q, k, v, _na_bias_table(na_rpb[0]))
            w_o = na_w_o[0]
        xs = _mixout_ffn(st, xs, attn, mod, gains[i, 2], w_o.astype(BF16), w_in, w_out, i,
                         latent_only=(i == DEPTH - 1))
    return xs
```

```python
import functools
import math

import jax
import jax.numpy as jnp
from jax import lax
from jax.experimental import pallas as pl
from jax.experimental.pallas import tpu as pltpu

D_MODEL = 1024
DEPTH = 4
GRID_W = 64
N_MIXERS = 4
N_MOD = 9
EPS = 1e-6
LOG2E = math.log2(math.e)
ROPE_BASE = 10000.0
NEG_INF = -1e30
D_FF = 2816
POOL_WINDOWS = (2, 4, 8, 16)
POOL_G = D_MODEL // len(POOL_WINDOWS)
DIFF_HEADS = 8
DIFF_DH = 64
MLA_HEADS = 16
MLA_Q_RANK = 384
MLA_KV_RANK = 256
MLA_NOPE = 64
MLA_ROPE = 32
MLA_V = 64
NA_HEADS = 16
NA_DH = 64
NA_KH = 8
NA_KW = 16

LANES = 128
HALO = 8
TOKEN_TILE = 256
ATTN_TILE = 512
VMEM_LIMIT = 56 * 2**20

F32 = jnp.float32
BF16 = jnp.bfloat16


def _params(n_axes):
    return pltpu.CompilerParams(dimension_semantics=("arbitrary",) * n_axes,
                                vmem_limit_bytes=VMEM_LIMIT)


def _resident(shape, stack_index=()):
    index = tuple(stack_index) + (0,) * len(shape)
    return pl.BlockSpec((None,) * len(stack_index) + tuple(shape), lambda *_: index, pipeline_mode=pl.Buffered(1))


def _dot(a, b):
    return jnp.dot(a, b, preferred_element_type=F32)


def _dot_t(a, b):
    return lax.dot_general(a, b, (((1,), (1,)), ((), ())), preferred_element_type=F32)


def _modulate(x, g, shift, scale):
    ms = jnp.mean(x * x, axis=-1, keepdims=True)
    return (x * lax.rsqrt(ms + EPS)) * (g * (1.0 + scale)) + shift


def _silu(x):
    return x * (1.0 / (1.0 + jnp.exp(-x)))


def _adaln_kernel(c_ref, w_ref, b_ref, o_ref):
    s = _silu(c_ref[...]).astype(BF16)
    o_ref[...] = _dot(s, w_ref[...].astype(BF16)) + b_ref[...]


def _adaln(cc, ada_w, ada_b):
    rows = cc.shape[0]
    n_col = N_MOD * D_MODEL // D_MODEL
    out = pl.pallas_call(
        _adaln_kernel,
        grid=(DEPTH, n_col),
        in_specs=[pl.BlockSpec((rows, D_MODEL), lambda i, n: (0, 0)),
                  pl.BlockSpec((None, D_MODEL, D_MODEL), lambda i, n: (i, 0, n)),
                  pl.BlockSpec((None, 1, D_MODEL), lambda i, n: (i, 0, n))],
        out_specs=pl.BlockSpec((None, rows, D_MODEL), lambda i, n: (i, 0, n)),
        out_shape=jax.ShapeDtypeStruct((DEPTH, rows, N_MOD * D_MODEL), F32),
        compiler_params=_params(2),
        name="adaln",
    )(cc, ada_w, ada_b.reshape(DEPTH, 1, N_MOD * D_MODEL))
    return out.reshape(DEPTH, rows, N_MOD, D_MODEL)


def _ffn_core(x, mod_ref, k, g_ref, win_ref, wout_ref):
    shift = mod_ref[3 * k:3 * k + 1, :]
    scale = mod_ref[3 * k + 1:3 * k + 2, :]
    gate = mod_ref[3 * k + 2:3 * k + 3, :]
    parts = [x[r:r + TOKEN_TILE] for r in range(0, x.shape[0], TOKEN_TILE)]
    ups = [_dot(_modulate(p, g_ref[...], shift, scale).astype(BF16), win_ref[...]) for p in parts]
    downs = [_dot((_silu(u[:, :D_FF]) * u[:, D_FF:]).astype(BF16), wout_ref[...]) for u in ups]
    outs = [p + (0.5 * gate) * y for p, y in zip(parts, downs)]
    return outs[0] if len(outs) == 1 else jnp.concatenate(outs, axis=0)


def _wide_tile_branches(axis, seq, body, with_ctx=True):
    j = pl.program_id(axis)
    lat_steps = seq // ATTN_TILE

    @pl.when(j < lat_steps)
    def _():
        body(slice(None), slice(None))

    if with_ctx:
        @pl.when(j >= lat_steps)
        def _():
            body(slice(0, TOKEN_TILE), slice(seq, None))


def _ffn_kernel(x_ref, mod_ref, g_ref, win_ref, wout_ref, o_ref, *, k, seq):
    def step(rows, _):
        o_ref[rows, :] = _ffn_core(x_ref[rows, :], mod_ref, k, g_ref, win_ref, wout_ref)

    _wide_tile_branches(1, seq, step)


def _ffn_join_kernel(x_ref, c_ref, mod_ref, g_ref, win_ref, wout_ref, o_ref, *, seq):
    j = pl.program_id(1)

    @pl.when(j < seq // ATTN_TILE)
    def _():
        o_ref[...] = _ffn_core(x_ref[...], mod_ref, 0, g_ref, win_ref, wout_ref)

    @pl.when(j >= seq // ATTN_TILE)
    def _():
        o_ref[0:TOKEN_TILE, :] = _ffn_core(c_ref[...], mod_ref, 0, g_ref, win_ref, wout_ref)


def _mixout_ffn_kernel(x_ref, a_ref, mod_ref, g_ref, wo_ref, win_ref, wout_ref, o_ref, *, seq, with_ctx):
    def step(rows, _):
        x = x_ref[rows, :] + mod_ref[5:6, :] * _dot(a_ref[rows, :], wo_ref[...])
        o_ref[rows, :] = _ffn_core(x, mod_ref, 2, g_ref, win_ref, wout_ref)

    _wide_tile_branches(1, seq, step, with_ctx)


class _Stream:
    def __init__(self, batch, seq, ctx):
        assert seq % TOKEN_TILE == 0 and ctx == TOKEN_TILE and seq % GRID_W == 0
        self.batch, self.seq, self.ctx = batch, seq, ctx
        self.tokens = seq + ctx
        self.lat_tiles = seq // TOKEN_TILE
        self.tiles = self.tokens // TOKEN_TILE
        assert seq % ATTN_TILE == 0 and ATTN_TILE >= ctx
        self.attn_steps = seq // ATTN_TILE + 1

    def tile_spec(self, width):
        return pl.BlockSpec((None, TOKEN_TILE, width), lambda b, j: (b, j, 0))

    def wide_spec(self, width):
        return pl.BlockSpec((None, ATTN_TILE, width), lambda b, j: (b, j, 0))

    def mod_spec(self, rows=TOKEN_TILE):
        lat, ctx_row = self.seq // rows, self.batch
        return pl.BlockSpec((None, N_MOD, D_MODEL),
                            lambda b, j: (jnp.where(j < lat, b, ctx_row), 0, 0))


def _ffn_weight_specs(layer, half):
    return [_resident((D_MODEL, 2 * D_FF), (layer, half)), _resident((D_FF, D_MODEL), (layer, half))]


def _ffn(st, xs, mod, k, g, w_in, w_out, layer):
    return pl.pallas_call(
        functools.partial(_ffn_kernel, k=k, seq=st.seq),
        grid=(st.batch, st.attn_steps),
        in_specs=[st.wide_spec(D_MODEL), st.mod_spec(ATTN_TILE), _resident((1, D_MODEL))]
        + _ffn_weight_specs(layer, k // 2),
        out_specs=st.wide_spec(D_MODEL),
        out_shape=jax.ShapeDtypeStruct((st.batch, st.tokens, D_MODEL), F32),
        compiler_params=_params(2),
        name="ffn",
    )(xs, mod, g, w_in, w_out)


def _ffn_join(st, x, ctx, mod, g, w_in, w_out):
    last_lat = st.attn_steps - 2
    return pl.pallas_call(
        functools.partial(_ffn_join_kernel, seq=st.seq),
        grid=(st.batch, st.attn_steps),
        in_specs=[pl.BlockSpec((None, ATTN_TILE, D_MODEL), lambda b, j: (b, jnp.minimum(j, last_lat), 0)),
                  pl.BlockSpec((None, st.ctx, D_MODEL), lambda b, j: (b, 0, 0)),
                  st.mod_spec(ATTN_TILE), _resident((1, D_MODEL))] + _ffn_weight_specs(0, 0),
        out_specs=st.wide_spec(D_MODEL),
        out_shape=jax.ShapeDtypeStruct((st.batch, st.tokens, D_MODEL), F32),
        compiler_params=_params(2),
        name="ffn_join",
    )(x, ctx, mod, g, w_in, w_out)


def _mixout_ffn(st, xs, attn, mod, g, w_o, w_in, w_out, layer, latent_only):
    steps = st.attn_steps - 1 if latent_only else st.attn_steps
    tokens = st.seq if latent_only else st.tokens
    return pl.pallas_call(
        functools.partial(_mixout_ffn_kernel, seq=st.seq, with_ctx=not latent_only),
        grid=(st.batch, steps),
        in_specs=[st.wide_spec(D_MODEL), st.wide_spec(D_MODEL), st.mod_spec(ATTN_TILE),
                  _resident((1, D_MODEL)), _resident((D_MODEL, D_MODEL))] + _ffn_weight_specs(layer, 1),
        out_specs=st.wide_spec(D_MODEL),
        out_shape=jax.ShapeDtypeStruct((st.batch, tokens, D_MODEL), F32),
        compiler_params=_params(2),
        name="mixout_ffn",
    )(xs, attn, mod, g, w_o, w_in, w_out)


def _pool_kernel(x_ref, xp_ref, xn_ref, mod_ref, g_ref, w_ref, b_ref, s_ref, o_ref, hext_ref,
                 *, lat_tiles, tiles, seq, ctx):
    j = pl.program_id(1)
    g, shift, scale = g_ref[...], mod_ref[3:4, :], mod_ref[4:5, :]
    x = x_ref[...]
    h = _modulate(x, g, shift, scale)
    has_prev = jnp.logical_and(j != 0, j != lat_tiles)
    has_next = jnp.logical_and(j != lat_tiles - 1, j != tiles - 1)
    hext_ref[0:HALO, :] = jnp.where(has_prev, _modulate(xp_ref[...], g, shift, scale), 0.0)
    hext_ref[HALO:HALO + TOKEN_TILE, :] = h
    hext_ref[HALO + TOKEN_TILE:, :] = jnp.where(has_next, _modulate(xn_ref[...], g, shift, scale), 0.0)
    in_lat = j < lat_tiles
    t = (j - jnp.where(in_lat, 0, lat_tiles)) * TOKEN_TILE + lax.broadcasted_iota(jnp.int32, (TOKEN_TILE, 1), 0)
    n = jnp.where(in_lat, seq, ctx)
    for gi, win in enumerate(POOL_WINDOWS):
        cols = slice(gi * POOL_G, (gi + 1) * POOL_G)
        acc = hext_ref[HALO - win // 2:HALO - win // 2 + TOKEN_TILE, cols]
        for k in range(1 - win // 2, win - win // 2):
            acc = acc + hext_ref[HALO + k:HALO + k + TOKEN_TILE, cols]
        lo = jnp.maximum(t - win // 2, 0)
        hi = jnp.minimum(t - win // 2 + win, n)
        mean = acc / (hi - lo).astype(F32)
        d = (mean - h[:, cols]).astype(BF16)
        y = (_dot(d, w_ref[gi]) + b_ref[:, cols]) * s_ref[:, cols]
        o_ref[:, cols] = x[:, cols] + mod_ref[5:6, cols] * y


def _pool(st, xs, mod, g, w, b, scale):
    per_tile = TOKEN_TILE // HALO
    last = st.tokens // HALO - 1
    kern = functools.partial(_pool_kernel, lat_tiles=st.lat_tiles, tiles=st.tiles, seq=st.seq, ctx=st.ctx)
    return pl.pallas_call(
        kern,
        grid=(st.batch, st.tiles),
        in_specs=[st.tile_spec(D_MODEL),
                  pl.BlockSpec((None, HALO, D_MODEL), lambda b_, j: (b_, jnp.maximum(j * per_tile - 1, 0), 0)),
                  pl.BlockSpec((None, HALO, D_MODEL), lambda b_, j: (b_, jnp.minimum((j + 1) * per_tile, last), 0)),
                  st.mod_spec(), _resident((1, D_MODEL)),
                  _resident((len(POOL_WINDOWS), POOL_G, POOL_G)),
                  _resident((1, D_MODEL)), _resident((1, D_MODEL))],
        out_specs=st.tile_spec(D_MODEL),
        out_shape=jax.ShapeDtypeStruct((st.batch, st.tokens, D_MODEL), F32),
        scratch_shapes=[pltpu.VMEM((TOKEN_TILE + 2 * HALO, D_MODEL), F32)],
        compiler_params=_params(2),
        name="pool",
    )(xs, xs, xs, mod, g, w, b, scale)


def _lane_mask(lo, hi):
    lane = lax.broadcasted_iota(jnp.int32, (1, LANES), 1)
    return jnp.logical_and(lane >= lo, lane < hi)


def _segment_mean_matrix(lengths):
    assert sum(lengths) == LANES and all(n & (n - 1) == 0 for n in lengths)
    ids = jnp.repeat(jnp.arange(len(lengths)), jnp.array(lengths), total_repeat_length=LANES)
    inv = 1.0 / jnp.array(lengths, F32)[ids]
    m = jnp.where(ids[:, None] == ids[None, :], inv[None, :], 0.0)
    return jnp.tile(m, (2, 1)).astype(BF16)


def _segment_inv_rms(x, seg_ref):
    sq = x * x
    hi = sq.astype(BF16)
    lo = (sq - hi.astype(F32)).astype(BF16)
    return lax.rsqrt(_dot(jnp.concatenate([hi, lo], axis=1), seg_ref[...]) + EPS)


def _halves_inv_rms(x):
    sq = x * x
    lower = _lane_mask(0, LANES // 2)
    low = jnp.sum(jnp.where(lower, sq, 0.0), axis=-1, keepdims=True)
    high = jnp.sum(jnp.where(lower, 0.0, sq), axis=-1, keepdims=True)
    inv_n = 2.0 / LANES
    return jnp.where(lower, lax.rsqrt(low * inv_n + EPS), lax.rsqrt(high * inv_n + EPS))


def _half_values_with_ones(v, upper):
    lane = lax.broadcasted_iota(jnp.int32, (1, LANES), 1)
    keep = (lane >= LANES // 2) if upper else (lane < LANES // 2)
    ones_lane = 0 if upper else LANES // 2
    return jnp.where(keep, v, (lane == ones_lane).astype(v.dtype)), keep, ones_lane


def _rotate(z, tabs, shift):
    c_ref, u_ref, d_ref = tabs
    return (z * c_ref[...] + pltpu.roll(z, LANES - shift, axis=1) * u_ref[...]
            + pltpu.roll(z, shift, axis=1) * d_ref[...])


def _rope_tables(st, dim, lane_lo, period, gain, post):
    nf = dim // 4
    inv = ROPE_BASE ** (-jnp.arange(nf, dtype=F32) / nf)
    t = jnp.arange(st.seq, dtype=jnp.int32)
    pos = jnp.stack([t // GRID_W, t % GRID_W], axis=-1).astype(F32)
    ang = jnp.broadcast_to(pos[:, :, None, None] * inv, (st.seq, 2, 2, nf)).reshape(st.seq, dim)
    cos, sin = jnp.cos(ang), jnp.sin(ang)
    first_half = (jnp.arange(dim) % (2 * nf)) < nf
    sin_up = jnp.where(first_half, -sin, 0.0)
    sin_dn = jnp.where(first_half, 0.0, sin)

    def place(tab, fill, lane_gain):
        blk = jnp.full((st.seq, period), fill, F32).at[:, lane_lo:lane_lo + dim].set(tab)
        blk = jnp.tile(blk, (1, LANES // period))
        return jnp.concatenate([blk, jnp.full((st.ctx, LANES), fill, F32)], axis=0) * (lane_gain * post)

    return (place(cos, 1.0, gain), place(sin_up, 0.0, jnp.roll(gain, -nf)), place(sin_dn, 0.0, jnp.roll(gain, nf)))


def _qkv_pre_kernel(*refs, rope):
    if rope:
        x_ref, mod_ref, g_ref, w_ref, seg_ref, *tabs, q_ref, k_ref, v_ref = refs
    else:
        x_ref, mod_ref, g_ref, w_ref, gain_ref, q_ref, k_ref, v_ref = refs
    h = _modulate(x_ref[...], g_ref[...], mod_ref[3:4, :], mod_ref[4:5, :]).astype(BF16)
    def project(t):
        return _dot(h, w_ref[:, t * D_MODEL:(t + 1) * D_MODEL])

    def finish(t, x, dst):
        for c in range(D_MODEL // LANES):
            xb = x[:, c * LANES:(c + 1) * LANES]
            if rope:
                y = _rotate(xb * _segment_inv_rms(xb, seg_ref), tabs[3 * t:3 * t + 3], DIFF_DH // 4)
            else:
                y = xb * _halves_inv_rms(xb) * gain_ref[t:t + 1, :]
            dst[:, c * LANES:(c + 1) * LANES] = y.astype(BF16)

    q, k = project(0), project(1)
    finish(0, q, q_ref)
    v = project(2)
    finish(1, k, k_ref)
    v_ref[...] = v.astype(BF16)


def _qkv_pre(st, xs, mod, g, w_qkv, qk_gain, rope):
    tab_spec = pl.BlockSpec((TOKEN_TILE, LANES), lambda b, j: (j, 0))
    in_specs = [st.tile_spec(D_MODEL), st.mod_spec(), _resident((1, D_MODEL)), _resident((D_MODEL, 3 * D_MODEL))]
    args = [xs, mod, g, w_qkv]
    lane_gain = jnp.tile(qk_gain, (1, 2))
    post = (DIFF_DH ** -0.5 * LOG2E, 1.0)
    if rope:
        in_specs += [_resident((2 * LANES, LANES))] + [tab_spec] * 6
        args += [_segment_mean_matrix((DIFF_DH, DIFF_DH))]
        for t in range(2):
            args += list(_rope_tables(st, DIFF_DH, 0, DIFF_DH, lane_gain[t], post[t]))
    else:
        in_specs += [_resident((2, LANES))]
        args += [lane_gain * jnp.array(post, F32)[:, None]]
    shape = jax.ShapeDtypeStruct((st.batch, st.tokens, D_MODEL), BF16)
    return pl.pallas_call(
        functools.partial(_qkv_pre_kernel, rope=rope),
        grid=(st.batch, st.tiles),
        in_specs=in_specs,
        out_specs=[st.tile_spec(D_MODEL)] * 3,
        out_shape=[shape] * 3,
        compiler_params=_params(2),
        name="qkv_pre",
    )(*args)


DIFF_HEADS_PER_STEP = 8


def _diff_attn_kernel(q_ref, k_ref, v_ref, lam_ref, subg_ref, o_ref, *, seq, lam_init):
    lp = lam_ref[...]
    lam = (jnp.exp(jnp.sum(lp[0:1] * lp[1:2], axis=-1, keepdims=True))
           - jnp.exp(jnp.sum(lp[2:3] * lp[3:4], axis=-1, keepdims=True)) + lam_init)

    def attend(query_rows, key_rows):
        def logits(hd):
            cols = slice(hd * LANES, (hd + 1) * LANES)
            q, k = q_ref[query_rows, cols], k_ref[key_rows, cols]
            return [_dot_t(jnp.where(_lane_mask(c * DIFF_DH, (c + 1) * DIFF_DH), q, jnp.zeros_like(q)), k)
                    for c in range(2)]

        s_next = logits(0)
        for hd in range(DIFF_HEADS_PER_STEP):
            cols = slice(hd * LANES, (hd + 1) * LANES)
            s_cur = s_next
            if hd + 1 < DIFF_HEADS_PER_STEP:
                s_next = logits(hd + 1)
            probs = []
            for s in s_cur:
                p = jnp.exp2(s - jnp.max(s, axis=-1, keepdims=True))
                probs.append((p, 1.0 / jnp.sum(p, axis=-1, keepdims=True)))
            (p0, r0), (p1, r1) = probs
            a = (p0 - p1 * (lam * r1 / r0)).astype(BF16)
            o = _dot(a, v_ref[key_rows, cols]) * r0
            ms = jnp.mean(o * o, axis=-1, keepdims=True)
            o_ref[query_rows, cols] = ((o * lax.rsqrt(ms + EPS)) * subg_ref[...] * (1.0 - lam_init)).astype(BF16)

    _wide_tile_branches(2, seq, attend)


def _diff_attn(st, q, k, v, lam_p, sub_g, lam_init):
    width = DIFF_HEADS_PER_STEP * LANES
    head = lambda b, h, j: (b, j, h)
    keys = lambda b, h, j: (b, 0, h)
    kern = functools.partial(_diff_attn_kernel, seq=st.seq, lam_init=lam_init)
    return pl.pallas_call(
        kern,
        grid=(st.batch, DIFF_HEADS // DIFF_HEADS_PER_STEP, st.attn_steps),
        in_specs=[pl.BlockSpec((None, ATTN_TILE, width), head),
                  pl.BlockSpec((None, st.tokens, width), keys),
                  pl.BlockSpec((None, st.tokens, width), keys),
                  _resident((4, DIFF_DH)), _resident((1, LANES))],
        out_specs=pl.BlockSpec((None, ATTN_TILE, width), head),
        out_shape=jax.ShapeDtypeStruct((st.batch, st.tokens, D_MODEL), BF16),
        compiler_params=_params(3),
        name="diff_attn",
    )(q, k, v, lam_p, sub_g.reshape(1, LANES))


MLA_DOWN = MLA_Q_RANK + MLA_KV_RANK + LANES


def _mla_pre_kernel(x_ref, mod_ref, g_ref, wd_ref, qg_ref, kvg_ref, wuq_ref, wukv_ref, seg_ref, kgain_ref,
                    qc_ref, qu_ref, qd_ref, kc_ref, ku_ref, kd_ref, q_ref, k_ref, v_ref):
    h = _modulate(x_ref[...], g_ref[...], mod_ref[3:4, :], mod_ref[4:5, :]).astype(BF16)
    down = _dot(h, wd_ref[...])

    def full_norm(z, gain):
        ms = jnp.mean(z * z, axis=-1, keepdims=True)
        return ((z * lax.rsqrt(ms + EPS)) * gain).astype(BF16)

    q = _dot(full_norm(down[:, :MLA_Q_RANK], qg_ref[...]), wuq_ref[...])
    kv = _dot(full_norm(down[:, MLA_Q_RANK:MLA_Q_RANK + MLA_KV_RANK], kvg_ref[...]), wukv_ref[...])
    kpe = down[:, MLA_Q_RANK + MLA_KV_RANK:]
    kpe = _rotate(kpe * _segment_inv_rms(kpe, seg_ref), (kc_ref, ku_ref, kd_ref), MLA_ROPE // 4)
    for hd in range(MLA_HEADS):
        cols = slice(hd * LANES, (hd + 1) * LANES)
        qb = q[:, cols]
        q_ref[:, cols] = _rotate(qb * _segment_inv_rms(qb, seg_ref), (qc_ref, qu_ref, qd_ref),
                                 MLA_ROPE // 4).astype(BF16)
        kb = kv[:, cols]
        k_ref[:, cols] = (kb * _segment_inv_rms(kb, seg_ref) * kgain_ref[...] + kpe).astype(BF16)
    v_ref[...] = kv[:, MLA_HEADS * LANES:].astype(BF16)


def _mla_pre(st, xs, mod, g, w_down, q_g, kv_g, w_uq, w_ukv, qk_gain):
    tab_spec = pl.BlockSpec((TOKEN_TILE, LANES), lambda b, j: (j, 0))
    slots = MLA_HEADS * LANES
    scale = (MLA_NOPE + MLA_ROPE) ** -0.5 * LOG2E
    q_tabs = _rope_tables(st, MLA_ROPE, MLA_NOPE, LANES, qk_gain[0], scale)
    k_tabs = _rope_tables(st, MLA_ROPE, MLA_NOPE, LANES, qk_gain[1], 1.0)
    seg = _segment_mean_matrix((MLA_NOPE, MLA_ROPE, LANES - MLA_NOPE - MLA_ROPE))
    return pl.pallas_call(
        _mla_pre_kernel,
        grid=(st.batch, st.tiles),
        in_specs=[st.tile_spec(D_MODEL), st.mod_spec(), _resident((1, D_MODEL)),
                  _resident((D_MODEL, MLA_DOWN)), _resident((1, MLA_Q_RANK)), _resident((1, MLA_KV_RANK)),
                  _resident((MLA_Q_RANK, slots)), _resident((MLA_KV_RANK, slots + D_MODEL)),
                  _resident((2 * LANES, LANES)), _resident((1, LANES))] + [tab_spec] * 6,
        out_specs=[st.tile_spec(slots), st.tile_spec(slots), st.tile_spec(D_MODEL)],
        out_shape=[jax.ShapeDtypeStruct((st.batch, st.tokens, slots), BF16),
                   jax.ShapeDtypeStruct((st.batch, st.tokens, slots), BF16),
                   jax.ShapeDtypeStruct((st.batch, st.tokens, D_MODEL), BF16)],
        compiler_params=_params(2),
        name="mla_pre",
    )(xs, mod, g, w_down, q_g, kv_g, w_uq, w_ukv, seg, qk_gain[1:2], *q_tabs, *k_tabs)


MLA_PAIRS_PER_STEP = 4


def _mla_attn_kernel(q_ref, k_ref, v_ref, o_ref, *, seq):
    def attend(query_rows, key_rows):
        def logits(head):
            slot = slice(head * LANES, (head + 1) * LANES)
            return _dot_t(q_ref[query_rows, slot], k_ref[key_rows, slot])

        s_next = logits(0)
        for pair in range(MLA_PAIRS_PER_STEP):
            v = v_ref[key_rows, pair * LANES:(pair + 1) * LANES]
            out = None
            for hh in range(2):
                head = 2 * pair + hh
                s = s_next
                if head + 1 < 2 * MLA_PAIRS_PER_STEP:
                    s_next = logits(head + 1)
                p = jnp.exp2(s - jnp.max(s, axis=-1, keepdims=True))
                vh, keep, ones_lane = _half_values_with_ones(v, hh == 1)
                o = _dot(p.astype(BF16), vh)
                o = o * (1.0 / o[:, ones_lane:ones_lane + 1])
                out = o if out is None else jnp.where(keep, o, out)
            o_ref[query_rows, pair * LANES:(pair + 1) * LANES] = out.astype(BF16)

    _wide_tile_branches(2, seq, attend)


def _mla_attn(st, q, k, v):
    slots = 2 * MLA_PAIRS_PER_STEP * LANES
    vals = MLA_PAIRS_PER_STEP * LANES
    kern = functools.partial(_mla_attn_kernel, seq=st.seq)
    return pl.pallas_call(
        kern,
        grid=(st.batch, MLA_HEADS // (2 * MLA_PAIRS_PER_STEP), st.attn_steps),
        in_specs=[pl.BlockSpec((None, ATTN_TILE, slots), lambda b, h, j: (b, j, h)),
                  pl.BlockSpec((None, st.tokens, slots), lambda b, h, j: (b, 0, h)),
                  pl.BlockSpec((None, st.tokens, vals), lambda b, h, j: (b, 0, h))],
        out_specs=pl.BlockSpec((None, ATTN_TILE, vals), lambda b, h, j: (b, j, h)),
        out_shape=jax.ShapeDtypeStruct((st.batch, st.tokens, D_MODEL), BF16),
        compiler_params=_params(3),
        name="mla_attn",
    )(q, k, v)


NA_QROWS = TOKEN_TILE // GRID_W
NA_WROWS = NA_QROWS + NA_KH
NA_PAIRS = NA_WROWS // 2
NA_BIAS_LO = -(NA_QROWS - 1) - (NA_WROWS - NA_QROWS - NA_KH // 2) + NA_KH - 1
NA_BIAS_N = (NA_WROWS - 2) + NA_KH - 1 - NA_BIAS_LO + 1


def _na_attn_kernel(q_ref, k_ref, v_ref, bias_ref, o_ref, cap_ref, *, rows, seq):
    r0 = pl.program_id(1) * NA_QROWS
    w0 = jnp.clip(r0 - NA_KH // 2, 0, rows - NA_WROWS)
    start = pl.multiple_of(w0 * GRID_W, GRID_W)
    n_loc = NA_WROWS * GRID_W
    qidx = lax.broadcasted_iota(jnp.int32, (TOKEN_TILE, n_loc), 0)
    kidx = lax.broadcasted_iota(jnp.int32, (TOKEN_TILE, n_loc), 1)
    q_row, q_col = r0 + qidx // GRID_W, qidx % GRID_W
    k_row, k_col = w0 + kidx // GRID_W, kidx % GRID_W
    row_start = jnp.clip(q_row - NA_KH // 2, 0, rows - NA_KH)
    col_start = jnp.clip(q_col - NA_KW // 2, 0, GRID_W - NA_KW)
    valid = ((k_row >= row_start) & (k_row < row_start + NA_KH)
             & (k_col >= col_start) & (k_col < col_start + NA_KW))
    cap_ref[...] = jnp.where(valid, jnp.inf, NEG_INF)
    base = w0 - r0 + NA_KH - 1 - NA_BIAS_LO

    def logits(head):
        pair, hh = divmod(head, 2)
        cols = slice(pair * LANES, (pair + 1) * LANES)
        q = q_ref[:, cols]
        qh = jnp.where(_lane_mask(hh * NA_DH, (hh + 1) * NA_DH), q, jnp.zeros_like(q))
        return _dot_t(qh, k_ref[pl.ds(start, n_loc), cols]), _dot_t(qh, k_ref[seq:, cols])

    s_next = logits(0)
    for pair in range(NA_HEADS // 2):
        cols = slice(pair * LANES, (pair + 1) * LANES)
        v_loc, v_ctx = v_ref[pl.ds(start, n_loc), cols], v_ref[seq:, cols]
        out = None
        for hh in range(2):
            head = 2 * pair + hh
            s_loc, s_ctx = s_next
            if head + 1 < NA_HEADS:
                s_next = logits(head + 1)
            bias = jnp.concatenate(
                [jnp.concatenate([bias_ref[head, base + 2 * p - a] for p in range(NA_PAIRS)], axis=1)
                 for a in range(NA_QROWS)], axis=0)
            s_loc = jnp.minimum(s_loc + bias, cap_ref[...])
            m = jnp.maximum(jnp.max(s_loc, axis=-1, keepdims=True), jnp.max(s_ctx, axis=-1, keepdims=True))
            p_loc, p_ctx = jnp.exp2(s_loc - m), jnp.exp2(s_ctx - m)
            vh_loc, keep, ones_lane = _half_values_with_ones(v_loc, hh == 1)
            vh_ctx, _, _ = _half_values_with_ones(v_ctx, hh == 1)
            o = _dot(p_loc.astype(BF16), vh_loc) + _dot(p_ctx.astype(BF16), vh_ctx)
            o = o * (1.0 / o[:, ones_lane:ones_lane + 1])
            out = o if out is None else jnp.where(keep, o, out)
        o_ref[:, cols] = out.astype(BF16)


def _na_bias_table(rpb):
    n_rel, n_col = 2 * NA_KH - 1, 2 * NA_KW - 1
    left = GRID_W - NA_KW
    v = jnp.pad(rpb * LOG2E, ((0, 0), (0, 0), (left, 2 * GRID_W - left - n_col)))
    v = jnp.tile(v, (1, 1, GRID_W))[:, :, :GRID_W * (2 * GRID_W - 1)]
    toep = v.reshape(NA_HEADS, n_rel, GRID_W, 2 * GRID_W - 1)[..., GRID_W - 1:]
    toep = jnp.pad(toep, ((0, 0), (-NA_BIAS_LO, NA_BIAS_LO + NA_BIAS_N + 1 - n_rel), (0, 0), (0, 0)))
    return jnp.concatenate([toep[:, :-1], toep[:, 1:]], axis=-1)


def _na_attn(st, q, k, v, bias_tab):
    rows = st.seq // GRID_W
    assert rows >= NA_WROWS and rows % NA_QROWS == 0
    kern = functools.partial(_na_attn_kernel, rows=rows, seq=st.seq)
    return pl.pallas_call(
        kern,
        grid=(st.batch, st.lat_tiles),
        in_specs=[st.tile_spec(D_MODEL),
                  pl.BlockSpec((None, st.tokens, D_MODEL), lambda b, j: (b, 0, 0)),
                  pl.BlockSpec((None, st.tokens, D_MODEL), lambda b, j: (b, 0, 0)),
                  _resident((NA_HEADS, NA_BIAS_N, GRID_W, 2 * GRID_W))],
        out_specs=st.tile_spec(D_MODEL),
        out_shape=jax.ShapeDtypeStruct((st.batch, st.seq, D_MODEL), BF16),
        scratch_shapes=[pltpu.VMEM((TOKEN_TILE, NA_WROWS * GRID_W), F32)],
        compiler_params=_params(2),
        name="na_attn",
    )(q, k, v, bias_tab)


def _slot_columns(w, width):
    kdim = w.shape[0]
    w = w.reshape(kdim, -1, width)
    return jnp.pad(w, ((0, 0), (0, 0), (0, LANES - width))).reshape(kdim, -1)


def _slot_gain(g):
    return jnp.pad(g, ((0, 0), (0, LANES - g.shape[1])))


def kernel(x, c, ctx, c_ctx, ada_w, ada_b, norm_g, ffn_w_in, ffn_w_out, pool_w, pool_b, pool_scale,
           diff_w_qkv, diff_qk_g, diff_lambda, diff_sub_g, diff_w_o,
           mla_w_dq, mla_q_g, mla_w_uq, mla_w_dkv, mla_kv_g, mla_w_ukv, mla_qk_g, mla_w_o,
           na_w_qkv, na_qk_g, na_rpb, na_w_o):
    assert DEPTH == N_MIXERS, "one layer per mixer: the context stream ends after the last softmax mixer's keys"
    batch, seq, _ = x.shape
    st = _Stream(batch, seq, ctx.shape[1])
    rows =-(-(batch + 1) // 16) * 16
    cc = jnp.zeros((rows, D_MODEL), F32).at[:batch].set(c).at[batch].set(c_ctx)
    mods = _adaln(cc, ada_w, ada_b)

    w_in = ffn_w_in.astype(BF16)
    w_out = ffn_w_out.astype(BF16)
    gains = norm_g.reshape(DEPTH, 3, 1, D_MODEL)

    for i in range(DEPTH):
        mod = mods[i]
        if i == 0:
            xs = _ffn_join(st, x, ctx, mod, gains[i, 0], w_in, w_out)
            xs = _pool(st, xs, mod, gains[i, 1], pool_w[0].astype(BF16), pool_b[0].reshape(1, D_MODEL),
                       pool_scale[0].reshape(1, D_MODEL))
            xs = _ffn(st, xs, mod, 2, gains[i, 2], w_in, w_out, i)
            continue
        xs = _ffn(st, xs, mod, 0, gains[i, 0], w_in, w_out, i)
        if i == 1:
            q, k, v = _qkv_pre(st, xs, mod, gains[i, 1], diff_w_qkv[0].astype(BF16), diff_qk_g[0], True)
            lam_init = 0.8 - 0.6 * math.exp(-0.3 * i)
            attn = _diff_attn(st, q, k, v, diff_lambda[0], diff_sub_g[0], lam_init)
            w_o = diff_w_o[0]
        elif i == 2:
            kpe_cols = jnp.pad(mla_w_dkv[0][:, MLA_KV_RANK:], ((0, 0), (MLA_NOPE, LANES - MLA_NOPE - MLA_ROPE)))
            w_down = jnp.concatenate([mla_w_dq[0], mla_w_dkv[0][:, :MLA_KV_RANK], kpe_cols], axis=1)
            ukv = mla_w_ukv[0].reshape(MLA_KV_RANK, MLA_HEADS, MLA_NOPE + MLA_V)
            w_ukv = jnp.concatenate([_slot_columns(ukv[:, :, :MLA_NOPE].reshape(MLA_KV_RANK, -1), MLA_NOPE),
                                     ukv[:, :, MLA_NOPE:].reshape(MLA_KV_RANK, -1)], axis=1)
            q, k, v = _mla_pre(st, xs, mod, gains[i, 1], w_down.astype(BF16),
                               mla_q_g[0].reshape(1, -1), mla_kv_g[0].reshape(1, -1),
                               _slot_columns(mla_w_uq[0], MLA_NOPE + MLA_ROPE).astype(BF16),
                               w_ukv.astype(BF16), _slot_gain(mla_qk_g[0]))
            attn = _mla_attn(st, q, k, v)
            w_o = mla_w_o[0]
        else:
            q, k, v = _qkv_pre(st, xs, mod, gains[i, 1], na_w_qkv[0].astype(BF16), na_qk_g[0], False)
            attn = _na_attn(st, q, k, v, _na_bias_table(na_rpb[0]))
            w_o = na_w_o[0]
        xs = _mixout_ffn(st, xs, attn, mod, gains[i, 2], w_o.astype(BF16), w_in, w_out, i,
                         latent_only=(i == DEPTH - 1))
    return xs
```

```python
import functools
import math

import jax
import jax.numpy as jnp
from jax import lax
from jax.experimental import pallas as pl
from jax.experimental.pallas import tpu as pltpu

D_MODEL = 1024
DEPTH = 4
GRID_W = 64
N_MIXERS = 4
N_MOD = 9
EPS = 1e-6
LOG2E = math.log2(math.e)
ROPE_BASE = 10000.0
NEG_INF = -1e30
D_FF = 2816
POOL_WINDOWS = (2, 4, 8, 16)
POOL_G = D_MODEL // len(POOL_WINDOWS)
DIFF_HEADS = 8
DIFF_DH = 64
MLA_HEADS = 16
MLA_Q_RANK = 384
MLA_KV_RANK = 256
MLA_NOPE = 64
MLA_ROPE = 32
MLA_V = 64
NA_HEADS = 16
NA_DH = 64
NA_KH = 8
NA_KW = 16

LANES = 128
HALO = 8
TOKEN_TILE = 256
ATTN_TILE = 512
VMEM_LIMIT = 56 * 2**20

F32 = jnp.float32
BF16 = jnp.bfloat16


def _params(n_axes):
    return pltpu.CompilerParams(dimension_semantics=("arbitrary",) * n_axes,
                                vmem_limit_bytes=VMEM_LIMIT)


def _resident(shape, stack_index=()):
    index = tuple(stack_index) + (0,) * len(shape)
    return pl.BlockSpec((None,) * len(stack_index) + tuple(shape), lambda *_: index, pipeline_mode=pl.Buffered(1))


def _dot(a, b):
    return jnp.dot(a, b, preferred_element_type=F32)


def _dot_t(a, b):
    return lax.dot_general(a, b, (((1,), (1,)), ((), ())), preferred_element_type=F32)


def _modulate(x, g, shift, scale):
    ms = jnp.mean(x * x, axis=-1, keepdims=True)
    return (x * lax.rsqrt(ms + EPS)) * (g * (1.0 + scale)) + shift


def _silu(x):
    return x * (1.0 / (1.0 + jnp.exp(-x)))


def _adaln_kernel(c_ref, w_ref, b_ref, o_ref):
    s = _silu(c_ref[...]).astype(BF16)
    o_ref[...] = _dot(s, w_ref[...].astype(BF16)) + b_ref[...]


def _adaln(cc, ada_w, ada_b):
    rows = cc.shape[0]
    n_col = N_MOD * D_MODEL // D_MODEL
    out = pl.pallas_call(
        _adaln_kernel,
        grid=(DEPTH, n_col),
        in_specs=[pl.BlockSpec((rows, D_MODEL), lambda i, n: (0, 0)),
                  pl.BlockSpec((None, D_MODEL, D_MODEL), lambda i, n: (i, 0, n)),
                  pl.BlockSpec((None, 1, D_MODEL), lambda i, n: (i, 0, n))],
        out_specs=pl.BlockSpec((None, rows, D_MODEL), lambda i, n: (i, 0, n)),
        out_shape=jax.ShapeDtypeStruct((DEPTH, rows, N_MOD * D_MODEL), F32),
        compiler_params=_params(2),
        name="adaln",
    )(cc, ada_w, ada_b.reshape(DEPTH, 1, N_MOD * D_MODEL))
    return out.reshape(DEPTH, rows, N_MOD, D_MODEL)


def _ffn_core(x, mod_ref, k, g_ref, win_ref, wout_ref):
    shift = mod_ref[3 * k:3 * k + 1, :]
    scale = mod_ref[3 * k + 1:3 * k + 2, :]
    gate = mod_ref[3 * k + 2:3 * k + 3, :]
    parts = [x[r:r + TOKEN_TILE] for r in range(0, x.shape[0], TOKEN_TILE)]
    ups = [_dot(_modulate(p, g_ref[...], shift, scale).astype(BF16), win_ref[...]) for p in parts]
    downs = [_dot((_silu(u[:, :D_FF]) * u[:, D_FF:]).astype(BF16), wout_ref[...]) for u in ups]
    outs = [p + (0.5 * gate) * y for p, y in zip(parts, downs)]
    return outs[0] if len(outs) == 1 else jnp.concatenate(outs, axis=0)


def _wide_tile_branches(axis, seq, body, with_ctx=True):
    j = pl.program_id(axis)
    lat_steps = seq // ATTN_TILE

    @pl.when(j < lat_steps)
    def _():
        body(slice(None), slice(None))

    if with_ctx:
        @pl.when(j >= lat_steps)
        def _():
            body(slice(0, TOKEN_TILE), slice(seq, None))


def _ffn_kernel(x_ref, mod_ref, g_ref, win_ref, wout_ref, o_ref, *, k, seq):
    def step(rows, _):
        o_ref[rows, :] = _ffn_core(x_ref[rows, :], mod_ref, k, g_ref, win_ref, wout_ref)

    _wide_tile_branches(1, seq, step)


def _ffn_join_kernel(x_ref, c_ref, mod_ref, g_ref, win_ref, wout_ref, o_ref, *, seq):
    j = pl.program_id(1)

    @pl.when(j < seq // ATTN_TILE)
    def _():
        o_ref[...] = _ffn_core(x_ref[...], mod_ref, 0, g_ref, win_ref, wout_ref)

    @pl.when(j >= seq // ATTN_TILE)
    def _():
        o_ref[0:TOKEN_TILE, :] = _ffn_core(c_ref[...], mod_ref, 0, g_ref, win_ref, wout_ref)


def _mixout_ffn_kernel(x_ref, a_ref, mod_ref, g_ref, wo_ref, win_ref, wout_ref, o_ref, *, seq, with_ctx):
    def step(rows, _):
        x = x_ref[rows, :] + mod_ref[5:6, :] * _dot(a_ref[rows, :], wo_ref[...])
        o_ref[rows, :] = _ffn_core(x, mod_ref, 2, g_ref, win_ref, wout_ref)

    _wide_tile_branches(1, seq, step, with_ctx)


class _Stream:
    def __init__(self, batch, seq, ctx):
        assert seq % TOKEN_TILE == 0 and ctx == TOKEN_TILE and seq % GRID_W == 0
        self.batch, self.seq, self.ctx = batch, seq, ctx
        self.tokens = seq + ctx
        self.lat_tiles = seq // TOKEN_TILE
        self.tiles = self.tokens // TOKEN_TILE
        assert seq % ATTN_TILE == 0 and ATTN_TILE >= ctx
        self.attn_steps = seq // ATTN_TILE + 1

    def tile_spec(self, width):
        return pl.BlockSpec((None, TOKEN_TILE, width), lambda b, j: (b, j, 0))

    def wide_spec(self, width):
        return pl.BlockSpec((None, ATTN_TILE, width), lambda b, j: (b, j, 0))

    def mod_spec(self, rows=TOKEN_TILE):
        lat, ctx_row = self.seq // rows, self.batch
        return pl.BlockSpec((None, N_MOD, D_MODEL),
                            lambda b, j: (jnp.where(j < lat, b, ctx_row), 0, 0))


def _ffn_weight_specs(layer, half):
    return [_resident((D_MODEL, 2 * D_FF), (layer, half)), _resident((D_FF, D_MODEL), (layer, half))]


def _ffn(st, xs, mod, k, g, w_in, w_out, layer):
    return pl.pallas_call(
        functools.partial(_ffn_kernel, k=k, seq=st.seq),
        grid=(st.batch, st.attn_steps),
        in_specs=[st.wide_spec(D_MODEL), st.mod_spec(ATTN_TILE), _resident((1, D_MODEL))]
        + _ffn_weight_specs(layer, k // 2),
        out_specs=st.wide_spec(D_MODEL),
        out_shape=jax.ShapeDtypeStruct((st.batch, st.tokens, D_MODEL), F32),
        compiler_params=_params(2),
        name="ffn",
    )(xs, mod, g, w_in, w_out)


def _ffn_join(st, x, ctx, mod, g, w_in, w_out):
    last_lat = st.attn_steps - 2
    return pl.pallas_call(
        functools.partial(_ffn_join_kernel, seq=st.seq),
        grid=(st.batch, st.attn_steps),
        in_specs=[pl.BlockSpec((None, ATTN_TILE, D_MODEL), lambda b, j: (b, jnp.minimum(j, last_lat), 0)),
                  pl.BlockSpec((None, st.ctx, D_MODEL), lambda b, j: (b, 0, 0)),
                  st.mod_spec(ATTN_TILE), _resident((1, D_MODEL))] + _ffn_weight_specs(0, 0),
        out_specs=st.wide_spec(D_MODEL),
        out_shape=jax.ShapeDtypeStruct((st.batch, st.tokens, D_MODEL), F32),
        compiler_params=_params(2),
        name="ffn_join",
    )(x, ctx, mod, g, w_in, w_out)


def _mixout_ffn(st, xs, attn, mod, g, w_o, w_in, w_out, layer, latent_only):
    steps = st.attn_steps - 1 if latent_only else st.attn_steps
    tokens = st.seq if latent_only else st.tokens
    return pl.pallas_call(
        functools.partial(_mixout_ffn_kernel, seq=st.seq, with_ctx=not latent_only),
        grid=(st.batch, steps),
        in_specs=[st.wide_spec(D_MODEL), st.wide_spec(D_MODEL), st.mod_spec(ATTN_TILE),
                  _resident((1, D_MODEL)), _resident((D_MODEL, D_MODEL))] + _ffn_weight_specs(layer, 1),
        out_specs=st.wide_spec(D_MODEL),
        out_shape=jax.ShapeDtypeStruct((st.batch, tokens, D_MODEL), F32),
        compiler_params=_params(2),
        name="mixout_ffn",
    )(xs, attn, mod, g, w_o, w_in, w_out)


def _window_sum(e, start, win, rows):
    steps = [win >> (i + 1) for i in range(win.bit_length() - 1)]
    cur, base = e, start
    for idx, shift in enumerate(steps):
        n = rows + HALO * (len(steps) - 1 - idx)
        cur = cur[base:base + n] + cur[base + shift:base + shift + n]
        base = 0
    return cur


def _pool_kernel(x_ref, xp_ref, xn_ref, mod_ref, g_ref, w_ref, b_ref, s_ref, o_ref, hext_ref,
                 *, lat_tiles, tiles, seq, ctx):
    j = pl.program_id(1)
    g, shift, scale = g_ref[...], mod_ref[3:4, :], mod_ref[4:5, :]
    x = x_ref[...]
    h = _modulate(x, g, shift, scale)
    has_prev = jnp.logical_and(j != 0, j != lat_tiles)
    has_next = jnp.logical_and(j != lat_tiles - 1, j != tiles - 1)
    hext_ref[0:HALO, :] = jnp.where(has_prev, _modulate(xp_ref[...], g, shift, scale), 0.0)
    hext_ref[HALO:HALO + TOKEN_TILE, :] = h
    hext_ref[HALO + TOKEN_TILE:2 * HALO + TOKEN_TILE, :] = jnp.where(
        has_next, _modulate(xn_ref[...], g, shift, scale), 0.0)
    hext_ref[2 * HALO + TOKEN_TILE:, :] = jnp.zeros((2 * HALO, D_MODEL), F32)
    in_lat = j < lat_tiles
    t = (j - jnp.where(in_lat, 0, lat_tiles)) * TOKEN_TILE + lax.broadcasted_iota(jnp.int32, (TOKEN_TILE, 1), 0)
    n = jnp.where(in_lat, seq, ctx)
    for gi, win in enumerate(POOL_WINDOWS):
        cols = slice(gi * POOL_G, (gi + 1) * POOL_G)
        acc = _window_sum(hext_ref[:, cols], HALO - win // 2, win, TOKEN_TILE)
        lo = jnp.maximum(t - win // 2, 0)
        hi = jnp.minimum(t - win // 2 + win, n)
        mean = acc / (hi - lo).astype(F32)
        d = (mean - h[:, cols]).astype(BF16)
        y = (_dot(d, w_ref[gi]) + b_ref[:, cols]) * s_ref[:, cols]
        o_ref[:, cols] = x[:, cols] + mod_ref[5:6, cols] * y


def _pool(st, xs, mod, g, w, b, scale):
    per_tile = TOKEN_TILE // HALO
    last = st.tokens // HALO - 1
    kern = functools.partial(_pool_kernel, lat_tiles=st.lat_tiles, tiles=st.tiles, seq=st.seq, ctx=st.ctx)
    return pl.pallas_call(
        kern,
        grid=(st.batch, st.tiles),
        in_specs=[st.tile_spec(D_MODEL),
                  pl.BlockSpec((None, HALO, D_MODEL), lambda b_, j: (b_, jnp.maximum(j * per_tile - 1, 0), 0)),
                  pl.BlockSpec((None, HALO, D_MODEL), lambda b_, j: (b_, jnp.minimum((j + 1) * per_tile, last), 0)),
                  st.mod_spec(), _resident((1, D_MODEL)),
                  _resident((len(POOL_WINDOWS), POOL_G, POOL_G)),
                  _resident((1, D_MODEL)), _resident((1, D_MODEL))],
        out_specs=st.tile_spec(D_MODEL),
        out_shape=jax.ShapeDtypeStruct((st.batch, st.tokens, D_MODEL), F32),
        scratch_shapes=[pltpu.VMEM((TOKEN_TILE + 4 * HALO, D_MODEL), F32)],
        compiler_params=_params(2),
        name="pool",
    )(xs, xs, xs, mod, g, w, b, scale)


def _lane_mask(lo, hi):
    lane = lax.broadcasted_iota(jnp.int32, (1, LANES), 1)
    return jnp.logical_and(lane >= lo, lane < hi)


def _segment_mean_matrix(lengths):
    assert sum(lengths) == LANES and all(n & (n - 1) == 0 for n in lengths)
    ids = jnp.repeat(jnp.arange(len(lengths)), jnp.array(lengths), total_repeat_length=LANES)
    inv = 1.0 / jnp.array(lengths, F32)[ids]
    m = jnp.where(ids[:, None] == ids[None, :], inv[None, :], 0.0)
    return jnp.tile(m, (2, 1)).astype(BF16)


def _segment_inv_rms(x, seg_ref):
    sq = x * x
    hi = sq.astype(BF16)
    lo = (sq - hi.astype(F32)).astype(BF16)
    return lax.rsqrt(_dot(jnp.concatenate([hi, lo], axis=1), seg_ref[...]) + EPS)


def _halves_inv_rms(x):
    sq = x * x
    lower = _lane_mask(0, LANES // 2)
    low = jnp.sum(jnp.where(lower, sq, 0.0), axis=-1, keepdims=True)
    high = jnp.sum(jnp.where(lower, 0.0, sq), axis=-1, keepdims=True)
    inv_n = 2.0 / LANES
    return jnp.where(lower, lax.rsqrt(low * inv_n + EPS), lax.rsqrt(high * inv_n + EPS))


def _half_values_with_ones(v, upper):
    lane = lax.broadcasted_iota(jnp.int32, (1, LANES), 1)
    keep = (lane >= LANES // 2) if upper else (lane < LANES // 2)
    ones_lane = 0 if upper else LANES // 2
    return jnp.where(keep, v, (lane == ones_lane).astype(v.dtype)), keep, ones_lane


def _rotate(z, tabs, shift):
    c_ref, u_ref, d_ref = tabs
    return (z * c_ref[...] + pltpu.roll(z, LANES - shift, axis=1) * u_ref[...]
            + pltpu.roll(z, shift, axis=1) * d_ref[...])


def _rope_tables(st, dim, lane_lo, period, gain, post):
    nf = dim // 4
    inv = ROPE_BASE ** (-jnp.arange(nf, dtype=F32) / nf)
    t = jnp.arange(st.seq, dtype=jnp.int32)
    pos = jnp.stack([t // GRID_W, t % GRID_W], axis=-1).astype(F32)
    ang = jnp.broadcast_to(pos[:, :, None, None] * inv, (st.seq, 2, 2, nf)).reshape(st.seq, dim)
    cos, sin = jnp.cos(ang), jnp.sin(ang)
    first_half = (jnp.arange(dim) % (2 * nf)) < nf
    sin_up = jnp.where(first_half, -sin, 0.0)
    sin_dn = jnp.where(first_half, 0.0, sin)

    def place(tab, fill, lane_gain):
        blk = jnp.full((st.seq, period), fill, F32).at[:, lane_lo:lane_lo + dim].set(tab)
        blk = jnp.tile(blk, (1, LANES // period))
        return jnp.concatenate([blk, jnp.full((st.ctx, LANES), fill, F32)], axis=0) * (lane_gain * post)

    return (place(cos, 1.0, gain), place(sin_up, 0.0, jnp.roll(gain, -nf)), place(sin_dn, 0.0, jnp.roll(gain, nf)))


def _qkv_pre_kernel(*refs, rope):
    if rope:
        x_ref, mod_ref, g_ref, w_ref, seg_ref, *tabs, q_ref, k_ref, v_ref = refs
    else:
        x_ref, mod_ref, g_ref, w_ref, gain_ref, q_ref, k_ref, v_ref = refs
    h = _modulate(x_ref[...], g_ref[...], mod_ref[3:4, :], mod_ref[4:5, :]).astype(BF16)
    def project(t):
        return _dot(h, w_ref[:, t * D_MODEL:(t + 1) * D_MODEL])

    def finish(t, x, dst):
        for c in range(D_MODEL // LANES):
            xb = x[:, c * LANES:(c + 1) * LANES]
            if rope:
                y = _rotate(xb * _segment_inv_rms(xb, seg_ref), tabs[3 * t:3 * t + 3], DIFF_DH // 4)
            else:
                y = xb * _halves_inv_rms(xb) * gain_ref[t:t + 1, :]
            dst[:, c * LANES:(c + 1) * LANES] = y.astype(BF16)

    q, k = project(0), project(1)
    finish(0, q, q_ref)
    v = project(2)
    finish(1, k, k_ref)
    v_ref[...] = v.astype(BF16)


def _qkv_pre(st, xs, mod, g, w_qkv, qk_gain, rope):
    tab_spec = pl.BlockSpec((TOKEN_TILE, LANES), lambda b, j: (j, 0))
    in_specs = [st.tile_spec(D_MODEL), st.mod_spec(), _resident((1, D_MODEL)), _resident((D_MODEL, 3 * D_MODEL))]
    args = [xs, mod, g, w_qkv]
    lane_gain = jnp.tile(qk_gain, (1, 2))
    post = (DIFF_DH ** -0.5 * LOG2E, 1.0)
    if rope:
        in_specs += [_resident((2 * LANES, LANES))] + [tab_spec] * 6
        args += [_segment_mean_matrix((DIFF_DH, DIFF_DH))]
        for t in range(2):
            args += list(_rope_tables(st, DIFF_DH, 0, DIFF_DH, lane_gain[t], post[t]))
    else:
        in_specs += [_resident((2, LANES))]
        args += [lane_gain * jnp.array(post, F32)[:, None]]
    shape = jax.ShapeDtypeStruct((st.batch, st.tokens, D_MODEL), BF16)
    return pl.pallas_call(
        functools.partial(_qkv_pre_kernel, rope=rope),
        grid=(st.batch, st.tiles),
        in_specs=in_specs,
        out_specs=[st.tile_spec(D_MODEL)] * 3,
        out_shape=[shape] * 3,
        compiler_params=_params(2),
        name="qkv_pre",
    )(*args)


DIFF_HEADS_PER_STEP = 8


def _diff_attn_kernel(q_ref, k_ref, v_ref, lam_ref, subg_ref, o_ref, *, seq, lam_init):
    lp = lam_ref[...]
    lam = (jnp.exp(jnp.sum(lp[0:1] * lp[1:2], axis=-1, keepdims=True))
           - jnp.exp(jnp.sum(lp[2:3] * lp[3:4], axis=-1, keepdims=True)) + lam_init)

    def attend(query_rows, key_rows):
        def logits(hd):
            cols = slice(hd * LANES, (hd + 1) * LANES)
            q, k = q_ref[query_rows, cols], k_ref[key_rows, cols]
            return [_dot_t(jnp.where(_lane_mask(c * DIFF_DH, (c + 1) * DIFF_DH), q, jnp.zeros_like(q)), k)
                    for c in range(2)]

        s_next = logits(0)
        for hd in range(DIFF_HEADS_PER_STEP):
            cols = slice(hd * LANES, (hd + 1) * LANES)
            s_cur = s_next
            if hd + 1 < DIFF_HEADS_PER_STEP:
                s_next = logits(hd + 1)
            probs = []
            for s in s_cur:
                p = jnp.exp2(s - jnp.max(s, axis=-1, keepdims=True))
                probs.append((p, 1.0 / jnp.sum(p, axis=-1, keepdims=True)))
            (p0, r0), (p1, r1) = probs
            a = (p0 - p1 * (lam * r1 / r0)).astype(BF16)
            o = _dot(a, v_ref[key_rows, cols]) * r0
            ms = jnp.mean(o * o, axis=-1, keepdims=True)
            o_ref[query_rows, cols] = ((o * lax.rsqrt(ms + EPS)) * subg_ref[...] * (1.0 - lam_init)).astype(BF16)

    _wide_tile_branches(2, seq, attend)


def _diff_attn(st, q, k, v, lam_p, sub_g, lam_init):
    width = DIFF_HEADS_PER_STEP * LANES
    head = lambda b, h, j: (b, j, h)
    keys = lambda b, h, j: (b, 0, h)
    kern = functools.partial(_diff_attn_kernel, seq=st.seq, lam_init=lam_init)
    return pl.pallas_call(
        kern,
        grid=(st.batch, DIFF_HEADS // DIFF_HEADS_PER_STEP, st.attn_steps),
        in_specs=[pl.BlockSpec((None, ATTN_TILE, width), head),
                  pl.BlockSpec((None, st.tokens, width), keys),
                  pl.BlockSpec((None, st.tokens, width), keys),
                  _resident((4, DIFF_DH)), _resident((1, LANES))],
        out_specs=pl.BlockSpec((None, ATTN_TILE, width), head),
        out_shape=jax.ShapeDtypeStruct((st.batch, st.tokens, D_MODEL), BF16),
        compiler_params=_params(3),
        name="diff_attn",
    )(q, k, v, lam_p, sub_g.reshape(1, LANES))


MLA_DOWN = MLA_Q_RANK + MLA_KV_RANK + LANES


def _mla_pre_kernel(x_ref, mod_ref, g_ref, wd_ref, qg_ref, kvg_ref, wuq_ref, wukv_ref, seg_ref, kgain_ref,
                    qc_ref, qu_ref, qd_ref, kc_ref, ku_ref, kd_ref, q_ref, k_ref, v_ref):
    h = _modulate(x_ref[...], g_ref[...], mod_ref[3:4, :], mod_ref[4:5, :]).astype(BF16)
    down = _dot(h, wd_ref[...])

    def full_norm(z, gain):
        ms = jnp.mean(z * z, axis=-1, keepdims=True)
        return ((z * lax.rsqrt(ms + EPS)) * gain).astype(BF16)

    q = _dot(full_norm(down[:, :MLA_Q_RANK], qg_ref[...]), wuq_ref[...])
    kv = _dot(full_norm(down[:, MLA_Q_RANK:MLA_Q_RANK + MLA_KV_RANK], kvg_ref[...]), wukv_ref[...])
    kpe = down[:, MLA_Q_RANK + MLA_KV_RANK:]
    kpe = _rotate(kpe * _segment_inv_rms(kpe, seg_ref), (kc_ref, ku_ref, kd_ref), MLA_ROPE // 4)
    for hd in range(MLA_HEADS):
        cols = slice(hd * LANES, (hd + 1) * LANES)
        qb = q[:, cols]
        q_ref[:, cols] = _rotate(qb * _segment_inv_rms(qb, seg_ref), (qc_ref, qu_ref, qd_ref),
                                 MLA_ROPE // 4).astype(BF16)
        kb = kv[:, cols]
        k_ref[:, cols] = (kb * _segment_inv_rms(kb, seg_ref) * kgain_ref[...] + kpe).astype(BF16)
    v_ref[...] = kv[:, MLA_HEADS * LANES:].astype(BF16)


def _mla_pre(st, xs, mod, g, w_down, q_g, kv_g, w_uq, w_ukv, qk_gain):
    tab_spec = pl.BlockSpec((TOKEN_TILE, LANES), lambda b, j: (j, 0))
    slots = MLA_HEADS * LANES
    scale = (MLA_NOPE + MLA_ROPE) ** -0.5 * LOG2E
    q_tabs = _rope_tables(st, MLA_ROPE, MLA_NOPE, LANES, qk_gain[0], scale)
    k_tabs = _rope_tables(st, MLA_ROPE, MLA_NOPE, LANES, qk_gain[1], 1.0)
    seg = _segment_mean_matrix((MLA_NOPE, MLA_ROPE, LANES - MLA_NOPE - MLA_ROPE))
    return pl.pallas_call(
        _mla_pre_kernel,
        grid=(st.batch, st.tiles),
        in_specs=[st.tile_spec(D_MODEL), st.mod_spec(), _resident((1, D_MODEL)),
                  _resident((D_MODEL, MLA_DOWN)), _resident((1, MLA_Q_RANK)), _resident((1, MLA_KV_RANK)),
                  _resident((MLA_Q_RANK, slots)), _resident((MLA_KV_RANK, slots + D_MODEL)),
                  _resident((2 * LANES, LANES)), _resident((1, LANES))] + [tab_spec] * 6,
        out_specs=[st.tile_spec(slots), st.tile_spec(slots), st.tile_spec(D_MODEL)],
        out_shape=[jax.ShapeDtypeStruct((st.batch, st.tokens, slots), BF16),
                   jax.ShapeDtypeStruct((st.batch, st.tokens, slots), BF16),
                   jax.ShapeDtypeStruct((st.batch, st.tokens, D_MODEL), BF16)],
        compiler_params=_params(2),
        name="mla_pre",
    )(xs, mod, g, w_down, q_g, kv_g, w_uq, w_ukv, seg, qk_gain[1:2], *q_tabs, *k_tabs)


MLA_PAIRS_PER_STEP = 4


def _mla_attn_kernel(q_ref, k_ref, v_ref, o_ref, *, seq):
    def attend(query_rows, key_rows):
        def logits(head):
            slot = slice(head * LANES, (head + 1) * LANES)
            return _dot_t(q_ref[query_rows, slot], k_ref[key_rows, slot])

        s_next = logits(0)
        for pair in range(MLA_PAIRS_PER_STEP):
            v = v_ref[key_rows, pair * LANES:(pair + 1) * LANES]
            out = None
            for hh in range(2):
                head = 2 * pair + hh
                s = s_next
                if head + 1 < 2 * MLA_PAIRS_PER_STEP:
                    s_next = logits(head + 1)
                p = jnp.exp2(s - jnp.max(s, axis=-1, keepdims=True))
                vh, keep, ones_lane = _half_values_with_ones(v, hh == 1)
                o = _dot(p.astype(BF16), vh)
                o = o * (1.0 / o[:, ones_lane:ones_lane + 1])
                out = o if out is None else jnp.where(keep, o, out)
            o_ref[query_rows, pair * LANES:(pair + 1) * LANES] = out.astype(BF16)

    _wide_tile_branches(2, seq, attend)


def _mla_attn(st, q, k, v):
    slots = 2 * MLA_PAIRS_PER_STEP * LANES
    vals = MLA_PAIRS_PER_STEP * LANES
    kern = functools.partial(_mla_attn_kernel, seq=st.seq)
    return pl.pallas_call(
        kern,
        grid=(st.batch, MLA_HEADS // (2 * MLA_PAIRS_PER_STEP), st.attn_steps),
        in_specs=[pl.BlockSpec((None, ATTN_TILE, slots), lambda b, h, j: (b, j, h)),
                  pl.BlockSpec((None, st.tokens, slots), lambda b, h, j: (b, 0, h)),
                  pl.BlockSpec((None, st.tokens, vals), lambda b, h, j: (b, 0, h))],
        out_specs=pl.BlockSpec((None, ATTN_TILE, vals), lambda b, h, j: (b, j, h)),
        out_shape=jax.ShapeDtypeStruct((st.batch, st.tokens, D_MODEL), BF16),
        compiler_params=_params(3),
        name="mla_attn",
    )(q, k, v)


NA_QROWS = TOKEN_TILE // GRID_W
NA_WROWS = NA_QROWS + NA_KH
NA_PAIRS = NA_WROWS // 2
NA_BIAS_LO = -(NA_QROWS - 1) - (NA_WROWS - NA_QROWS - NA_KH // 2) + NA_KH - 1
NA_BIAS_N = (NA_WROWS - 2) + NA_KH - 1 - NA_BIAS_LO + 1


def _na_attn_kernel(q_ref, k_ref, v_ref, bias_ref, o_ref, cap_ref, *, rows, seq):
    r0 = pl.program_id(1) * NA_QROWS
    w0 = jnp.clip(r0 - NA_KH // 2, 0, rows - NA_WROWS)
    start = pl.multiple_of(w0 * GRID_W, GRID_W)
    n_loc = NA_WROWS * GRID_W
    qidx = lax.broadcasted_iota(jnp.int32, (TOKEN_TILE, n_loc), 0)
    kidx = lax.broadcasted_iota(jnp.int32, (TOKEN_TILE, n_loc), 1)
    q_row, q_col = r0 + qidx // GRID_W, qidx % GRID_W
    k_row, k_col = w0 + kidx // GRID_W, kidx % GRID_W
    row_start = jnp.clip(q_row - NA_KH // 2, 0, rows - NA_KH)
    col_start = jnp.clip(q_col - NA_KW // 2, 0, GRID_W - NA_KW)
    valid = ((k_row >= row_start) & (k_row < row_start + NA_KH)
             & (k_col >= col_start) & (k_col < col_start + NA_KW))
    cap_ref[...] = jnp.where(valid, jnp.inf, NEG_INF)
    base = w0 - r0 + NA_KH - 1 - NA_BIAS_LO

    def logits(head):
        pair, hh = divmod(head, 2)
        cols = slice(pair * LANES, (pair + 1) * LANES)
        q = q_ref[:, cols]
        qh = jnp.where(_lane_mask(hh * NA_DH, (hh + 1) * NA_DH), q, jnp.zeros_like(q))
        return _dot_t(qh, k_ref[pl.ds(start, n_loc), cols]), _dot_t(qh, k_ref[seq:, cols])

    s_next = logits(0)
    for pair in range(NA_HEADS // 2):
        cols = slice(pair * LANES, (pair + 1) * LANES)
        v_loc, v_ctx = v_ref[pl.ds(start, n_loc), cols], v_ref[seq:, cols]
        out = None
        for hh in range(2):
            head = 2 * pair + hh
            s_loc, s_ctx = s_next
            if head + 1 < NA_HEADS:
                s_next = logits(head + 1)
            bias = jnp.concatenate(
                [jnp.concatenate([bias_ref[head, base + 2 * p - a] for p in range(NA_PAIRS)], axis=1)
                 for a in range(NA_QROWS)], axis=0)
            s_loc = jnp.minimum(s_loc + bias, cap_ref[...])
            m = jnp.maximum(jnp.max(s_loc, axis=-1, keepdims=True), jnp.max(s_ctx, axis=-1, keepdims=True))
            p_loc, p_ctx = jnp.exp2(s_loc - m), jnp.exp2(s_ctx - m)
            vh_loc, keep, ones_lane = _half_values_with_ones(v_loc, hh == 1)
            vh_ctx, _, _ = _half_values_with_ones(v_ctx, hh == 1)
            o = _dot(p_loc.astype(BF16), vh_loc) + _dot(p_ctx.astype(BF16), vh_ctx)
            o = o * (1.0 / o[:, ones_lane:ones_lane + 1])
            out = o if out is None else jnp.where(keep, o, out)
        o_ref[:, cols] = out.astype(BF16)


def _na_bias_table(rpb):
    n_rel, n_col = 2 * NA_KH - 1, 2 * NA_KW - 1
    left = GRID_W - NA_KW
    v = jnp.pad(rpb * LOG2E, ((0, 0), (0, 0), (left, 2 * GRID_W - left - n_col)))
    v = jnp.tile(v, (1, 1, GRID_W))[:, :, :GRID_W * (2 * GRID_W - 1)]
    toep = v.reshape(NA_HEADS, n_rel, GRID_W, 2 * GRID_W - 1)[..., GRID_W - 1:]
    toep = jnp.pad(toep, ((0, 0), (-NA_BIAS_LO, NA_BIAS_LO + NA_BIAS_N + 1 - n_rel), (0, 0), (0, 0)))
    return jnp.concatenate([toep[:, :-1], toep[:, 1:]], axis=-1)


def _na_attn(st, q, k, v, bias_tab):
    rows = st.seq // GRID_W
    assert rows >= NA_WROWS and rows % NA_QROWS == 0
    kern = functools.partial(_na_attn_kernel, rows=rows, seq=st.seq)
    return pl.pallas_call(
        kern,
        grid=(st.batch, st.lat_tiles),
        in_specs=[st.tile_spec(D_MODEL),
                  pl.BlockSpec((None, st.tokens, D_MODEL), lambda b, j: (b, 0, 0)),
                  pl.BlockSpec((None, st.tokens, D_MODEL), lambda b, j: (b, 0, 0)),
                  _resident((NA_HEADS, NA_BIAS_N, GRID_W, 2 * GRID_W))],
        out_specs=st.tile_spec(D_MODEL),
        out_shape=jax.ShapeDtypeStruct((st.batch, st.seq, D_MODEL), BF16),
        scratch_shapes=[pltpu.VMEM((TOKEN_TILE, NA_WROWS * GRID_W), F32)],
        compiler_params=_params(2),
        name="na_attn",
    )(q, k, v, bias_tab)


def _slot_columns(w, width):
    kdim = w.shape[0]
    w = w.reshape(kdim, -1, width)
    return jnp.pad(w, ((0, 0), (0, 0), (0, LANES - width))).reshape(kdim, -1)


def _slot_gain(g):
    return jnp.pad(g, ((0, 0), (0, LANES - g.shape[1])))


def kernel(x, c, ctx, c_ctx, ada_w, ada_b, norm_g, ffn_w_in, ffn_w_out, pool_w, pool_b, pool_scale,
           diff_w_qkv, diff_qk_g, diff_lambda, diff_sub_g, diff_w_o,
           mla_w_dq, mla_q_g, mla_w_uq, mla_w_dkv, mla_kv_g, mla_w_ukv, mla_qk_g, mla_w_o,
           na_w_qkv, na_qk_g, na_rpb, na_w_o):
    assert DEPTH == N_MIXERS, "one layer per mixer: the context stream ends after the last softmax mixer's keys"
    batch, seq, _ = x.shape
    st = _Stream(batch, seq, ctx.shape[1])
    rows =-(-(batch + 1) // 16) * 16
    cc = jnp.zeros((rows, D_MODEL), F32).at[:batch].set(c).at[batch].set(c_ctx)
    mods = _adaln(cc, ada_w, ada_b)

    w_in = ffn_w_in.astype(BF16)
    w_out = ffn_w_out.astype(BF16)
    gains = norm_g.reshape(DEPTH, 3, 1, D_MODEL)

    for i in range(DEPTH):
        mod = mods[i]
        if i == 0:
            xs = _ffn_join(st, x, ctx, mod, gains[i, 0], w_in, w_out)
            xs = _pool(st, xs, mod, gains[i, 1], pool_w[0].astype(BF16), pool_b[0].reshape(1, D_MODEL),
                       pool_scale[0].reshape(1, D_MODEL))
            xs = _ffn(st, xs, mod, 2, gains[i, 2], w_in, w_out, i)
            continue
        xs = _ffn(st, xs, mod, 0, gains[i, 0], w_in, w_out, i)
        if i == 1:
            q, k, v = _qkv_pre(st, xs, mod, gains[i, 1], diff_w_qkv[0].astype(BF16), diff_qk_g[0], True)
            lam_init = 0.8 - 0.6 * math.exp(-0.3 * i)
            attn = _diff_attn(st, q, k, v, diff_lambda[0], diff_sub_g[0], lam_init)
            w_o = diff_w_o[0]
        elif i == 2:
            kpe_cols = jnp.pad(mla_w_dkv[0][:, MLA_KV_RANK:], ((0, 0), (MLA_NOPE, LANES - MLA_NOPE - MLA_ROPE)))
            w_down = jnp.concatenate([mla_w_dq[0], mla_w_dkv[0][:, :MLA_KV_RANK], kpe_cols], axis=1)
            ukv = mla_w_ukv[0].reshape(MLA_KV_RANK, MLA_HEADS, MLA_NOPE + MLA_V)
            w_ukv = jnp.concatenate([_slot_columns(ukv[:, :, :MLA_NOPE].reshape(MLA_KV_RANK, -1), MLA_NOPE),
                                     ukv[:, :, MLA_NOPE:].reshape(MLA_KV_RANK, -1)], axis=1)
            q, k, v = _mla_pre(st, xs, mod, gains[i, 1], w_down.astype(BF16),
                               mla_q_g[0].reshape(1, -1), mla_kv_g[0].reshape(1, -1),
                               _slot_columns(mla_w_uq[0], MLA_NOPE + MLA_ROPE).astype(BF16),
                               w_ukv.astype(BF16), _slot_gain(mla_qk_g[0]))
            attn = _mla_attn(st, q, k, v)
            w_o = mla_w_o[0]
        else:
            q, k, v = _qkv_pre(st, xs, mod, gains[i, 1], na_w_qkv[0].astype(BF16), na_qk_g[0], False)
            attn = _na_attn(st, q, k, v, _na_bias_table(na_rpb[0]))
            w_o = na_w_o[0]
        xs = _mixout_ffn(st, xs, attn, mod, gains[i, 2], w_o.astype(BF16), w_in, w_out, i,
                         latent_only=(i == DEPTH - 1))
    return xs
```

```python
import functools
import math

import jax
import jax.numpy as jnp
from jax import lax
from jax.experimental import pallas as pl
from jax.experimental.pallas import tpu as pltpu

D_MODEL = 1024
DEPTH = 4
GRID_W = 64
N_MIXERS = 4
N_MOD = 9
EPS = 1e-6
LOG2E = math.log2(math.e)
ROPE_BASE = 10000.0
NEG_INF = -1e30
D_FF = 2816
POOL_WINDOWS = (2, 4, 8, 16)
POOL_G = D_MODEL // len(POOL_WINDOWS)
DIFF_HEADS = 8
DIFF_DH = 64
MLA_HEADS = 16
MLA_Q_RANK = 384
MLA_KV_RANK = 256
MLA_NOPE = 64
MLA_ROPE = 32
MLA_V = 64
NA_HEADS = 16
NA_DH = 64
NA_KH = 8
NA_KW = 16

LANES = 128
HALO = 8
TOKEN_TILE = 256
ATTN_TILE = 512
FFN_TILE = 1024
VMEM_LIMIT = 56 * 2**20

F32 = jnp.float32
BF16 = jnp.bfloat16


def _params(n_axes):
    return pltpu.CompilerParams(dimension_semantics=("arbitrary",) * n_axes,
                                vmem_limit_bytes=VMEM_LIMIT)


def _resident(shape, stack_index=()):
    index = tuple(stack_index) + (0,) * len(shape)
    return pl.BlockSpec((None,) * len(stack_index) + tuple(shape), lambda *_: index, pipeline_mode=pl.Buffered(1))


def _dot(a, b):
    return jnp.dot(a, b, preferred_element_type=F32)


def _dot_t(a, b):
    return lax.dot_general(a, b, (((1,), (1,)), ((), ())), preferred_element_type=F32)


def _modulate(x, g, shift, scale):
    ms = jnp.mean(x * x, axis=-1, keepdims=True)
    return (x * lax.rsqrt(ms + EPS)) * (g * (1.0 + scale)) + shift


def _silu(x):
    return x * (1.0 / (1.0 + jnp.exp(-x)))


def _adaln_kernel(c_ref, w_ref, b_ref, o_ref):
    s = _silu(c_ref[...]).astype(BF16)
    o_ref[...] = _dot(s, w_ref[...].astype(BF16)) + b_ref[...]


def _adaln(cc, ada_w, ada_b):
    rows = cc.shape[0]
    n_col = N_MOD * D_MODEL // D_MODEL
    out = pl.pallas_call(
        _adaln_kernel,
        grid=(DEPTH, n_col),
        in_specs=[pl.BlockSpec((rows, D_MODEL), lambda i, n: (0, 0)),
                  pl.BlockSpec((None, D_MODEL, D_MODEL), lambda i, n: (i, 0, n)),
                  pl.BlockSpec((None, 1, D_MODEL), lambda i, n: (i, 0, n))],
        out_specs=pl.BlockSpec((None, rows, D_MODEL), lambda i, n: (i, 0, n)),
        out_shape=jax.ShapeDtypeStruct((DEPTH, rows, N_MOD * D_MODEL), F32),
        compiler_params=_params(2),
        name="adaln",
    )(cc, ada_w, ada_b.reshape(DEPTH, 1, N_MOD * D_MODEL))
    return out.reshape(DEPTH, rows, N_MOD, D_MODEL)


def _ffn_core(x, mod_ref, k, g_ref, win_ref, wout_ref):
    shift = mod_ref[3 * k:3 * k + 1, :]
    scale = mod_ref[3 * k + 1:3 * k + 2, :]
    gate = mod_ref[3 * k + 2:3 * k + 3, :]
    parts = [x[r:r + TOKEN_TILE] for r in range(0, x.shape[0], TOKEN_TILE)]
    ups = [_dot(_modulate(p, g_ref[...], shift, scale).astype(BF16), win_ref[...]) for p in parts]
    downs = [_dot((_silu(u[:, :D_FF]) * u[:, D_FF:]).astype(BF16), wout_ref[...]) for u in ups]
    outs = [p + (0.5 * gate) * y for p, y in zip(parts, downs)]
    return outs[0] if len(outs) == 1 else jnp.concatenate(outs, axis=0)


def _wide_tile_branches(axis, seq, body, with_ctx=True, tile=ATTN_TILE):
    j = pl.program_id(axis)
    lat_steps = seq // tile

    @pl.when(j < lat_steps)
    def _():
        body(slice(None), slice(None))

    if with_ctx:
        @pl.when(j >= lat_steps)
        def _():
            body(slice(0, TOKEN_TILE), slice(seq, None))


def _ffn_kernel(x_ref, mod_ref, g_ref, win_ref, wout_ref, o_ref, *, k, seq):
    def step(rows, _):
        o_ref[rows, :] = _ffn_core(x_ref[rows, :], mod_ref, k, g_ref, win_ref, wout_ref)

    _wide_tile_branches(1, seq, step, tile=FFN_TILE)


def _ffn_join_kernel(x_ref, c_ref, mod_ref, g_ref, win_ref, wout_ref, o_ref, *, seq):
    j = pl.program_id(1)

    @pl.when(j < seq // FFN_TILE)
    def _():
        o_ref[...] = _ffn_core(x_ref[...], mod_ref, 0, g_ref, win_ref, wout_ref)

    @pl.when(j >= seq // FFN_TILE)
    def _():
        o_ref[0:TOKEN_TILE, :] = _ffn_core(c_ref[...], mod_ref, 0, g_ref, win_ref, wout_ref)


def _mixout_ffn_kernel(x_ref, a_ref, mod_ref, g_ref, wo_ref, win_ref, wout_ref, o_ref, *, seq, with_ctx):
    def step(rows, _):
        x = x_ref[rows, :] + mod_ref[5:6, :] * _dot(a_ref[rows, :], wo_ref[...])
        o_ref[rows, :] = _ffn_core(x, mod_ref, 2, g_ref, win_ref, wout_ref)

    _wide_tile_branches(1, seq, step, with_ctx, FFN_TILE)


class _Stream:
    def __init__(self, batch, seq, ctx):
        assert seq % TOKEN_TILE == 0 and ctx == TOKEN_TILE and seq % GRID_W == 0
        self.batch, self.seq, self.ctx = batch, seq, ctx
        self.tokens = seq + ctx
        self.lat_tiles = seq // TOKEN_TILE
        self.tiles = self.tokens // TOKEN_TILE
        assert seq % ATTN_TILE == 0 and ATTN_TILE >= ctx
        self.attn_steps = seq // ATTN_TILE + 1
        assert seq % FFN_TILE == 0
        self.ffn_steps = seq // FFN_TILE + 1

    def tile_spec(self, width):
        return pl.BlockSpec((None, TOKEN_TILE, width), lambda b, j: (b, j, 0))

    def ffn_spec(self, width):
        return pl.BlockSpec((None, FFN_TILE, width), lambda b, j: (b, j, 0))

    def mod_spec(self, rows=TOKEN_TILE):
        lat, ctx_row = self.seq // rows, self.batch
        return pl.BlockSpec((None, N_MOD, D_MODEL),
                            lambda b, j: (jnp.where(j < lat, b, ctx_row), 0, 0))


def _ffn_weight_specs(layer, half):
    return [_resident((D_MODEL, 2 * D_FF), (layer, half)), _resident((D_FF, D_MODEL), (layer, half))]


def _ffn(st, xs, mod, k, g, w_in, w_out, layer):
    return pl.pallas_call(
        functools.partial(_ffn_kernel, k=k, seq=st.seq),
        grid=(st.batch, st.ffn_steps),
        in_specs=[st.ffn_spec(D_MODEL), st.mod_spec(FFN_TILE), _resident((1, D_MODEL))]
        + _ffn_weight_specs(layer, k // 2),
        out_specs=st.ffn_spec(D_MODEL),
        out_shape=jax.ShapeDtypeStruct((st.batch, st.tokens, D_MODEL), F32),
        compiler_params=_params(2),
        name="ffn",
    )(xs, mod, g, w_in, w_out)


def _ffn_join(st, x, ctx, mod, g, w_in, w_out):
    last_lat = st.ffn_steps - 2
    return pl.pallas_call(
        functools.partial(_ffn_join_kernel, seq=st.seq),
        grid=(st.batch, st.ffn_steps),
        in_specs=[pl.BlockSpec((None, FFN_TILE, D_MODEL), lambda b, j: (b, jnp.minimum(j, last_lat), 0)),
                  pl.BlockSpec((None, st.ctx, D_MODEL), lambda b, j: (b, 0, 0)),
                  st.mod_spec(FFN_TILE), _resident((1, D_MODEL))] + _ffn_weight_specs(0, 0),
        out_specs=st.ffn_spec(D_MODEL),
        out_shape=jax.ShapeDtypeStruct((st.batch, st.tokens, D_MODEL), F32),
        compiler_params=_params(2),
        name="ffn_join",
    )(x, ctx, mod, g, w_in, w_out)


def _mixout_ffn(st, xs, attn, mod, g, w_o, w_in, w_out, layer, latent_only):
    steps = st.ffn_steps - 1 if latent_only else st.ffn_steps
    tokens = st.seq if latent_only else st.tokens
    return pl.pallas_call(
        functools.partial(_mixout_ffn_kernel, seq=st.seq, with_ctx=not latent_only),
        grid=(st.batch, steps),
        in_specs=[st.ffn_spec(D_MODEL), st.ffn_spec(D_MODEL), st.mod_spec(FFN_TILE),
                  _resident((1, D_MODEL)), _resident((D_MODEL, D_MODEL))] + _ffn_weight_specs(layer, 1),
        out_specs=st.ffn_spec(D_MODEL),
        out_shape=jax.ShapeDtypeStruct((st.batch, tokens, D_MODEL), F32),
        compiler_params=_params(2),
        name="mixout_ffn",
    )(xs, attn, mod, g, w_o, w_in, w_out)


def _window_sum(e, start, win, rows):
    steps = [win >> (i + 1) for i in range(win.bit_length() - 1)]
    cur, base = e, start
    for idx, shift in enumerate(steps):
        n = rows + HALO * (len(steps) - 1 - idx)
        cur = cur[base:base + n] + cur[base + shift:base + shift + n]
        base = 0
    return cur


def _pool_kernel(x_ref, xp_ref, xn_ref, mod_ref, g_ref, w_ref, b_ref, s_ref, o_ref, hext_ref,
                 *, lat_tiles, tiles, seq, ctx):
    j = pl.program_id(1)
    g, shift, scale = g_ref[...], mod_ref[3:4, :], mod_ref[4:5, :]
    x = x_ref[...]
    h = _modulate(x, g, shift, scale)
    has_prev = jnp.logical_and(j != 0, j != lat_tiles)
    has_next = jnp.logical_and(j != lat_tiles - 1, j != tiles - 1)
    hext_ref[0:HALO, :] = jnp.where(has_prev, _modulate(xp_ref[...], g, shift, scale), 0.0)
    hext_ref[HALO:HALO + TOKEN_TILE, :] = h
    hext_ref[HALO + TOKEN_TILE:2 * HALO + TOKEN_TILE, :] = jnp.where(
        has_next, _modulate(xn_ref[...], g, shift, scale), 0.0)
    hext_ref[2 * HALO + TOKEN_TILE:, :] = jnp.zeros((2 * HALO, D_MODEL), F32)
    in_lat = j < lat_tiles
    t = (j - jnp.where(in_lat, 0, lat_tiles)) * TOKEN_TILE + lax.broadcasted_iota(jnp.int32, (TOKEN_TILE, 1), 0)
    n = jnp.where(in_lat, seq, ctx)
    for gi, win in enumerate(POOL_WINDOWS):
        cols = slice(gi * POOL_G, (gi + 1) * POOL_G)
        acc = _window_sum(hext_ref[:, cols], HALO - win // 2, win, TOKEN_TILE)
        lo = jnp.maximum(t - win // 2, 0)
        hi = jnp.minimum(t - win // 2 + win, n)
        mean = acc / (hi - lo).astype(F32)
        d = (mean - h[:, cols]).astype(BF16)
        y = (_dot(d, w_ref[gi]) + b_ref[:, cols]) * s_ref[:, cols]
        o_ref[:, cols] = x[:, cols] + mod_ref[5:6, cols] * y


def _pool(st, xs, mod, g, w, b, scale):
    per_tile = TOKEN_TILE // HALO
    last = st.tokens // HALO - 1
    kern = functools.partial(_pool_kernel, lat_tiles=st.lat_tiles, tiles=st.tiles, seq=st.seq, ctx=st.ctx)
    return pl.pallas_call(
        kern,
        grid=(st.batch, st.tiles),
        in_specs=[st.tile_spec(D_MODEL),
                  pl.BlockSpec((None, HALO, D_MODEL), lambda b_, j: (b_, jnp.maximum(j * per_tile - 1, 0), 0)),
                  pl.BlockSpec((None, HALO, D_MODEL), lambda b_, j: (b_, jnp.minimum((j + 1) * per_tile, last), 0)),
                  st.mod_spec(), _resident((1, D_MODEL)),
                  _resident((len(POOL_WINDOWS), POOL_G, POOL_G)),
                  _resident((1, D_MODEL)), _resident((1, D_MODEL))],
        out_specs=st.tile_spec(D_MODEL),
        out_shape=jax.ShapeDtypeStruct((st.batch, st.tokens, D_MODEL), F32),
        scratch_shapes=[pltpu.VMEM((TOKEN_TILE + 4 * HALO, D_MODEL), F32)],
        compiler_params=_params(2),
        name="pool",
    )(xs, xs, xs, mod, g, w, b, scale)


def _lane_mask(lo, hi):
    lane = lax.broadcasted_iota(jnp.int32, (1, LANES), 1)
    return jnp.logical_and(lane >= lo, lane < hi)


def _segment_mean_matrix(lengths):
    assert sum(lengths) == LANES and all(n & (n - 1) == 0 for n in lengths)
    ids = jnp.repeat(jnp.arange(len(lengths)), jnp.array(lengths), total_repeat_length=LANES)
    inv = 1.0 / jnp.array(lengths, F32)[ids]
    m = jnp.where(ids[:, None] == ids[None, :], inv[None, :], 0.0)
    return jnp.tile(m, (2, 1)).astype(BF16)


def _segment_inv_rms(x, seg_ref):
    sq = x * x
    hi = sq.astype(BF16)
    lo = (sq - hi.astype(F32)).astype(BF16)
    return lax.rsqrt(_dot(jnp.concatenate([hi, lo], axis=1), seg_ref[...]) + EPS)


def _halves_inv_rms(x):
    sq = x * x
    lower = _lane_mask(0, LANES // 2)
    low = jnp.sum(jnp.where(lower, sq, 0.0), axis=-1, keepdims=True)
    high = jnp.sum(jnp.where(lower, 0.0, sq), axis=-1, keepdims=True)
    inv_n = 2.0 / LANES
    return jnp.where(lower, lax.rsqrt(low * inv_n + EPS), lax.rsqrt(high * inv_n + EPS))


def _half_values_with_ones(v, upper):
    lane = lax.broadcasted_iota(jnp.int32, (1, LANES), 1)
    keep = (lane >= LANES // 2) if upper else (lane < LANES // 2)
    ones_lane = 0 if upper else LANES // 2
    return jnp.where(keep, v, (lane == ones_lane).astype(v.dtype)), keep, ones_lane


def _rotate(z, tabs, shift):
    c_ref, u_ref, d_ref = tabs
    return (z * c_ref[...] + pltpu.roll(z, LANES - shift, axis=1) * u_ref[...]
            + pltpu.roll(z, shift, axis=1) * d_ref[...])


def _rope_tables(st, dim, lane_lo, period, gain, post):
    nf = dim // 4
    inv = ROPE_BASE ** (-jnp.arange(nf, dtype=F32) / nf)
    t = jnp.arange(st.seq, dtype=jnp.int32)
    pos = jnp.stack([t // GRID_W, t % GRID_W], axis=-1).astype(F32)
    ang = jnp.broadcast_to(pos[:, :, None, None] * inv, (st.seq, 2, 2, nf)).reshape(st.seq, dim)
    cos, sin = jnp.cos(ang), jnp.sin(ang)
    first_half = (jnp.arange(dim) % (2 * nf)) < nf
    sin_up = jnp.where(first_half, -sin, 0.0)
    sin_dn = jnp.where(first_half, 0.0, sin)

    def place(tab, fill, lane_gain):
        blk = jnp.full((st.seq, period), fill, F32).at[:, lane_lo:lane_lo + dim].set(tab)
        blk = jnp.tile(blk, (1, LANES // period))
        return jnp.concatenate([blk, jnp.full((st.ctx, LANES), fill, F32)], axis=0) * (lane_gain * post)

    return (place(cos, 1.0, gain), place(sin_up, 0.0, jnp.roll(gain, -nf)), place(sin_dn, 0.0, jnp.roll(gain, nf)))


def _qkv_pre_kernel(*refs, rope):
    if rope:
        x_ref, mod_ref, g_ref, w_ref, seg_ref, *tabs, q_ref, k_ref, v_ref = refs
    else:
        x_ref, mod_ref, g_ref, w_ref, gain_ref, q_ref, k_ref, v_ref = refs
    h = _modulate(x_ref[...], g_ref[...], mod_ref[3:4, :], mod_ref[4:5, :]).astype(BF16)
    def project(t):
        return _dot(h, w_ref[:, t * D_MODEL:(t + 1) * D_MODEL])

    def finish(t, x, dst):
        for c in range(D_MODEL // LANES):
            xb = x[:, c * LANES:(c + 1) * LANES]
            if rope:
                y = _rotate(xb * _segment_inv_rms(xb, seg_ref), tabs[3 * t:3 * t + 3], DIFF_DH // 4)
            else:
                y = xb * _halves_inv_rms(xb) * gain_ref[t:t + 1, :]
            dst[:, c * LANES:(c + 1) * LANES] = y.astype(BF16)

    q, k = project(0), project(1)
    finish(0, q, q_ref)
    v = project(2)
    finish(1, k, k_ref)
    v_ref[...] = v.astype(BF16)


def _qkv_pre(st, xs, mod, g, w_qkv, qk_gain, rope):
    tab_spec = pl.BlockSpec((TOKEN_TILE, LANES), lambda b, j: (j, 0))
    in_specs = [st.tile_spec(D_MODEL), st.mod_spec(), _resident((1, D_MODEL)), _resident((D_MODEL, 3 * D_MODEL))]
    args = [xs, mod, g, w_qkv]
    lane_gain = jnp.tile(qk_gain, (1, 2))
    post = (DIFF_DH ** -0.5 * LOG2E, 1.0)
    if rope:
        in_specs += [_resident((2 * LANES, LANES))] + [tab_spec] * 6
        args += [_segment_mean_matrix((DIFF_DH, DIFF_DH))]
        for t in range(2):
            args += list(_rope_tables(st, DIFF_DH, 0, DIFF_DH, lane_gain[t], post[t]))
    else:
        in_specs += [_resident((2, LANES))]
        args += [lane_gain * jnp.array(post, F32)[:, None]]
    shape = jax.ShapeDtypeStruct((st.batch, st.tokens, D_MODEL), BF16)
    return pl.pallas_call(
        functools.partial(_qkv_pre_kernel, rope=rope),
        grid=(st.batch, st.tiles),
        in_specs=in_specs,
        out_specs=[st.tile_spec(D_MODEL)] * 3,
        out_shape=[shape] * 3,
        compiler_params=_params(2),
        name="qkv_pre",
    )(*args)


DIFF_HEADS_PER_STEP = 8


def _diff_attn_kernel(q_ref, k_ref, v_ref, lam_ref, subg_ref, o_ref, *, seq, lam_init):
    lp = lam_ref[...]
    lam = (jnp.exp(jnp.sum(lp[0:1] * lp[1:2], axis=-1, keepdims=True))
           - jnp.exp(jnp.sum(lp[2:3] * lp[3:4], axis=-1, keepdims=True)) + lam_init)

    def attend(query_rows, key_rows):
        def logits(hd):
            cols = slice(hd * LANES, (hd + 1) * LANES)
            q, k = q_ref[query_rows, cols], k_ref[key_rows, cols]
            return [_dot_t(jnp.where(_lane_mask(c * DIFF_DH, (c + 1) * DIFF_DH), q, jnp.zeros_like(q)), k)
                    for c in range(2)]

        s_next = logits(0)
        for hd in range(DIFF_HEADS_PER_STEP):
            cols = slice(hd * LANES, (hd + 1) * LANES)
            s_cur = s_next
            if hd + 1 < DIFF_HEADS_PER_STEP:
                s_next = logits(hd + 1)
            probs = []
            for s in s_cur:
                p = jnp.exp2(s - jnp.max(s, axis=-1, keepdims=True))
                probs.append((p, 1.0 / jnp.sum(p, axis=-1, keepdims=True)))
            (p0, r0), (p1, r1) = probs
            a = (p0 - p1 * (lam * r1 / r0)).astype(BF16)
            o = _dot(a, v_ref[key_rows, cols]) * r0
            ms = jnp.mean(o * o, axis=-1, keepdims=True)
            o_ref[query_rows, cols] = ((o * lax.rsqrt(ms + EPS)) * subg_ref[...] * (1.0 - lam_init)).astype(BF16)

    _wide_tile_branches(2, seq, attend)


def _diff_attn(st, q, k, v, lam_p, sub_g, lam_init):
    width = DIFF_HEADS_PER_STEP * LANES
    head = lambda b, h, j: (b, j, h)
    keys = lambda b, h, j: (b, 0, h)
    kern = functools.partial(_diff_attn_kernel, seq=st.seq, lam_init=lam_init)
    return pl.pallas_call(
        kern,
        grid=(st.batch, DIFF_HEADS // DIFF_HEADS_PER_STEP, st.attn_steps),
        in_specs=[pl.BlockSpec((None, ATTN_TILE, width), head),
                  pl.BlockSpec((None, st.tokens, width), keys),
                  pl.BlockSpec((None, st.tokens, width), keys),
                  _resident((4, DIFF_DH)), _resident((1, LANES))],
        out_specs=pl.BlockSpec((None, ATTN_TILE, width), head),
        out_shape=jax.ShapeDtypeStruct((st.batch, st.tokens, D_MODEL), BF16),
        compiler_params=_params(3),
        name="diff_attn",
    )(q, k, v, lam_p, sub_g.reshape(1, LANES))


MLA_DOWN = MLA_Q_RANK + MLA_KV_RANK + LANES


def _mla_pre_kernel(x_ref, mod_ref, g_ref, wd_ref, qg_ref, kvg_ref, wuq_ref, wukv_ref, seg_ref, kgain_ref,
                    qc_ref, qu_ref, qd_ref, kc_ref, ku_ref, kd_ref, q_ref, k_ref, v_ref):
    h = _modulate(x_ref[...], g_ref[...], mod_ref[3:4, :], mod_ref[4:5, :]).astype(BF16)
    down = _dot(h, wd_ref[...])

    def full_norm(z, gain):
        ms = jnp.mean(z * z, axis=-1, keepdims=True)
        return ((z * lax.rsqrt(ms + EPS)) * gain).astype(BF16)

    q = _dot(full_norm(down[:, :MLA_Q_RANK], qg_ref[...]), wuq_ref[...])
    kv = _dot(full_norm(down[:, MLA_Q_RANK:MLA_Q_RANK + MLA_KV_RANK], kvg_ref[...]), wukv_ref[...])
    kpe = down[:, MLA_Q_RANK + MLA_KV_RANK:]
    kpe = _rotate(kpe * _segment_inv_rms(kpe, seg_ref), (kc_ref, ku_ref, kd_ref), MLA_ROPE // 4)
    for hd in range(MLA_HEADS):
        cols = slice(hd * LANES, (hd + 1) * LANES)
        qb = q[:, cols]
        q_ref[:, cols] = _rotate(qb * _segment_inv_rms(qb, seg_ref), (qc_ref, qu_ref, qd_ref),
                                 MLA_ROPE // 4).astype(BF16)
        kb = kv[:, cols]
        k_ref[:, cols] = (kb * _segment_inv_rms(kb, seg_ref) * kgain_ref[...] + kpe).astype(BF16)
    v_ref[...] = kv[:, MLA_HEADS * LANES:].astype(BF16)


def _mla_pre(st, xs, mod, g, w_down, q_g, kv_g, w_uq, w_ukv, qk_gain):
    tab_spec = pl.BlockSpec((TOKEN_TILE, LANES), lambda b, j: (j, 0))
    slots = MLA_HEADS * LANES
    scale = (MLA_NOPE + MLA_ROPE) ** -0.5 * LOG2E
    q_tabs = _rope_tables(st, MLA_ROPE, MLA_NOPE, LANES, qk_gain[0], scale)
    k_tabs = _rope_tables(st, MLA_ROPE, MLA_NOPE, LANES, qk_gain[1], 1.0)
    seg = _segment_mean_matrix((MLA_NOPE, MLA_ROPE, LANES - MLA_NOPE - MLA_ROPE))
    return pl.pallas_call(
        _mla_pre_kernel,
        grid=(st.batch, st.tiles),
        in_specs=[st.tile_spec(D_MODEL), st.mod_spec(), _resident((1, D_MODEL)),
                  _resident((D_MODEL, MLA_DOWN)), _resident((1, MLA_Q_RANK)), _resident((1, MLA_KV_RANK)),
                  _resident((MLA_Q_RANK, slots)), _resident((MLA_KV_RANK, slots + D_MODEL)),
                  _resident((2 * LANES, LANES)), _resident((1, LANES))] + [tab_spec] * 6,
        out_specs=[st.tile_spec(slots), st.tile_spec(slots), st.tile_spec(D_MODEL)],
        out_shape=[jax.ShapeDtypeStruct((st.batch, st.tokens, slots), BF16),
                   jax.ShapeDtypeStruct((st.batch, st.tokens, slots), BF16),
                   jax.ShapeDtypeStruct((st.batch, st.tokens, D_MODEL), BF16)],
        compiler_params=_params(2),
        name="mla_pre",
    )(xs, mod, g, w_down, q_g, kv_g, w_uq, w_ukv, seg, qk_gain[1:2], *q_tabs, *k_tabs)


MLA_PAIRS_PER_STEP = 4


def _mla_attn_kernel(q_ref, k_ref, v_ref, o_ref, *, seq):
    def attend(query_rows, key_rows):
        def logits(head):
            slot = slice(head * LANES, (head + 1) * LANES)
            return _dot_t(q_ref[query_rows, slot], k_ref[key_rows, slot])

        s_next = logits(0)
        for pair in range(MLA_PAIRS_PER_STEP):
            v = v_ref[key_rows, pair * LANES:(pair + 1) * LANES]
            out = None
            for hh in range(2):
                head = 2 * pair + hh
                s = s_next
                if head + 1 < 2 * MLA_PAIRS_PER_STEP:
                    s_next = logits(head + 1)
                p = jnp.exp2(s - jnp.max(s, axis=-1, keepdims=True))
                vh, keep, ones_lane = _half_values_with_ones(v, hh == 1)
                o = _dot(p.astype(BF16), vh)
                o = o * (1.0 / o[:, ones_lane:ones_lane + 1])
                out = o if out is None else jnp.where(keep, o, out)
            o_ref[query_rows, pair * LANES:(pair + 1) * LANES] = out.astype(BF16)

    _wide_tile_branches(2, seq, attend)


def _mla_attn(st, q, k, v):
    slots = 2 * MLA_PAIRS_PER_STEP * LANES
    vals = MLA_PAIRS_PER_STEP * LANES
    kern = functools.partial(_mla_attn_kernel, seq=st.seq)
    return pl.pallas_call(
        kern,
        grid=(st.batch, MLA_HEADS // (2 * MLA_PAIRS_PER_STEP), st.attn_steps),
        in_specs=[pl.BlockSpec((None, ATTN_TILE, slots), lambda b, h, j: (b, j, h)),
                  pl.BlockSpec((None, st.tokens, slots), lambda b, h, j: (b, 0, h)),
                  pl.BlockSpec((None, st.tokens, vals), lambda b, h, j: (b, 0, h))],
        out_specs=pl.BlockSpec((None, ATTN_TILE, vals), lambda b, h, j: (b, j, h)),
        out_shape=jax.ShapeDtypeStruct((st.batch, st.tokens, D_MODEL), BF16),
        compiler_params=_params(3),
        name="mla_attn",
    )(q, k, v)


NA_QROWS = TOKEN_TILE // GRID_W
NA_WROWS = NA_QROWS + NA_KH
NA_PAIRS = NA_WROWS // 2
NA_BIAS_LO = -(NA_QROWS - 1) - (NA_WROWS - NA_QROWS - NA_KH // 2) + NA_KH - 1
NA_BIAS_N = (NA_WROWS - 2) + NA_KH - 1 - NA_BIAS_LO + 1


def _na_attn_kernel(q_ref, k_ref, v_ref, bias_ref, o_ref, cap_ref, *, rows, seq):
    r0 = pl.program_id(1) * NA_QROWS
    w0 = jnp.clip(r0 - NA_KH // 2, 0, rows - NA_WROWS)
    start = pl.multiple_of(w0 * GRID_W, GRID_W)
    n_loc = NA_WROWS * GRID_W
    qidx = lax.broadcasted_iota(jnp.int32, (TOKEN_TILE, n_loc), 0)
    kidx = lax.broadcasted_iota(jnp.int32, (TOKEN_TILE, n_loc), 1)
    q_row, q_col = r0 + qidx // GRID_W, qidx % GRID_W
    k_row, k_col = w0 + kidx // GRID_W, kidx % GRID_W
    row_start = jnp.clip(q_row - NA_KH // 2, 0, rows - NA_KH)
    col_start = jnp.clip(q_col - NA_KW // 2, 0, GRID_W - NA_KW)
    valid = ((k_row >= row_start) & (k_row < row_start + NA_KH)
             & (k_col >= col_start) & (k_col < col_start + NA_KW))
    cap_ref[...] = jnp.where(valid, jnp.inf, NEG_INF)
    base = w0 - r0 + NA_KH - 1 - NA_BIAS_LO

    def logits(head):
        pair, hh = divmod(head, 2)
        cols = slice(pair * LANES, (pair + 1) * LANES)
        q = q_ref[:, cols]
        qh = jnp.where(_lane_mask(hh * NA_DH, (hh + 1) * NA_DH), q, jnp.zeros_like(q))
        return _dot_t(qh, k_ref[pl.ds(start, n_loc), cols]), _dot_t(qh, k_ref[seq:, cols])

    s_next = logits(0)
    for pair in range(NA_HEADS // 2):
        cols = slice(pair * LANES, (pair + 1) * LANES)
        v_loc, v_ctx = v_ref[pl.ds(start, n_loc), cols], v_ref[seq:, cols]
        out = None
        for hh in range(2):
            head = 2 * pair + hh
            s_loc, s_ctx = s_next
            if head + 1 < NA_HEADS:
                s_next = logits(head + 1)
            bias = jnp.concatenate(
                [jnp.concatenate([bias_ref[head, base + 2 * p - a] for p in range(NA_PAIRS)], axis=1)
                 for a in range(NA_QROWS)], axis=0)
            s_loc = jnp.minimum(s_loc + bias, cap_ref[...])
            m = jnp.maximum(jnp.max(s_loc, axis=-1, keepdims=True), jnp.max(s_ctx, axis=-1, keepdims=True))
            p_loc, p_ctx = jnp.exp2(s_loc - m), jnp.exp2(s_ctx - m)
            vh_loc, keep, ones_lane = _half_values_with_ones(v_loc, hh == 1)
            vh_ctx, _, _ = _half_values_with_ones(v_ctx, hh == 1)
            o = _dot(p_loc.astype(BF16), vh_loc) + _dot(p_ctx.astype(BF16), vh_ctx)
            o = o * (1.0 / o[:, ones_lane:ones_lane + 1])
            out = o if out is None else jnp.where(keep, o, out)
        o_ref[:, cols] = out.astype(BF16)


def _na_bias_table(rpb):
    n_rel, n_col = 2 * NA_KH - 1, 2 * NA_KW - 1
    left = GRID_W - NA_KW
    v = jnp.pad(rpb * LOG2E, ((0, 0), (0, 0), (left, 2 * GRID_W - left - n_col)))
    v = jnp.tile(v, (1, 1, GRID_W))[:, :, :GRID_W * (2 * GRID_W - 1)]
    toep = v.reshape(NA_HEADS, n_rel, GRID_W, 2 * GRID_W - 1)[..., GRID_W - 1:]
    toep = jnp.pad(toep, ((0, 0), (-NA_BIAS_LO, NA_BIAS_LO + NA_BIAS_N + 1 - n_rel), (0, 0), (0, 0)))
    return jnp.concatenate([toep[:, :-1], toep[:, 1:]], axis=-1)


def _na_attn(st, q, k, v, bias_tab):
    rows = st.seq // GRID_W
    assert rows >= NA_WROWS and rows % NA_QROWS == 0
    kern = functools.partial(_na_attn_kernel, rows=rows, seq=st.seq)
    return pl.pallas_call(
        kern,
        grid=(st.batch, st.lat_tiles),
        in_specs=[st.tile_spec(D_MODEL),
                  pl.BlockSpec((None, st.tokens, D_MODEL), lambda b, j: (b, 0, 0)),
                  pl.BlockSpec((None, st.tokens, D_MODEL), lambda b, j: (b, 0, 0)),
                  _resident((NA_HEADS, NA_BIAS_N, GRID_W, 2 * GRID_W))],
        out_specs=st.tile_spec(D_MODEL),
        out_shape=jax.ShapeDtypeStruct((st.batch, st.seq, D_MODEL), BF16),
        scratch_shapes=[pltpu.VMEM((TOKEN_TILE, NA_WROWS * GRID_W), F32)],
        compiler_params=_params(2),
        name="na_attn",
    )(q, k, v, bias_tab)


def _slot_columns(w, width):
    kdim = w.shape[0]
    w = w.reshape(kdim, -1, width)
    return jnp.pad(w, ((0, 0), (0, 0), (0, LANES - width))).reshape(kdim, -1)


def _slot_gain(g):
    return jnp.pad(g, ((0, 0), (0, LANES - g.shape[1])))


def kernel(x, c, ctx, c_ctx, ada_w, ada_b, norm_g, ffn_w_in, ffn_w_out, pool_w, pool_b, pool_scale,
           diff_w_qkv, diff_qk_g, diff_lambda, diff_sub_g, diff_w_o,
           mla_w_dq, mla_q_g, mla_w_uq, mla_w_dkv, mla_kv_g, mla_w_ukv, mla_qk_g, mla_w_o,
           na_w_qkv, na_qk_g, na_rpb, na_w_o):
    assert DEPTH == N_MIXERS, "one layer per mixer: the context stream ends after the last softmax mixer's keys"
    batch, seq, _ = x.shape
    st = _Stream(batch, seq, ctx.shape[1])
    rows =-(-(batch + 1) // 16) * 16
    cc = jnp.zeros((rows, D_MODEL), F32).at[:batch].set(c).at[batch].set(c_ctx)
    mods = _adaln(cc, ada_w, ada_b)

    w_in = ffn_w_in.astype(BF16)
    w_out = ffn_w_out.astype(BF16)
    gains = norm_g.reshape(DEPTH, 3, 1, D_MODEL)

    for i in range(DEPTH):
        mod = mods[i]
        if i == 0:
            xs = _ffn_join(st, x, ctx, mod, gains[i, 0], w_in, w_out)
            xs = _pool(st, xs, mod, gains[i, 1], pool_w[0].astype(BF16), pool_b[0].reshape(1, D_MODEL),
                       pool_scale[0].reshape(1, D_MODEL))
            xs = _ffn(st, xs, mod, 2, gains[i, 2], w_in, w_out, i)
            continue
        xs = _ffn(st, xs, mod, 0, gains[i, 0], w_in, w_out, i)
        if i == 1:
            q, k, v = _qkv_pre(st, xs, mod, gains[i, 1], diff_w_qkv[0].astype(BF16), diff_qk_g[0], True)
            lam_init = 0.8 - 0.6 * math.exp(-0.3 * i)
            attn = _diff_attn(st, q, k, v, diff_lambda[0], diff_sub_g[0], lam_init)
            w_o = diff_w_o[0]
        elif i == 2:
            kpe_cols = jnp.pad(mla_w_dkv[0][:, MLA_KV_RANK:], ((0, 0), (MLA_NOPE, LANES - MLA_NOPE - MLA_ROPE)))
            w_down = jnp.concatenate([mla_w_dq[0], mla_w_dkv[0][:, :MLA_KV_RANK], kpe_cols], axis=1)
            ukv = mla_w_ukv[0].reshape(MLA_KV_RANK, MLA_HEADS, MLA_NOPE + MLA_V)
            w_ukv = jnp.concatenate([_slot_columns(ukv[:, :, :MLA_NOPE].reshape(MLA_KV_RANK, -1), MLA_NOPE),
                                     ukv[:, :, MLA_NOPE:].reshape(MLA_KV_RANK, -1)], axis=1)
            q, k, v = _mla_pre(st, xs, mod, gains[i, 1], w_down.astype(BF16),
                               mla_q_g[0].reshape(1, -1), mla_kv_g[0].reshape(1, -1),
                               _slot_columns(mla_w_uq[0], MLA_NOPE + MLA_ROPE).astype(BF16),
                               w_ukv.astype(BF16), _slot_gain(mla_qk_g[0]))
            attn = _mla_attn(st, q, k, v)
            w_o = mla_w_o[0]
        else:
            q, k, v = _qkv_pre(st, xs, mod, gains[i, 1], na_w_qkv[0].astype(BF16), na_qk_g[0], False)
            attn = _na_attn(st, q, k, v, _na_bias_table(na_rpb[0]))
            w_o = na_w_o[0]
        xs = _mixout_ffn(st, xs, attn, mod, gains[i, 2], w_o.astype(BF16), w_in, w_out, i,
                         latent_only=(i == DEPTH - 1))
    return xs
```

```python
import functools
import math

import jax
import jax.numpy as jnp
from jax import lax
from jax.experimental import pallas as pl
from jax.experimental.pallas import tpu as pltpu

D_MODEL = 1024
DEPTH = 4
GRID_W = 64
N_MIXERS = 4
N_MOD = 9
EPS = 1e-6
LOG2E = math.log2(math.e)
ROPE_BASE = 10000.0
NEG_INF = -1e30
D_FF = 2816
POOL_WINDOWS = (2, 4, 8, 16)
POOL_G = D_MODEL // len(POOL_WINDOWS)
DIFF_HEADS = 8
DIFF_DH = 64
MLA_HEADS = 16
MLA_Q_RANK = 384
MLA_KV_RANK = 256
MLA_NOPE = 64
MLA_ROPE = 32
MLA_V = 64
NA_HEADS = 16
NA_DH = 64
NA_KH = 8
NA_KW = 16

LANES = 128
HALO = 8
TOKEN_TILE = 256
ATTN_TILE = 512
FFN_TILE = 1024
PROJ_TILE = 512
VMEM_LIMIT = 56 * 2**20

F32 = jnp.float32
BF16 = jnp.bfloat16


def _params(n_axes):
    return pltpu.CompilerParams(dimension_semantics=("arbitrary",) * n_axes,
                                vmem_limit_bytes=VMEM_LIMIT)


def _resident(shape, stack_index=()):
    index = tuple(stack_index) + (0,) * len(shape)
    return pl.BlockSpec((None,) * len(stack_index) + tuple(shape), lambda *_: index, pipeline_mode=pl.Buffered(1))


def _dot(a, b):
    return jnp.dot(a, b, preferred_element_type=F32)


def _dot_t(a, b):
    return lax.dot_general(a, b, (((1,), (1,)), ((), ())), preferred_element_type=F32)


def _modulate(x, g, shift, scale):
    ms = jnp.mean(x * x, axis=-1, keepdims=True)
    return (x * lax.rsqrt(ms + EPS)) * (g * (1.0 + scale)) + shift


def _silu(x):
    return x * (1.0 / (1.0 + jnp.exp(-x)))


def _adaln_kernel(c_ref, w_ref, b_ref, o_ref):
    s = _silu(c_ref[...]).astype(BF16)
    o_ref[...] = _dot(s, w_ref[...].astype(BF16)) + b_ref[...]


def _adaln(cc, ada_w, ada_b):
    rows = cc.shape[0]
    n_col = N_MOD * D_MODEL // D_MODEL
    out = pl.pallas_call(
        _adaln_kernel,
        grid=(DEPTH, n_col),
        in_specs=[pl.BlockSpec((rows, D_MODEL), lambda i, n: (0, 0)),
                  pl.BlockSpec((None, D_MODEL, D_MODEL), lambda i, n: (i, 0, n)),
                  pl.BlockSpec((None, 1, D_MODEL), lambda i, n: (i, 0, n))],
        out_specs=pl.BlockSpec((None, rows, D_MODEL), lambda i, n: (i, 0, n)),
        out_shape=jax.ShapeDtypeStruct((DEPTH, rows, N_MOD * D_MODEL), F32),
        compiler_params=_params(2),
        name="adaln",
    )(cc, ada_w, ada_b.reshape(DEPTH, 1, N_MOD * D_MODEL))
    return out.reshape(DEPTH, rows, N_MOD, D_MODEL)


def _ffn_core(x, mod_ref, k, g_ref, win_ref, wout_ref):
    shift = mod_ref[3 * k:3 * k + 1, :]
    scale = mod_ref[3 * k + 1:3 * k + 2, :]
    gate = mod_ref[3 * k + 2:3 * k + 3, :]
    parts = [x[r:r + TOKEN_TILE] for r in range(0, x.shape[0], TOKEN_TILE)]
    ups = [_dot(_modulate(p, g_ref[...], shift, scale).astype(BF16), win_ref[...]) for p in parts]
    downs = [_dot((_silu(u[:, :D_FF]) * u[:, D_FF:]).astype(BF16), wout_ref[...]) for u in ups]
    outs = [p + (0.5 * gate) * y for p, y in zip(parts, downs)]
    return outs[0] if len(outs) == 1 else jnp.concatenate(outs, axis=0)


def _wide_tile_branches(axis, seq, body, with_ctx=True, tile=ATTN_TILE):
    j = pl.program_id(axis)
    lat_steps = seq // tile

    @pl.when(j < lat_steps)
    def _():
        body(slice(None), slice(None))

    if with_ctx:
        @pl.when(j >= lat_steps)
        def _():
            body(slice(0, TOKEN_TILE), slice(seq, None))


def _ffn_kernel(x_ref, mod_ref, g_ref, win_ref, wout_ref, o_ref, *, k, seq):
    def step(rows, _):
        o_ref[rows, :] = _ffn_core(x_ref[rows, :], mod_ref, k, g_ref, win_ref, wout_ref)

    _wide_tile_branches(1, seq, step, tile=FFN_TILE)


def _ffn_join_kernel(x_ref, c_ref, mod_ref, g_ref, win_ref, wout_ref, o_ref, *, seq):
    j = pl.program_id(1)

    @pl.when(j < seq // FFN_TILE)
    def _():
        o_ref[...] = _ffn_core(x_ref[...], mod_ref, 0, g_ref, win_ref, wout_ref)

    @pl.when(j >= seq // FFN_TILE)
    def _():
        o_ref[0:TOKEN_TILE, :] = _ffn_core(c_ref[...], mod_ref, 0, g_ref, win_ref, wout_ref)


def _mixout_ffn_kernel(x_ref, a_ref, mod_ref, g_ref, wo_ref, win_ref, wout_ref, o_ref, *, seq, with_ctx):
    def step(rows, _):
        x = x_ref[rows, :] + mod_ref[5:6, :] * _dot(a_ref[rows, :], wo_ref[...])
        o_ref[rows, :] = _ffn_core(x, mod_ref, 2, g_ref, win_ref, wout_ref)

    _wide_tile_branches(1, seq, step, with_ctx, FFN_TILE)


class _Stream:
    def __init__(self, batch, seq, ctx):
        assert seq % TOKEN_TILE == 0 and ctx == TOKEN_TILE and seq % GRID_W == 0
        self.batch, self.seq, self.ctx = batch, seq, ctx
        self.tokens = seq + ctx
        self.lat_tiles = seq // TOKEN_TILE
        self.tiles = self.tokens // TOKEN_TILE
        assert seq % ATTN_TILE == 0 and ATTN_TILE >= ctx
        self.attn_steps = seq // ATTN_TILE + 1
        assert seq % FFN_TILE == 0 and seq % PROJ_TILE == 0
        self.ffn_steps = seq // FFN_TILE + 1
        self.proj_steps = seq // PROJ_TILE + 1

    def tile_spec(self, width):
        return pl.BlockSpec((None, TOKEN_TILE, width), lambda b, j: (b, j, 0))

    def ffn_spec(self, width):
        return pl.BlockSpec((None, FFN_TILE, width), lambda b, j: (b, j, 0))

    def proj_spec(self, width):
        return pl.BlockSpec((None, PROJ_TILE, width), lambda b, j: (b, j, 0))

    def mod_spec(self, rows=TOKEN_TILE):
        lat, ctx_row = self.seq // rows, self.batch
        return pl.BlockSpec((None, N_MOD, D_MODEL),
                            lambda b, j: (jnp.where(j < lat, b, ctx_row), 0, 0))


def _ffn_weight_specs(layer, half):
    return [_resident((D_MODEL, 2 * D_FF), (layer, half)), _resident((D_FF, D_MODEL), (layer, half))]


def _ffn(st, xs, mod, k, g, w_in, w_out, layer):
    return pl.pallas_call(
        functools.partial(_ffn_kernel, k=k, seq=st.seq),
        grid=(st.batch, st.ffn_steps),
        in_specs=[st.ffn_spec(D_MODEL), st.mod_spec(FFN_TILE), _resident((1, D_MODEL))]
        + _ffn_weight_specs(layer, k // 2),
        out_specs=st.ffn_spec(D_MODEL),
        out_shape=jax.ShapeDtypeStruct((st.batch, st.tokens, D_MODEL), F32),
        compiler_params=_params(2),
        name="ffn",
    )(xs, mod, g, w_in, w_out)


def _ffn_join(st, x, ctx, mod, g, w_in, w_out):
    last_lat = st.ffn_steps - 2
    return pl.pallas_call(
        functools.partial(_ffn_join_kernel, seq=st.seq),
        grid=(st.batch, st.ffn_steps),
        in_specs=[pl.BlockSpec((None, FFN_TILE, D_MODEL), lambda b, j: (b, jnp.minimum(j, last_lat), 0)),
                  pl.BlockSpec((None, st.ctx, D_MODEL), lambda b, j: (b, 0, 0)),
                  st.mod_spec(FFN_TILE), _resident((1, D_MODEL))] + _ffn_weight_specs(0, 0),
        out_specs=st.ffn_spec(D_MODEL),
        out_shape=jax.ShapeDtypeStruct((st.batch, st.tokens, D_MODEL), F32),
        compiler_params=_params(2),
        name="ffn_join",
    )(x, ctx, mod, g, w_in, w_out)


def _mixout_ffn(st, xs, attn, mod, g, w_o, w_in, w_out, layer, latent_only):
    steps = st.ffn_steps - 1 if latent_only else st.ffn_steps
    tokens = st.seq if latent_only else st.tokens
    return pl.pallas_call(
        functools.partial(_mixout_ffn_kernel, seq=st.seq, with_ctx=not latent_only),
        grid=(st.batch, steps),
        in_specs=[st.ffn_spec(D_MODEL), st.ffn_spec(D_MODEL), st.mod_spec(FFN_TILE),
                  _resident((1, D_MODEL)), _resident((D_MODEL, D_MODEL))] + _ffn_weight_specs(layer, 1),
        out_specs=st.ffn_spec(D_MODEL),
        out_shape=jax.ShapeDtypeStruct((st.batch, tokens, D_MODEL), F32),
        compiler_params=_params(2),
        name="mixout_ffn",
    )(xs, attn, mod, g, w_o, w_in, w_out)


def _window_sum(e, start, win, rows):
    steps = [win >> (i + 1) for i in range(win.bit_length() - 1)]
    cur, base = e, start
    for idx, shift in enumerate(steps):
        n = rows + HALO * (len(steps) - 1 - idx)
        cur = cur[base:base + n] + cur[base + shift:base + shift + n]
        base = 0
    return cur


def _pool_kernel(x_ref, xp_ref, xn_ref, mod_ref, g_ref, w_ref, b_ref, s_ref, o_ref, hext_ref,
                 *, lat_tiles, tiles, seq, ctx):
    j = pl.program_id(1)
    g, shift, scale = g_ref[...], mod_ref[3:4, :], mod_ref[4:5, :]
    x = x_ref[...]
    h = _modulate(x, g, shift, scale)
    has_prev = jnp.logical_and(j != 0, j != lat_tiles)
    has_next = jnp.logical_and(j != lat_tiles - 1, j != tiles - 1)
    hext_ref[0:HALO, :] = jnp.where(has_prev, _modulate(xp_ref[...], g, shift, scale), 0.0)
    hext_ref[HALO:HALO + TOKEN_TILE, :] = h
    hext_ref[HALO + TOKEN_TILE:2 * HALO + TOKEN_TILE, :] = jnp.where(
        has_next, _modulate(xn_ref[...], g, shift, scale), 0.0)
    hext_ref[2 * HALO + TOKEN_TILE:, :] = jnp.zeros((2 * HALO, D_MODEL), F32)
    in_lat = j < lat_tiles
    t = (j - jnp.where(in_lat, 0, lat_tiles)) * TOKEN_TILE + lax.broadcasted_iota(jnp.int32, (TOKEN_TILE, 1), 0)
    n = jnp.where(in_lat, seq, ctx)
    for gi, win in enumerate(POOL_WINDOWS):
        cols = slice(gi * POOL_G, (gi + 1) * POOL_G)
        acc = _window_sum(hext_ref[:, cols], HALO - win // 2, win, TOKEN_TILE)
        lo = jnp.maximum(t - win // 2, 0)
        hi = jnp.minimum(t - win // 2 + win, n)
        mean = acc / (hi - lo).astype(F32)
        d = (mean - h[:, cols]).astype(BF16)
        y = (_dot(d, w_ref[gi]) + b_ref[:, cols]) * s_ref[:, cols]
        o_ref[:, cols] = x[:, cols] + mod_ref[5:6, cols] * y


def _pool(st, xs, mod, g, w, b, scale):
    per_tile = TOKEN_TILE // HALO
    last = st.tokens // HALO - 1
    kern = functools.partial(_pool_kernel, lat_tiles=st.lat_tiles, tiles=st.tiles, seq=st.seq, ctx=st.ctx)
    return pl.pallas_call(
        kern,
        grid=(st.batch, st.tiles),
        in_specs=[st.tile_spec(D_MODEL),
                  pl.BlockSpec((None, HALO, D_MODEL), lambda b_, j: (b_, jnp.maximum(j * per_tile - 1, 0), 0)),
                  pl.BlockSpec((None, HALO, D_MODEL), lambda b_, j: (b_, jnp.minimum((j + 1) * per_tile, last), 0)),
                  st.mod_spec(), _resident((1, D_MODEL)),
                  _resident((len(POOL_WINDOWS), POOL_G, POOL_G)),
                  _resident((1, D_MODEL)), _resident((1, D_MODEL))],
        out_specs=st.tile_spec(D_MODEL),
        out_shape=jax.ShapeDtypeStruct((st.batch, st.tokens, D_MODEL), F32),
        scratch_shapes=[pltpu.VMEM((TOKEN_TILE + 4 * HALO, D_MODEL), F32)],
        compiler_params=_params(2),
        name="pool",
    )(xs, xs, xs, mod, g, w, b, scale)


def _lane_mask(lo, hi):
    lane = lax.broadcasted_iota(jnp.int32, (1, LANES), 1)
    return jnp.logical_and(lane >= lo, lane < hi)


def _segment_mean_matrix(lengths):
    assert sum(lengths) == LANES and all(n & (n - 1) == 0 for n in lengths)
    ids = jnp.repeat(jnp.arange(len(lengths)), jnp.array(lengths), total_repeat_length=LANES)
    inv = 1.0 / jnp.array(lengths, F32)[ids]
    m = jnp.where(ids[:, None] == ids[None, :], inv[None, :], 0.0)
    return jnp.tile(m, (2, 1)).astype(BF16)


def _segment_inv_rms(x, seg_ref):
    sq = x * x
    hi = sq.astype(BF16)
    lo = (sq - hi.astype(F32)).astype(BF16)
    return lax.rsqrt(_dot(jnp.concatenate([hi, lo], axis=1), seg_ref[...]) + EPS)


def _halves_inv_rms(x):
    sq = x * x
    lower = _lane_mask(0, LANES // 2)
    low = jnp.sum(jnp.where(lower, sq, 0.0), axis=-1, keepdims=True)
    high = jnp.sum(jnp.where(lower, 0.0, sq), axis=-1, keepdims=True)
    inv_n = 2.0 / LANES
    return jnp.where(lower, lax.rsqrt(low * inv_n + EPS), lax.rsqrt(high * inv_n + EPS))


def _half_values_with_ones(v, upper):
    lane = lax.broadcasted_iota(jnp.int32, (1, LANES), 1)
    keep = (lane >= LANES // 2) if upper else (lane < LANES // 2)
    ones_lane = 0 if upper else LANES // 2
    return jnp.where(keep, v, (lane == ones_lane).astype(v.dtype)), keep, ones_lane


def _rotate(z, tabs, shift):
    c_ref, u_ref, d_ref = tabs
    return (z * c_ref[...] + pltpu.roll(z, LANES - shift, axis=1) * u_ref[...]
            + pltpu.roll(z, shift, axis=1) * d_ref[...])


def _rope_tables(st, dim, lane_lo, period, gain, post):
    nf = dim // 4
    inv = ROPE_BASE ** (-jnp.arange(nf, dtype=F32) / nf)
    t = jnp.arange(st.seq, dtype=jnp.int32)
    pos = jnp.stack([t // GRID_W, t % GRID_W], axis=-1).astype(F32)
    ang = jnp.broadcast_to(pos[:, :, None, None] * inv, (st.seq, 2, 2, nf)).reshape(st.seq, dim)
    cos, sin = jnp.cos(ang), jnp.sin(ang)
    first_half = (jnp.arange(dim) % (2 * nf)) < nf
    sin_up = jnp.where(first_half, -sin, 0.0)
    sin_dn = jnp.where(first_half, 0.0, sin)

    def place(tab, fill, lane_gain):
        blk = jnp.full((st.seq, period), fill, F32).at[:, lane_lo:lane_lo + dim].set(tab)
        blk = jnp.tile(blk, (1, LANES // period))
        return jnp.concatenate([blk, jnp.full((st.ctx, LANES), fill, F32)], axis=0) * (lane_gain * post)

    return (place(cos, 1.0, gain), place(sin_up, 0.0, jnp.roll(gain, -nf)), place(sin_dn, 0.0, jnp.roll(gain, nf)))


def _qkv_pre_kernel(*refs, rope, seq):
    if rope:
        x_ref, mod_ref, g_ref, w_ref, seg_ref, *tabs, q_ref, k_ref, v_ref = refs
    else:
        x_ref, mod_ref, g_ref, w_ref, gain_ref, q_ref, k_ref, v_ref = refs

    def step(rows, _):
        h = _modulate(x_ref[rows, :], g_ref[...], mod_ref[3:4, :], mod_ref[4:5, :]).astype(BF16)

        def project(t):
            return _dot(h, w_ref[:, t * D_MODEL:(t + 1) * D_MODEL])

        def finish(t, x, dst):
            for c in range(D_MODEL // LANES):
                xb = x[:, c * LANES:(c + 1) * LANES]
                if rope:
                    row_tabs = [tab.at[rows, :] for tab in tabs[3 * t:3 * t + 3]]
                    y = _rotate(xb * _segment_inv_rms(xb, seg_ref), row_tabs, DIFF_DH // 4)
                else:
                    y = xb * _halves_inv_rms(xb) * gain_ref[t:t + 1, :]
                dst[rows, c * LANES:(c + 1) * LANES] = y.astype(BF16)

        q, k = project(0), project(1)
        finish(0, q, q_ref)
        v = project(2)
        finish(1, k, k_ref)
        v_ref[rows, :] = v.astype(BF16)

    _wide_tile_branches(1, seq, step, tile=PROJ_TILE)


def _qkv_pre(st, xs, mod, g, w_qkv, qk_gain, rope):
    tab_spec = pl.BlockSpec((PROJ_TILE, LANES), lambda b, j: (j, 0))
    in_specs = [st.proj_spec(D_MODEL), st.mod_spec(PROJ_TILE), _resident((1, D_MODEL)),
                _resident((D_MODEL, 3 * D_MODEL))]
    args = [xs, mod, g, w_qkv]
    lane_gain = jnp.tile(qk_gain, (1, 2))
    post = (DIFF_DH ** -0.5 * LOG2E, 1.0)
    if rope:
        in_specs += [_resident((2 * LANES, LANES))] + [tab_spec] * 6
        args += [_segment_mean_matrix((DIFF_DH, DIFF_DH))]
        for t in range(2):
            args += list(_rope_tables(st, DIFF_DH, 0, DIFF_DH, lane_gain[t], post[t]))
    else:
        in_specs += [_resident((2, LANES))]
        args += [lane_gain * jnp.array(post, F32)[:, None]]
    shape = jax.ShapeDtypeStruct((st.batch, st.tokens, D_MODEL), BF16)
    return pl.pallas_call(
        functools.partial(_qkv_pre_kernel, rope=rope, seq=st.seq),
        grid=(st.batch, st.proj_steps),
        in_specs=in_specs,
        out_specs=[st.proj_spec(D_MODEL)] * 3,
        out_shape=[shape] * 3,
        compiler_params=_params(2),
        name="qkv_pre",
    )(*args)


DIFF_HEADS_PER_STEP = 8


def _diff_attn_kernel(q_ref, k_ref, v_ref, lam_ref, subg_ref, o_ref, *, seq, lam_init):
    lp = lam_ref[...]
    lam = (jnp.exp(jnp.sum(lp[0:1] * lp[1:2], axis=-1, keepdims=True))
           - jnp.exp(jnp.sum(lp[2:3] * lp[3:4], axis=-1, keepdims=True)) + lam_init)

    def attend(query_rows, key_rows):
        def logits(hd):
            cols = slice(hd * LANES, (hd + 1) * LANES)
            q, k = q_ref[query_rows, cols], k_ref[key_rows, cols]
            return [_dot_t(jnp.where(_lane_mask(c * DIFF_DH, (c + 1) * DIFF_DH), q, jnp.zeros_like(q)), k)
                    for c in range(2)]

        s_next = logits(0)
        for hd in range(DIFF_HEADS_PER_STEP):
            cols = slice(hd * LANES, (hd + 1) * LANES)
            s_cur = s_next
            if hd + 1 < DIFF_HEADS_PER_STEP:
                s_next = logits(hd + 1)
            probs = []
            for s in s_cur:
                p = jnp.exp2(s - jnp.max(s, axis=-1, keepdims=True))
                probs.append((p, 1.0 / jnp.sum(p, axis=-1, keepdims=True)))
            (p0, r0), (p1, r1) = probs
            a = (p0 - p1 * (lam * r1 / r0)).astype(BF16)
            o = _dot(a, v_ref[key_rows, cols]) * r0
            ms = jnp.mean(o * o, axis=-1, keepdims=True)
            o_ref[query_rows, cols] = ((o * lax.rsqrt(ms + EPS)) * subg_ref[...] * (1.0 - lam_init)).astype(BF16)

    _wide_tile_branches(2, seq, attend)


def _diff_attn(st, q, k, v, lam_p, sub_g, lam_init):
    width = DIFF_HEADS_PER_STEP * LANES
    head = lambda b, h, j: (b, j, h)
    keys = lambda b, h, j: (b, 0, h)
    kern = functools.partial(_diff_attn_kernel, seq=st.seq, lam_init=lam_init)
    return pl.pallas_call(
        kern,
        grid=(st.batch, DIFF_HEADS // DIFF_HEADS_PER_STEP, st.attn_steps),
        in_specs=[pl.BlockSpec((None, ATTN_TILE, width), head),
                  pl.BlockSpec((None, st.tokens, width), keys),
                  pl.BlockSpec((None, st.tokens, width), keys),
                  _resident((4, DIFF_DH)), _resident((1, LANES))],
        out_specs=pl.BlockSpec((None, ATTN_TILE, width), head),
        out_shape=jax.ShapeDtypeStruct((st.batch, st.tokens, D_MODEL), BF16),
        compiler_params=_params(3),
        name="diff_attn",
    )(q, k, v, lam_p, sub_g.reshape(1, LANES))


MLA_DOWN = MLA_Q_RANK + MLA_KV_RANK + LANES


def _mla_pre_kernel(x_ref, mod_ref, g_ref, wd_ref, qg_ref, kvg_ref, wuq_ref, wukv_ref, seg_ref, kgain_ref,
                    qc_ref, qu_ref, qd_ref, kc_ref, ku_ref, kd_ref, q_ref, k_ref, v_ref, *, seq):
    def full_norm(z, gain):
        ms = jnp.mean(z * z, axis=-1, keepdims=True)
        return ((z * lax.rsqrt(ms + EPS)) * gain).astype(BF16)

    def step(rows, _):
        q_tabs = [tab.at[rows, :] for tab in (qc_ref, qu_ref, qd_ref)]
        k_tabs = [tab.at[rows, :] for tab in (kc_ref, ku_ref, kd_ref)]
        h = _modulate(x_ref[rows, :], g_ref[...], mod_ref[3:4, :], mod_ref[4:5, :]).astype(BF16)
        down = _dot(h, wd_ref[...])
        q = _dot(full_norm(down[:, :MLA_Q_RANK], qg_ref[...]), wuq_ref[...])
        kv = _dot(full_norm(down[:, MLA_Q_RANK:MLA_Q_RANK + MLA_KV_RANK], kvg_ref[...]), wukv_ref[...])
        kpe = down[:, MLA_Q_RANK + MLA_KV_RANK:]
        kpe = _rotate(kpe * _segment_inv_rms(kpe, seg_ref), k_tabs, MLA_ROPE // 4)
        for hd in range(MLA_HEADS):
            cols = slice(hd * LANES, (hd + 1) * LANES)
            qb = q[:, cols]
            q_ref[rows, cols] = _rotate(qb * _segment_inv_rms(qb, seg_ref), q_tabs, MLA_ROPE // 4).astype(BF16)
            kb = kv[:, cols]
            k_ref[rows, cols] = (kb * _segment_inv_rms(kb, seg_ref) * kgain_ref[...] + kpe).astype(BF16)
        v_ref[rows, :] = kv[:, MLA_HEADS * LANES:].astype(BF16)

    _wide_tile_branches(1, seq, step, tile=PROJ_TILE)


def _mla_pre(st, xs, mod, g, w_down, q_g, kv_g, w_uq, w_ukv, qk_gain):
    tab_spec = pl.BlockSpec((PROJ_TILE, LANES), lambda b, j: (j, 0))
    slots = MLA_HEADS * LANES
    scale = (MLA_NOPE + MLA_ROPE) ** -0.5 * LOG2E
    q_tabs = _rope_tables(st, MLA_ROPE, MLA_NOPE, LANES, qk_gain[0], scale)
    k_tabs = _rope_tables(st, MLA_ROPE, MLA_NOPE, LANES, qk_gain[1], 1.0)
    seg = _segment_mean_matrix((MLA_NOPE, MLA_ROPE, LANES - MLA_NOPE - MLA_ROPE))
    return pl.pallas_call(
        functools.partial(_mla_pre_kernel, seq=st.seq),
        grid=(st.batch, st.proj_steps),
        in_specs=[st.proj_spec(D_MODEL), st.mod_spec(PROJ_TILE), _resident((1, D_MODEL)),
                  _resident((D_MODEL, MLA_DOWN)), _resident((1, MLA_Q_RANK)), _resident((1, MLA_KV_RANK)),
                  _resident((MLA_Q_RANK, slots)), _resident((MLA_KV_RANK, slots + D_MODEL)),
                  _resident((2 * LANES, LANES)), _resident((1, LANES))] + [tab_spec] * 6,
        out_specs=[st.proj_spec(slots), st.proj_spec(slots), st.proj_spec(D_MODEL)],
        out_shape=[jax.ShapeDtypeStruct((st.batch, st.tokens, slots), BF16),
                   jax.ShapeDtypeStruct((st.batch, st.tokens, slots), BF16),
                   jax.ShapeDtypeStruct((st.batch, st.tokens, D_MODEL), BF16)],
        compiler_params=_params(2),
        name="mla_pre",
    )(xs, mod, g, w_down, q_g, kv_g, w_uq, w_ukv, seg, qk_gain[1:2], *q_tabs, *k_tabs)


MLA_PAIRS_PER_STEP = 4


def _mla_attn_kernel(q_ref, k_ref, v_ref, o_ref, *, seq):
    def attend(query_rows, key_rows):
        def logits(head):
            slot = slice(head * LANES, (head + 1) * LANES)
            return _dot_t(q_ref[query_rows, slot], k_ref[key_rows, slot])

        s_next = logits(0)
        for pair in range(MLA_PAIRS_PER_STEP):
            v = v_ref[key_rows, pair * LANES:(pair + 1) * LANES]
            out = None
            for hh in range(2):
                head = 2 * pair + hh
                s = s_next
                if head + 1 < 2 * MLA_PAIRS_PER_STEP:
                    s_next = logits(head + 1)
                p = jnp.exp2(s - jnp.max(s, axis=-1, keepdims=True))
                vh, keep, ones_lane = _half_values_with_ones(v, hh == 1)
                o = _dot(p.astype(BF16), vh)
                o = o * (1.0 / o[:, ones_lane:ones_lane + 1])
                out = o if out is None else jnp.where(keep, o, out)
            o_ref[query_rows, pair * LANES:(pair + 1) * LANES] = out.astype(BF16)

    _wide_tile_branches(2, seq, attend)


def _mla_attn(st, q, k, v):
    slots = 2 * MLA_PAIRS_PER_STEP * LANES
    vals = MLA_PAIRS_PER_STEP * LANES
    kern = functools.partial(_mla_attn_kernel, seq=st.seq)
    return pl.pallas_call(
        kern,
        grid=(st.batch, MLA_HEADS // (2 * MLA_PAIRS_PER_STEP), st.attn_steps),
        in_specs=[pl.BlockSpec((None, ATTN_TILE, slots), lambda b, h, j: (b, j, h)),
                  pl.BlockSpec((None, st.tokens, slots), lambda b, h, j: (b, 0, h)),
                  pl.BlockSpec((None, st.tokens, vals), lambda b, h, j: (b, 0, h))],
        out_specs=pl.BlockSpec((None, ATTN_TILE, vals), lambda b, h, j: (b, j, h)),
        out_shape=jax.ShapeDtypeStruct((st.batch, st.tokens, D_MODEL), BF16),
        compiler_params=_params(3),
        name="mla_attn",
    )(q, k, v)


NA_QROWS = TOKEN_TILE // GRID_W
NA_WROWS = NA_QROWS + NA_KH
NA_PAIRS = NA_WROWS // 2
NA_BIAS_LO = -(NA_QROWS - 1) - (NA_WROWS - NA_QROWS - NA_KH // 2) + NA_KH - 1
NA_BIAS_N = (NA_WROWS - 2) + NA_KH - 1 - NA_BIAS_LO + 1


def _na_attn_kernel(q_ref, k_ref, v_ref, bias_ref, o_ref, cap_ref, *, rows, seq):
    r0 = pl.program_id(1) * NA_QROWS
    w0 = jnp.clip(r0 - NA_KH // 2, 0, rows - NA_WROWS)
    start = pl.multiple_of(w0 * GRID_W, GRID_W)
    n_loc = NA_WROWS * GRID_W
    qidx = lax.broadcasted_iota(jnp.int32, (TOKEN_TILE, n_loc), 0)
    kidx = lax.broadcasted_iota(jnp.int32, (TOKEN_TILE, n_loc), 1)
    q_row, q_col = r0 + qidx // GRID_W, qidx % GRID_W
    k_row, k_col = w0 + kidx // GRID_W, kidx % GRID_W
    row_start = jnp.clip(q_row - NA_KH // 2, 0, rows - NA_KH)
    col_start = jnp.clip(q_col - NA_KW // 2, 0, GRID_W - NA_KW)
    valid = ((k_row >= row_start) & (k_row < row_start + NA_KH)
             & (k_col >= col_start) & (k_col < col_start + NA_KW))
    cap_ref[...] = jnp.where(valid, jnp.inf, NEG_INF)
    base = w0 - r0 + NA_KH - 1 - NA_BIAS_LO

    def logits(head):
        pair, hh = divmod(head, 2)
        cols = slice(pair * LANES, (pair + 1) * LANES)
        q = q_ref[:, cols]
        qh = jnp.where(_lane_mask(hh * NA_DH, (hh + 1) * NA_DH), q, jnp.zeros_like(q))
        return _dot_t(qh, k_ref[pl.ds(start, n_loc), cols]), _dot_t(qh, k_ref[seq:, cols])

    s_next = logits(0)
    for pair in range(NA_HEADS // 2):
        cols = slice(pair * LANES, (pair + 1) * LANES)
        v_loc, v_ctx = v_ref[pl.ds(start, n_loc), cols], v_ref[seq:, cols]
        out = None
        for hh in range(2):
            head = 2 * pair + hh
            s_loc, s_ctx = s_next
            if head + 1 < NA_HEADS:
                s_next = logits(head + 1)
            bias = jnp.concatenate(
                [jnp.concatenate([bias_ref[head, base + 2 * p - a] for p in range(NA_PAIRS)], axis=1)
                 for a in range(NA_QROWS)], axis=0)
            s_loc = jnp.minimum(s_loc + bias, cap_ref[...])
            m = jnp.maximum(jnp.max(s_loc, axis=-1, keepdims=True), jnp.max(s_ctx, axis=-1, keepdims=True))
            p_loc, p_ctx = jnp.exp2(s_loc - m), jnp.exp2(s_ctx - m)
            vh_loc, keep, ones_lane = _half_values_with_ones(v_loc, hh == 1)
            vh_ctx, _, _ = _half_values_with_ones(v_ctx, hh == 1)
            o = _dot(p_loc.astype(BF16), vh_loc) + _dot(p_ctx.astype(BF16), vh_ctx)
            o = o * (1.0 / o[:, ones_lane:ones_lane + 1])
            out = o if out is None else jnp.where(keep, o, out)
        o_ref[:, cols] = out.astype(BF16)


def _na_bias_table(rpb):
    n_rel, n_col = 2 * NA_KH - 1, 2 * NA_KW - 1
    left = GRID_W - NA_KW
    v = jnp.pad(rpb * LOG2E, ((0, 0), (0, 0), (left, 2 * GRID_W - left - n_col)))
    v = jnp.tile(v, (1, 1, GRID_W))[:, :, :GRID_W * (2 * GRID_W - 1)]
    toep = v.reshape(NA_HEADS, n_rel, GRID_W, 2 * GRID_W - 1)[..., GRID_W - 1:]
    toep = jnp.pad(toep, ((0, 0), (-NA_BIAS_LO, NA_BIAS_LO + NA_BIAS_N + 1 - n_rel), (0, 0), (0, 0)))
    return jnp.concatenate([toep[:, :-1], toep[:, 1:]], axis=-1)


def _na_attn(st, q, k, v, bias_tab):
    rows = st.seq // GRID_W
    assert rows >= NA_WROWS and rows % NA_QROWS == 0
    kern = functools.partial(_na_attn_kernel, rows=rows, seq=st.seq)
    return pl.pallas_call(
        kern,
        grid=(st.batch, st.lat_tiles),
        in_specs=[st.tile_spec(D_MODEL),
                  pl.BlockSpec((None, st.tokens, D_MODEL), lambda b, j: (b, 0, 0)),
                  pl.BlockSpec((None, st.tokens, D_MODEL), lambda b, j: (b, 0, 0)),
                  _resident((NA_HEADS, NA_BIAS_N, GRID_W, 2 * GRID_W))],
        out_specs=st.tile_spec(D_MODEL),
        out_shape=jax.ShapeDtypeStruct((st.batch, st.seq, D_MODEL), BF16),
        scratch_shapes=[pltpu.VMEM((TOKEN_TILE, NA_WROWS * GRID_W), F32)],
        compiler_params=_params(2),
        name="na_attn",
    )(q, k, v, bias_tab)


def _slot_columns(w, width):
    kdim = w.shape[0]
    w = w.reshape(kdim, -1, width)
    return jnp.pad(w, ((0, 0), (0, 0), (0, LANES - width))).reshape(kdim, -1)


def _slot_gain(g):
    return jnp.pad(g, ((0, 0), (0, LANES - g.shape[1])))


def kernel(x, c, ctx, c_ctx, ada_w, ada_b, norm_g, ffn_w_in, ffn_w_out, pool_w, pool_b, pool_scale,
           diff_w_qkv, diff_qk_g, diff_lambda, diff_sub_g, diff_w_o,
           mla_w_dq, mla_q_g, mla_w_uq, mla_w_dkv, mla_kv_g, mla_w_ukv, mla_qk_g, mla_w_o,
           na_w_qkv, na_qk_g, na_rpb, na_w_o):
    assert DEPTH == N_MIXERS, "one layer per mixer: the context stream ends after the last softmax mixer's keys"
    batch, seq, _ = x.shape
    st = _Stream(batch, seq, ctx.shape[1])
    rows =-(-(batch + 1) // 16) * 16
    cc = jnp.zeros((rows, D_MODEL), F32).at[:batch].set(c).at[batch].set(c_ctx)
    mods = _adaln(cc, ada_w, ada_b)

    w_in = ffn_w_in.astype(BF16)
    w_out = ffn_w_out.astype(BF16)
    gains = norm_g.reshape(DEPTH, 3, 1, D_MODEL)

    for i in range(DEPTH):
        mod = mods[i]
        if i == 0:
            xs = _ffn_join(st, x, ctx, mod, gains[i, 0], w_in, w_out)
            xs = _pool(st, xs, mod, gains[i, 1], pool_w[0].astype(BF16), pool_b[0].reshape(1, D_MODEL),
                       pool_scale[0].reshape(1, D_MODEL))
            xs = _ffn(st, xs, mod, 2, gains[i, 2], w_in, w_out, i)
            continue
        xs = _ffn(st, xs, mod, 0, gains[i, 0], w_in, w_out, i)
        if i == 1:
            q, k, v = _qkv_pre(st, xs, mod, gains[i, 1], diff_w_qkv[0].astype(BF16), diff_qk_g[0], True)
            lam_init = 0.8 - 0.6 * math.exp(-0.3 * i)
            attn = _diff_attn(st, q, k, v, diff_lambda[0], diff_sub_g[0], lam_init)
            w_o = diff_w_o[0]
        elif i == 2:
            kpe_cols = jnp.pad(mla_w_dkv[0][:, MLA_KV_RANK:], ((0, 0), (MLA_NOPE, LANES - MLA_NOPE - MLA_ROPE)))
            w_down = jnp.concatenate([mla_w_dq[0], mla_w_dkv[0][:, :MLA_KV_RANK], kpe_cols], axis=1)
            ukv = mla_w_ukv[0].reshape(MLA_KV_RANK, MLA_HEADS, MLA_NOPE + MLA_V)
            w_ukv = jnp.concatenate([_slot_columns(ukv[:, :, :MLA_NOPE].reshape(MLA_KV_RANK, -1), MLA_NOPE),
                                     ukv[:, :, MLA_NOPE:].reshape(MLA_KV_RANK, -1)], axis=1)
            q, k, v = _mla_pre(st, xs, mod, gains[i, 1], w_down.astype(BF16),
                               mla_q_g[0].reshape(1, -1), mla_kv_g[0].reshape(1, -1),
                               _slot_columns(mla_w_uq[0], MLA_NOPE + MLA_ROPE).astype(BF16),
                               w_ukv.astype(BF16), _slot_gain(mla_qk_g[0]))
            attn = _mla_attn(st, q, k, v)
            w_o = mla_w_o[0]
        else:
            q, k, v = _qkv_pre(st, xs, mod, gains[i, 1], na_w_qkv[0].astype(BF16), na_qk_g[0], False)
            attn = _na_attn(st, q, k, v, _na_bias_table(na_rpb[0]))
            w_o = na_w_o[0]
        xs = _mixout_ffn(st, xs, attn, mod, gains[i, 2], w_o.astype(BF16), w_in, w_out, i,
                         latent_only=(i == DEPTH - 1))
    return xs
```

```python
import functools
import math

import jax
import jax.numpy as jnp
from jax import lax
from jax.experimental import pallas as pl
from jax.experimental.pallas import tpu as pltpu

D_MODEL = 1024
DEPTH = 4
GRID_W = 64
N_MIXERS = 4
N_MOD = 9
EPS = 1e-6
LOG2E = math.log2(math.e)
ROPE_BASE = 10000.0
NEG_INF = -1e30
D_FF = 2816
POOL_WINDOWS = (2, 4, 8, 16)
POOL_G = D_MODEL // len(POOL_WINDOWS)
DIFF_HEADS = 8
DIFF_DH = 64
MLA_HEADS = 16
MLA_Q_RANK = 384
MLA_KV_RANK = 256
MLA_NOPE = 64
MLA_ROPE = 32
MLA_V = 64
NA_HEADS = 16
NA_DH = 64
NA_KH = 8
NA_KW = 16

LANES = 128
HALO = 8
BF16_SUBLANES = 16
TOKEN_TILE = 256
ATTN_TILE = 512
FFN_TILE = 1024
PROJ_TILE = 512
VMEM_LIMIT = 56 * 2**20

F32 = jnp.float32
BF16 = jnp.bfloat16


def _params(n_axes):
    return pltpu.CompilerParams(dimension_semantics=("arbitrary",) * n_axes,
                                vmem_limit_bytes=VMEM_LIMIT)


def _resident(shape, stack_index=()):
    index = tuple(stack_index) + (0,) * len(shape)
    return pl.BlockSpec((None,) * len(stack_index) + tuple(shape), lambda *_: index, pipeline_mode=pl.Buffered(1))


def _dot(a, b):
    return jnp.dot(a, b, preferred_element_type=F32)


def _dot_t(a, b):
    return lax.dot_general(a, b, (((1,), (1,)), ((), ())), preferred_element_type=F32)


def _modulate(x, g, shift, scale):
    ms = jnp.mean(x * x, axis=-1, keepdims=True)
    return (x * lax.rsqrt(ms + EPS)) * (g * (1.0 + scale)) + shift


def _silu(x):
    return x * (1.0 / (1.0 + jnp.exp(-x)))


def _adaln_kernel(c_ref, w_ref, b_ref, o_ref):
    s = _silu(c_ref[...]).astype(BF16)
    o_ref[...] = _dot(s, w_ref[...].astype(BF16)) + b_ref[...]


def _adaln(cc, ada_w, ada_b):
    rows = cc.shape[0]
    out = pl.pallas_call(
        _adaln_kernel,
        grid=(DEPTH, N_MOD),
        in_specs=[pl.BlockSpec((rows, D_MODEL), lambda i, n: (0, 0)),
                  pl.BlockSpec((None, D_MODEL, D_MODEL), lambda i, n: (i, 0, n)),
                  pl.BlockSpec((None, 1, D_MODEL), lambda i, n: (i, 0, n))],
        out_specs=pl.BlockSpec((None, rows, D_MODEL), lambda i, n: (i, 0, n)),
        out_shape=jax.ShapeDtypeStruct((DEPTH, rows, N_MOD * D_MODEL), F32),
        compiler_params=_params(2),
        name="adaln",
    )(cc, ada_w, ada_b.reshape(DEPTH, 1, N_MOD * D_MODEL))
    return out.reshape(DEPTH, rows, N_MOD, D_MODEL)


def _ffn_core(x, mod_ref, k, g_ref, win_ref, wout_ref):
    shift = mod_ref[3 * k:3 * k + 1, :]
    scale = mod_ref[3 * k + 1:3 * k + 2, :]
    gate = mod_ref[3 * k + 2:3 * k + 3, :]
    parts = [x[r:r + TOKEN_TILE] for r in range(0, x.shape[0], TOKEN_TILE)]
    ups = [_dot(_modulate(p, g_ref[...], shift, scale).astype(BF16), win_ref[...]) for p in parts]
    downs = [_dot((_silu(u[:, :D_FF]) * u[:, D_FF:]).astype(BF16), wout_ref[...]) for u in ups]
    outs = [p + (0.5 * gate) * y for p, y in zip(parts, downs)]
    return outs[0] if len(outs) == 1 else jnp.concatenate(outs, axis=0)


def _wide_tile_branches(axis, seq, body, with_ctx=True, tile=ATTN_TILE):
    j = pl.program_id(axis)
    lat_steps = seq // tile

    @pl.when(j < lat_steps)
    def _():
        body(slice(None), slice(None))

    if with_ctx:
        @pl.when(j >= lat_steps)
        def _():
            body(slice(0, TOKEN_TILE), slice(seq, None))


def _ffn_kernel(x_ref, mod_ref, g_ref, win_ref, wout_ref, o_ref, *, k, seq):
    def step(rows, _):
        o_ref[rows, :] = _ffn_core(x_ref[rows, :], mod_ref, k, g_ref, win_ref, wout_ref)

    _wide_tile_branches(1, seq, step, tile=FFN_TILE)


def _ffn_join_kernel(x_ref, c_ref, mod_ref, g_ref, win_ref, wout_ref, o_ref, *, seq):
    j = pl.program_id(1)

    @pl.when(j < seq // FFN_TILE)
    def _():
        o_ref[...] = _ffn_core(x_ref[...], mod_ref, 0, g_ref, win_ref, wout_ref)

    @pl.when(j >= seq // FFN_TILE)
    def _():
        o_ref[0:TOKEN_TILE, :] = _ffn_core(c_ref[...], mod_ref, 0, g_ref, win_ref, wout_ref)


def _mixout_ffn_kernel(x_ref, a_ref, mod_ref, g_ref, wo_ref, win_ref, wout_ref, o_ref, *, seq, with_ctx):
    def step(rows, _):
        x = x_ref[rows, :] + mod_ref[5:6, :] * _dot(a_ref[rows, :], wo_ref[...])
        o_ref[rows, :] = _ffn_core(x, mod_ref, 2, g_ref, win_ref, wout_ref)

    _wide_tile_branches(1, seq, step, with_ctx, FFN_TILE)


class _Stream:
    def __init__(self, batch, seq, ctx):
        assert seq % TOKEN_TILE == 0 and ctx == TOKEN_TILE and seq % GRID_W == 0
        self.batch, self.seq, self.ctx = batch, seq, ctx
        self.tokens = seq + ctx
        self.lat_tiles = seq // TOKEN_TILE
        self.tiles = self.tokens // TOKEN_TILE
        assert seq % ATTN_TILE == 0 and ATTN_TILE >= ctx
        self.attn_steps = seq // ATTN_TILE + 1
        assert seq % FFN_TILE == 0 and seq % PROJ_TILE == 0
        self.ffn_steps = seq // FFN_TILE + 1
        self.proj_steps = seq // PROJ_TILE + 1

    def tile_spec(self, width):
        return pl.BlockSpec((None, TOKEN_TILE, width), lambda b, j: (b, j, 0))

    def ffn_spec(self, width):
        return pl.BlockSpec((None, FFN_TILE, width), lambda b, j: (b, j, 0))

    def proj_spec(self, width):
        return pl.BlockSpec((None, PROJ_TILE, width), lambda b, j: (b, j, 0))

    def mod_spec(self, rows=TOKEN_TILE):
        lat, ctx_row = self.seq // rows, self.batch
        return pl.BlockSpec((None, N_MOD, D_MODEL),
                            lambda b, j: (jnp.where(j < lat, b, ctx_row), 0, 0))


def _ffn_weight_specs(layer, half):
    return [_resident((D_MODEL, 2 * D_FF), (layer, half)), _resident((D_FF, D_MODEL), (layer, half))]


def _ffn(st, xs, mod, k, g, w_in, w_out, layer):
    return pl.pallas_call(
        functools.partial(_ffn_kernel, k=k, seq=st.seq),
        grid=(st.batch, st.ffn_steps),
        in_specs=[st.ffn_spec(D_MODEL), st.mod_spec(FFN_TILE), _resident((1, D_MODEL))]
        + _ffn_weight_specs(layer, k // 2),
        out_specs=st.ffn_spec(D_MODEL),
        out_shape=jax.ShapeDtypeStruct((st.batch, st.tokens, D_MODEL), F32),
        compiler_params=_params(2),
        name="ffn",
    )(xs, mod, g, w_in, w_out)


def _ffn_join(st, x, ctx, mod, g, w_in, w_out):
    last_lat = st.ffn_steps - 2
    return pl.pallas_call(
        functools.partial(_ffn_join_kernel, seq=st.seq),
        grid=(st.batch, st.ffn_steps),
        in_specs=[pl.BlockSpec((None, FFN_TILE, D_MODEL), lambda b, j: (b, jnp.minimum(j, last_lat), 0)),
                  pl.BlockSpec((None, st.ctx, D_MODEL), lambda b, j: (b, 0, 0)),
                  st.mod_spec(FFN_TILE), _resident((1, D_MODEL))] + _ffn_weight_specs(0, 0),
        out_specs=st.ffn_spec(D_MODEL),
        out_shape=jax.ShapeDtypeStruct((st.batch, st.tokens, D_MODEL), F32),
        compiler_params=_params(2),
        name="ffn_join",
    )(x, ctx, mod, g, w_in, w_out)


def _mixout_ffn(st, xs, attn, mod, g, w_o, w_in, w_out, layer, latent_only):
    steps = st.ffn_steps - 1 if latent_only else st.ffn_steps
    tokens = st.seq if latent_only else st.tokens
    return pl.pallas_call(
        functools.partial(_mixout_ffn_kernel, seq=st.seq, with_ctx=not latent_only),
        grid=(st.batch, steps),
        in_specs=[st.ffn_spec(D_MODEL), st.ffn_spec(D_MODEL), st.mod_spec(FFN_TILE),
                  _resident((1, D_MODEL)), _resident((D_MODEL, D_MODEL))] + _ffn_weight_specs(layer, 1),
        out_specs=st.ffn_spec(D_MODEL),
        out_shape=jax.ShapeDtypeStruct((st.batch, tokens, D_MODEL), F32),
        compiler_params=_params(2),
        name="mixout_ffn",
    )(xs, attn, mod, g, w_o, w_in, w_out)


def _window_sum(e, start, win, rows):
    steps = [win >> (i + 1) for i in range(win.bit_length() - 1)]
    cur, base = e, start
    for idx, shift in enumerate(steps):
        n = rows + HALO * (len(steps) - 1 - idx)
        cur = cur[base:base + n] + cur[base + shift:base + shift + n]
        base = 0
    return cur


def _pool_kernel(x_ref, xp_ref, xn_ref, mod_ref, g_ref, w_ref, b_ref, s_ref, o_ref, hext_ref,
                 *, lat_tiles, tiles, seq, ctx):
    j = pl.program_id(1)
    g, shift, scale = g_ref[...], mod_ref[3:4, :], mod_ref[4:5, :]
    x = x_ref[...]
    h = _modulate(x, g, shift, scale)
    has_prev = jnp.logical_and(j != 0, j != lat_tiles)
    has_next = jnp.logical_and(j != lat_tiles - 1, j != tiles - 1)
    hext_ref[0:HALO, :] = jnp.where(has_prev, _modulate(xp_ref[...], g, shift, scale), 0.0)
    hext_ref[HALO:HALO + TOKEN_TILE, :] = h
    hext_ref[HALO + TOKEN_TILE:2 * HALO + TOKEN_TILE, :] = jnp.where(
        has_next, _modulate(xn_ref[...], g, shift, scale), 0.0)
    hext_ref[2 * HALO + TOKEN_TILE:, :] = jnp.zeros((2 * HALO, D_MODEL), F32)
    in_lat = j < lat_tiles
    t = (j - jnp.where(in_lat, 0, lat_tiles)) * TOKEN_TILE + lax.broadcasted_iota(jnp.int32, (TOKEN_TILE, 1), 0)
    n = jnp.where(in_lat, seq, ctx)
    for gi, win in enumerate(POOL_WINDOWS):
        cols = slice(gi * POOL_G, (gi + 1) * POOL_G)
        acc = _window_sum(hext_ref[:, cols], HALO - win // 2, win, TOKEN_TILE)
        lo = jnp.maximum(t - win // 2, 0)
        hi = jnp.minimum(t - win // 2 + win, n)
        mean = acc / (hi - lo).astype(F32)
        d = (mean - h[:, cols]).astype(BF16)
        y = (_dot(d, w_ref[gi]) + b_ref[:, cols]) * s_ref[:, cols]
        o_ref[:, cols] = x[:, cols] + mod_ref[5:6, cols] * y


def _pool(st, xs, mod, g, w, b, scale):
    per_tile = TOKEN_TILE // HALO
    last = st.tokens // HALO - 1
    kern = functools.partial(_pool_kernel, lat_tiles=st.lat_tiles, tiles=st.tiles, seq=st.seq, ctx=st.ctx)
    return pl.pallas_call(
        kern,
        grid=(st.batch, st.tiles),
        in_specs=[st.tile_spec(D_MODEL),
                  pl.BlockSpec((None, HALO, D_MODEL), lambda b_, j: (b_, jnp.maximum(j * per_tile - 1, 0), 0)),
                  pl.BlockSpec((None, HALO, D_MODEL), lambda b_, j: (b_, jnp.minimum((j + 1) * per_tile, last), 0)),
                  st.mod_spec(), _resident((1, D_MODEL)),
                  _resident((len(POOL_WINDOWS), POOL_G, POOL_G)),
                  _resident((1, D_MODEL)), _resident((1, D_MODEL))],
        out_specs=st.tile_spec(D_MODEL),
        out_shape=jax.ShapeDtypeStruct((st.batch, st.tokens, D_MODEL), F32),
        scratch_shapes=[pltpu.VMEM((TOKEN_TILE + 4 * HALO, D_MODEL), F32)],
        compiler_params=_params(2),
        name="pool",
    )(xs, xs, xs, mod, g, w, b, scale)


def _lane_mask(lo, hi):
    lane = lax.broadcasted_iota(jnp.int32, (1, LANES), 1)
    return jnp.logical_and(lane >= lo, lane < hi)


def _segment_mean_matrix(lengths):
    assert sum(lengths) == LANES and all(n & (n - 1) == 0 for n in lengths)
    ids = jnp.repeat(jnp.arange(len(lengths)), jnp.array(lengths), total_repeat_length=LANES)
    inv = 1.0 / jnp.array(lengths, F32)[ids]
    m = jnp.where(ids[:, None] == ids[None, :], inv[None, :], 0.0)
    return jnp.tile(m, (2, 1)).astype(BF16)


def _segment_inv_rms(x, seg_ref):
    sq = x * x
    hi = sq.astype(BF16)
    lo = (sq - hi.astype(F32)).astype(BF16)
    return lax.rsqrt(_dot(jnp.concatenate([hi, lo], axis=1), seg_ref[...]) + EPS)


def _halves_inv_rms(x):
    sq = x * x
    lower = _lane_mask(0, LANES // 2)
    low = jnp.sum(jnp.where(lower, sq, 0.0), axis=-1, keepdims=True)
    high = jnp.sum(jnp.where(lower, 0.0, sq), axis=-1, keepdims=True)
    inv_n = 2.0 / LANES
    return jnp.where(lower, lax.rsqrt(low * inv_n + EPS), lax.rsqrt(high * inv_n + EPS))


def _half_values_with_ones(v, upper):
    lane = lax.broadcasted_iota(jnp.int32, (1, LANES), 1)
    keep = (lane >= LANES // 2) if upper else (lane < LANES // 2)
    ones_lane = 0 if upper else LANES // 2
    return jnp.where(keep, v, (lane == ones_lane).astype(v.dtype)), keep, ones_lane


def _rotate(z, tabs, shift):
    c_ref, u_ref, d_ref = tabs
    return (z * c_ref[...] + pltpu.roll(z, LANES - shift, axis=1) * u_ref[...]
            + pltpu.roll(z, shift, axis=1) * d_ref[...])


def _rope_tables(st, dim, lane_lo, period, gain, post):
    nf = dim // 4
    inv = ROPE_BASE ** (-jnp.arange(nf, dtype=F32) / nf)
    t = jnp.arange(st.seq, dtype=jnp.int32)
    pos = jnp.stack([t // GRID_W, t % GRID_W], axis=-1).astype(F32)
    ang = jnp.broadcast_to(pos[:, :, None, None] * inv, (st.seq, 2, 2, nf)).reshape(st.seq, dim)
    cos, sin = jnp.cos(ang), jnp.sin(ang)
    first_half = (jnp.arange(dim) % (2 * nf)) < nf
    sin_up = jnp.where(first_half, -sin, 0.0)
    sin_dn = jnp.where(first_half, 0.0, sin)

    def place(tab, fill, lane_gain):
        blk = jnp.full((st.seq, period), fill, F32).at[:, lane_lo:lane_lo + dim].set(tab)
        blk = jnp.tile(blk, (1, LANES // period))
        return jnp.concatenate([blk, jnp.full((st.ctx, LANES), fill, F32)], axis=0) * (lane_gain * post)

    return (place(cos, 1.0, gain), place(sin_up, 0.0, jnp.roll(gain, -nf)), place(sin_dn, 0.0, jnp.roll(gain, nf)))


def _qkv_pre_kernel(*refs, rope, seq):
    if rope:
        x_ref, mod_ref, g_ref, w_ref, seg_ref, *tabs, q_ref, k_ref, v_ref = refs
    else:
        x_ref, mod_ref, g_ref, w_ref, gain_ref, q_ref, k_ref, v_ref = refs

    def step(rows, _):
        h = _modulate(x_ref[rows, :], g_ref[...], mod_ref[3:4, :], mod_ref[4:5, :]).astype(BF16)

        def project(t):
            return _dot(h, w_ref[:, t * D_MODEL:(t + 1) * D_MODEL])

        def finish(t, x, dst):
            for c in range(D_MODEL // LANES):
                xb = x[:, c * LANES:(c + 1) * LANES]
                if rope:
                    row_tabs = [tab.at[rows, :] for tab in tabs[3 * t:3 * t + 3]]
                    y = _rotate(xb * _segment_inv_rms(xb, seg_ref), row_tabs, DIFF_DH // 4)
                else:
                    y = xb * _halves_inv_rms(xb) * gain_ref[t:t + 1, :]
                dst[rows, c * LANES:(c + 1) * LANES] = y.astype(BF16)

        q, k = project(0), project(1)
        finish(0, q, q_ref)
        v = project(2)
        finish(1, k, k_ref)
        v_ref[rows, :] = v.astype(BF16)

    _wide_tile_branches(1, seq, step, tile=PROJ_TILE)


def _qkv_pre(st, xs, mod, g, w_qkv, qk_gain, rope):
    tab_spec = pl.BlockSpec((PROJ_TILE, LANES), lambda b, j: (j, 0))
    in_specs = [st.proj_spec(D_MODEL), st.mod_spec(PROJ_TILE), _resident((1, D_MODEL)),
                _resident((D_MODEL, 3 * D_MODEL))]
    args = [xs, mod, g, w_qkv]
    lane_gain = jnp.tile(qk_gain, (1, 2))
    post = (DIFF_DH ** -0.5 * LOG2E, 1.0)
    if rope:
        in_specs += [_resident((2 * LANES, LANES))] + [tab_spec] * 6
        args += [_segment_mean_matrix((DIFF_DH, DIFF_DH))]
        for t in range(2):
            args += list(_rope_tables(st, DIFF_DH, 0, DIFF_DH, lane_gain[t], post[t]))
    else:
        in_specs += [_resident((2, LANES))]
        args += [lane_gain * jnp.array(post, F32)[:, None]]
    shape = jax.ShapeDtypeStruct((st.batch, st.tokens, D_MODEL), BF16)
    return pl.pallas_call(
        functools.partial(_qkv_pre_kernel, rope=rope, seq=st.seq),
        grid=(st.batch, st.proj_steps),
        in_specs=in_specs,
        out_specs=[st.proj_spec(D_MODEL)] * 3,
        out_shape=[shape] * 3,
        compiler_params=_params(2),
        name="qkv_pre",
    )(*args)


DIFF_HEADS_PER_STEP = 8


def _diff_attn_kernel(q_ref, k_ref, v_ref, lam_ref, subg_ref, o_ref, *, seq, lam_init):
    lp = lam_ref[...]
    lam = (jnp.exp(jnp.sum(lp[0:1] * lp[1:2], axis=-1, keepdims=True))
           - jnp.exp(jnp.sum(lp[2:3] * lp[3:4], axis=-1, keepdims=True)) + lam_init)

    def attend(query_rows, key_rows):
        def logits(hd):
            cols = slice(hd * LANES, (hd + 1) * LANES)
            q, k = q_ref[query_rows, cols], k_ref[key_rows, cols]
            return [_dot_t(jnp.where(_lane_mask(c * DIFF_DH, (c + 1) * DIFF_DH), q, jnp.zeros_like(q)), k)
                    for c in range(2)]

        s_next = logits(0)
        for hd in range(DIFF_HEADS_PER_STEP):
            cols = slice(hd * LANES, (hd + 1) * LANES)
            s_cur = s_next
            if hd + 1 < DIFF_HEADS_PER_STEP:
                s_next = logits(hd + 1)
            probs = []
            for s in s_cur:
                p = jnp.exp2(s - jnp.max(s, axis=-1, keepdims=True))
                probs.append((p, 1.0 / jnp.sum(p, axis=-1, keepdims=True)))
            (p0, r0), (p1, r1) = probs
            a = (p0 - p1 * (lam * r1 / r0)).astype(BF16)
            o = _dot(a, v_ref[key_rows, cols]) * r0
            ms = jnp.mean(o * o, axis=-1, keepdims=True)
            o_ref[query_rows, cols] = ((o * lax.rsqrt(ms + EPS)) * subg_ref[...] * (1.0 - lam_init)).astype(BF16)

    _wide_tile_branches(2, seq, attend)


def _diff_attn(st, q, k, v, lam_p, sub_g, lam_init):
    width = DIFF_HEADS_PER_STEP * LANES
    head = lambda b, h, j: (b, j, h)
    keys = lambda b, h, j: (b, 0, h)
    kern = functools.partial(_diff_attn_kernel, seq=st.seq, lam_init=lam_init)
    return pl.pallas_call(
        kern,
        grid=(st.batch, DIFF_HEADS // DIFF_HEADS_PER_STEP, st.attn_steps),
        in_specs=[pl.BlockSpec((None, ATTN_TILE, width), head),
                  pl.BlockSpec((None, st.tokens, width), keys),
                  pl.BlockSpec((None, st.tokens, width), keys),
                  _resident((4, DIFF_DH)), _resident((1, LANES))],
        out_specs=pl.BlockSpec((None, ATTN_TILE, width), head),
        out_shape=jax.ShapeDtypeStruct((st.batch, st.tokens, D_MODEL), BF16),
        compiler_params=_params(3),
        name="diff_attn",
    )(q, k, v, lam_p, sub_g.reshape(1, LANES))


MLA_DOWN = MLA_Q_RANK + MLA_KV_RANK + LANES


def _mla_pre_kernel(x_ref, mod_ref, g_ref, wd_ref, qg_ref, kvg_ref, wuq_ref, wukv_ref, seg_ref, kgain_ref,
                    qc_ref, qu_ref, qd_ref, kc_ref, ku_ref, kd_ref, q_ref, k_ref, v_ref, *, seq):
    def full_norm(z, gain):
        ms = jnp.mean(z * z, axis=-1, keepdims=True)
        return ((z * lax.rsqrt(ms + EPS)) * gain).astype(BF16)

    def step(rows, _):
        q_tabs = [tab.at[rows, :] for tab in (qc_ref, qu_ref, qd_ref)]
        k_tabs = [tab.at[rows, :] for tab in (kc_ref, ku_ref, kd_ref)]
        h = _modulate(x_ref[rows, :], g_ref[...], mod_ref[3:4, :], mod_ref[4:5, :]).astype(BF16)
        down = _dot(h, wd_ref[...])
        q = _dot(full_norm(down[:, :MLA_Q_RANK], qg_ref[...]), wuq_ref[...])
        kv = _dot(full_norm(down[:, MLA_Q_RANK:MLA_Q_RANK + MLA_KV_RANK], kvg_ref[...]), wukv_ref[...])
        kpe = down[:, MLA_Q_RANK + MLA_KV_RANK:]
        kpe = _rotate(kpe * _segment_inv_rms(kpe, seg_ref), k_tabs, MLA_ROPE // 4)
        for hd in range(MLA_HEADS):
            cols = slice(hd * LANES, (hd + 1) * LANES)
            qb = q[:, cols]
            q_ref[rows, cols] = _rotate(qb * _segment_inv_rms(qb, seg_ref), q_tabs, MLA_ROPE // 4).astype(BF16)
            kb = kv[:, cols]
            k_ref[rows, cols] = (kb * _segment_inv_rms(kb, seg_ref) * kgain_ref[...] + kpe).astype(BF16)
        v_ref[rows, :] = kv[:, MLA_HEADS * LANES:].astype(BF16)

    _wide_tile_branches(1, seq, step, tile=PROJ_TILE)


def _mla_pre(st, xs, mod, g, w_down, q_g, kv_g, w_uq, w_ukv, qk_gain):
    tab_spec = pl.BlockSpec((PROJ_TILE, LANES), lambda b, j: (j, 0))
    slots = MLA_HEADS * LANES
    scale = (MLA_NOPE + MLA_ROPE) ** -0.5 * LOG2E
    q_tabs = _rope_tables(st, MLA_ROPE, MLA_NOPE, LANES, qk_gain[0], scale)
    k_tabs = _rope_tables(st, MLA_ROPE, MLA_NOPE, LANES, qk_gain[1], 1.0)
    seg = _segment_mean_matrix((MLA_NOPE, MLA_ROPE, LANES - MLA_NOPE - MLA_ROPE))
    return pl.pallas_call(
        functools.partial(_mla_pre_kernel, seq=st.seq),
        grid=(st.batch, st.proj_steps),
        in_specs=[st.proj_spec(D_MODEL), st.mod_spec(PROJ_TILE), _resident((1, D_MODEL)),
                  _resident((D_MODEL, MLA_DOWN)), _resident((1, MLA_Q_RANK)), _resident((1, MLA_KV_RANK)),
                  _resident((MLA_Q_RANK, slots)), _resident((MLA_KV_RANK, slots + D_MODEL)),
                  _resident((2 * LANES, LANES)), _resident((1, LANES))] + [tab_spec] * 6,
        out_specs=[st.proj_spec(slots), st.proj_spec(slots), st.proj_spec(D_MODEL)],
        out_shape=[jax.ShapeDtypeStruct((st.batch, st.tokens, slots), BF16),
                   jax.ShapeDtypeStruct((st.batch, st.tokens, slots), BF16),
                   jax.ShapeDtypeStruct((st.batch, st.tokens, D_MODEL), BF16)],
        compiler_params=_params(2),
        name="mla_pre",
    )(xs, mod, g, w_down, q_g, kv_g, w_uq, w_ukv, seg, qk_gain[1:2], *q_tabs, *k_tabs)


MLA_PAIRS_PER_STEP = 4


def _mla_attn_kernel(q_ref, k_ref, v_ref, o_ref, *, seq):
    def attend(query_rows, key_rows):
        def logits(head):
            slot = slice(head * LANES, (head + 1) * LANES)
            return _dot_t(q_ref[query_rows, slot], k_ref[key_rows, slot])

        s_next = logits(0)
        for pair in range(MLA_PAIRS_PER_STEP):
            v = v_ref[key_rows, pair * LANES:(pair + 1) * LANES]
            out = None
            for hh in range(2):
                head = 2 * pair + hh
                s = s_next
                if head + 1 < 2 * MLA_PAIRS_PER_STEP:
                    s_next = logits(head + 1)
                p = jnp.exp2(s - jnp.max(s, axis=-1, keepdims=True))
                vh, keep, ones_lane = _half_values_with_ones(v, hh == 1)
                o = _dot(p.astype(BF16), vh)
                o = o * (1.0 / o[:, ones_lane:ones_lane + 1])
                out = o if out is None else jnp.where(keep, o, out)
            o_ref[query_rows, pair * LANES:(pair + 1) * LANES] = out.astype(BF16)

    _wide_tile_branches(2, seq, attend)


def _mla_attn(st, q, k, v):
    slots = 2 * MLA_PAIRS_PER_STEP * LANES
    vals = MLA_PAIRS_PER_STEP * LANES
    kern = functools.partial(_mla_attn_kernel, seq=st.seq)
    return pl.pallas_call(
        kern,
        grid=(st.batch, MLA_HEADS // (2 * MLA_PAIRS_PER_STEP), st.attn_steps),
        in_specs=[pl.BlockSpec((None, ATTN_TILE, slots), lambda b, h, j: (b, j, h)),
                  pl.BlockSpec((None, st.tokens, slots), lambda b, h, j: (b, 0, h)),
                  pl.BlockSpec((None, st.tokens, vals), lambda b, h, j: (b, 0, h))],
        out_specs=pl.BlockSpec((None, ATTN_TILE, vals), lambda b, h, j: (b, j, h)),
        out_shape=jax.ShapeDtypeStruct((st.batch, st.tokens, D_MODEL), BF16),
        compiler_params=_params(3),
        name="mla_attn",
    )(q, k, v)


NA_QROWS = TOKEN_TILE // GRID_W
NA_WROWS = NA_QROWS + NA_KH
NA_PAIRS = NA_WROWS // 2
NA_BIAS_LO = -(NA_QROWS - 1) - (NA_WROWS - NA_QROWS - NA_KH // 2) + NA_KH - 1
NA_BIAS_N = (NA_WROWS - 2) + NA_KH - 1 - NA_BIAS_LO + 1


def _na_attn_kernel(q_ref, k_ref, v_ref, bias_ref, o_ref, cap_ref, *, rows, seq):
    r0 = pl.program_id(1) * NA_QROWS
    w0 = jnp.clip(r0 - NA_KH // 2, 0, rows - NA_WROWS)
    start = pl.multiple_of(w0 * GRID_W, GRID_W)
    n_loc = NA_WROWS * GRID_W
    qidx = lax.broadcasted_iota(jnp.int32, (TOKEN_TILE, n_loc), 0)
    kidx = lax.broadcasted_iota(jnp.int32, (TOKEN_TILE, n_loc), 1)
    q_row, q_col = r0 + qidx // GRID_W, qidx % GRID_W
    k_row, k_col = w0 + kidx // GRID_W, kidx % GRID_W
    row_start = jnp.clip(q_row - NA_KH // 2, 0, rows - NA_KH)
    col_start = jnp.clip(q_col - NA_KW // 2, 0, GRID_W - NA_KW)
    valid = ((k_row >= row_start) & (k_row < row_start + NA_KH)
             & (k_col >= col_start) & (k_col < col_start + NA_KW))
    cap_ref[...] = jnp.where(valid, jnp.inf, NEG_INF)
    base = w0 - r0 + NA_KH - 1 - NA_BIAS_LO

    def logits(head):
        pair, hh = divmod(head, 2)
        cols = slice(pair * LANES, (pair + 1) * LANES)
        q = q_ref[:, cols]
        qh = jnp.where(_lane_mask(hh * NA_DH, (hh + 1) * NA_DH), q, jnp.zeros_like(q))
        return _dot_t(qh, k_ref[pl.ds(start, n_loc), cols]), _dot_t(qh, k_ref[seq:, cols])

    s_next = logits(0)
    for pair in range(NA_HEADS // 2):
        cols = slice(pair * LANES, (pair + 1) * LANES)
        v_loc, v_ctx = v_ref[pl.ds(start, n_loc), cols], v_ref[seq:, cols]
        out = None
        for hh in range(2):
            head = 2 * pair + hh
            s_loc, s_ctx = s_next
            if head + 1 < NA_HEADS:
                s_next = logits(head + 1)
            bias = jnp.concatenate(
                [jnp.concatenate([bias_ref[head, base + 2 * p - a] for p in range(NA_PAIRS)], axis=1)
                 for a in range(NA_QROWS)], axis=0)
            s_loc = jnp.minimum(s_loc + bias, cap_ref[...])
            m = jnp.maximum(jnp.max(s_loc, axis=-1, keepdims=True), jnp.max(s_ctx, axis=-1, keepdims=True))
            p_loc, p_ctx = jnp.exp2(s_loc - m), jnp.exp2(s_ctx - m)
            vh_loc, keep, ones_lane = _half_values_with_ones(v_loc, hh == 1)
            vh_ctx, _, _ = _half_values_with_ones(v_ctx, hh == 1)
            o = _dot(p_loc.astype(BF16), vh_loc) + _dot(p_ctx.astype(BF16), vh_ctx)
            o = o * (1.0 / o[:, ones_lane:ones_lane + 1])
            out = o if out is None else jnp.where(keep, o, out)
        o_ref[:, cols] = out.astype(BF16)


def _na_bias_table(rpb):
    n_rel, n_col = 2 * NA_KH - 1, 2 * NA_KW - 1
    left = GRID_W - NA_KW
    v = jnp.pad(rpb * LOG2E, ((0, 0), (0, 0), (left, 2 * GRID_W - left - n_col)))
    v = jnp.tile(v, (1, 1, GRID_W))[:, :, :GRID_W * (2 * GRID_W - 1)]
    toep = v.reshape(NA_HEADS, n_rel, GRID_W, 2 * GRID_W - 1)[..., GRID_W - 1:]
    toep = jnp.pad(toep, ((0, 0), (-NA_BIAS_LO, NA_BIAS_LO + NA_BIAS_N + 1 - n_rel), (0, 0), (0, 0)))
    return jnp.concatenate([toep[:, :-1], toep[:, 1:]], axis=-1)


def _na_attn(st, q, k, v, bias_tab):
    rows = st.seq // GRID_W
    assert rows >= NA_WROWS and rows % NA_QROWS == 0
    kern = functools.partial(_na_attn_kernel, rows=rows, seq=st.seq)
    return pl.pallas_call(
        kern,
        grid=(st.batch, st.lat_tiles),
        in_specs=[st.tile_spec(D_MODEL),
                  pl.BlockSpec((None, st.tokens, D_MODEL), lambda b, j: (b, 0, 0)),
                  pl.BlockSpec((None, st.tokens, D_MODEL), lambda b, j: (b, 0, 0)),
                  _resident((NA_HEADS, NA_BIAS_N, GRID_W, 2 * GRID_W))],
        out_specs=st.tile_spec(D_MODEL),
        out_shape=jax.ShapeDtypeStruct((st.batch, st.seq, D_MODEL), BF16),
        scratch_shapes=[pltpu.VMEM((TOKEN_TILE, NA_WROWS * GRID_W), F32)],
        compiler_params=_params(2),
        name="na_attn",
    )(q, k, v, bias_tab)


def _slot_columns(w, width):
    kdim = w.shape[0]
    w = w.reshape(kdim, -1, width)
    return jnp.pad(w, ((0, 0), (0, 0), (0, LANES - width))).reshape(kdim, -1)


def _slot_gain(g):
    return jnp.pad(g, ((0, 0), (0, LANES - g.shape[1])))


def kernel(x, c, ctx, c_ctx, ada_w, ada_b, norm_g, ffn_w_in, ffn_w_out, pool_w, pool_b, pool_scale,
           diff_w_qkv, diff_qk_g, diff_lambda, diff_sub_g, diff_w_o,
           mla_w_dq, mla_q_g, mla_w_uq, mla_w_dkv, mla_kv_g, mla_w_ukv, mla_qk_g, mla_w_o,
           na_w_qkv, na_qk_g, na_rpb, na_w_o):
    assert DEPTH == N_MIXERS, "one layer per mixer: the context stream ends after the last softmax mixer's keys"
    batch, seq, _ = x.shape
    st = _Stream(batch, seq, ctx.shape[1])
    rows = -(-(batch + 1) // BF16_SUBLANES) * BF16_SUBLANES
    cc = jnp.zeros((rows, D_MODEL), F32).at[:batch].set(c).at[batch].set(c_ctx)
    mods = _adaln(cc, ada_w, ada_b)

    w_in = ffn_w_in.astype(BF16)
    w_out = ffn_w_out.astype(BF16)
    gains = norm_g.reshape(DEPTH, 3, 1, D_MODEL)

    for i in range(DEPTH):
        mod = mods[i]
        if i == 0:
            xs = _ffn_join(st, x, ctx, mod, gains[i, 0], w_in, w_out)
            xs = _pool(st, xs, mod, gains[i, 1], pool_w[0].astype(BF16), pool_b[0].reshape(1, D_MODEL),
                       pool_scale[0].reshape(1, D_MODEL))
            xs = _ffn(st, xs, mod, 2, gains[i, 2], w_in, w_out, i)
            continue
        xs = _ffn(st, xs, mod, 0, gains[i, 0], w_in, w_out, i)
        if i == 1:
            q, k, v = _qkv_pre(st, xs, mod, gains[i, 1], diff_w_qkv[0].astype(BF16), diff_qk_g[0], True)
            lam_init = 0.8 - 0.6 * math.exp(-0.3 * i)
            attn = _diff_attn(st, q, k, v, diff_lambda[0], diff_sub_g[0], lam_init)
            w_o = diff_w_o[0]
        elif i == 2:
            kpe_cols = jnp.pad(mla_w_dkv[0][:, MLA_KV_RANK:], ((0, 0), (MLA_NOPE, LANES - MLA_NOPE - MLA_ROPE)))
            w_down = jnp.concatenate([mla_w_dq[0], mla_w_dkv[0][:, :MLA_KV_RANK], kpe_cols], axis=1)
            ukv = mla_w_ukv[0].reshape(MLA_KV_RANK, MLA_HEADS, MLA_NOPE + MLA_V)
            w_ukv = jnp.concatenate([_slot_columns(ukv[:, :, :MLA_NOPE].reshape(MLA_KV_RANK, -1), MLA_NOPE),
                                     ukv[:, :, MLA_NOPE:].reshape(MLA_KV_RANK, -1)], axis=1)
            q, k, v = _mla_pre(st, xs, mod, gains[i, 1], w_down.astype(BF16),
                               mla_q_g[0].reshape(1, -1), mla_kv_g[0].reshape(1, -1),
                               _slot_columns(mla_w_uq[0], MLA_NOPE + MLA_ROPE).astype(BF16),
                               w_ukv.astype(BF16), _slot_gain(mla_qk_g[0]))
            attn = _mla_attn(st, q, k, v)
            w_o = mla_w_o[0]
        else:
            q, k, v = _qkv_pre(st, xs, mod, gains[i, 1], na_w_qkv[0].astype(BF16), na_qk_g[0], False)
            attn = _na_attn(st, q, k, v, _na_bias_table(na_rpb[0]))
            w_o = na_w_o[0]
        xs = _mixout_ffn(st, xs, attn, mod, gains[i, 2], w_o.astype(BF16), w_in, w_out, i,
                         latent_only=(i == DEPTH - 1))
    return xs
```

```python
import functools
import math

import jax
import jax.numpy as jnp
from jax import lax
from jax.experimental import pallas as pl
from jax.experimental.pallas import tpu as pltpu

D_MODEL = 1024
DEPTH = 4
GRID_W = 64
N_MIXERS = 4
N_MOD = 9
EPS = 1e-6
LOG2E = math.log2(math.e)
ROPE_BASE = 10000.0
NEG_INF = -1e30
D_FF = 2816
POOL_WINDOWS = (2, 4, 8, 16)
POOL_G = D_MODEL // len(POOL_WINDOWS)
DIFF_HEADS = 8
DIFF_DH = 64
MLA_HEADS = 16
MLA_Q_RANK = 384
MLA_KV_RANK = 256
MLA_NOPE = 64
MLA_ROPE = 32
MLA_V = 64
NA_HEADS = 16
NA_DH = 64
NA_KH = 8
NA_KW = 16

LANES = 128
HALO = 8
BF16_SUBLANES = 16
TOKEN_TILE = 256
ATTN_TILE = 512
FFN_TILE = 1024
PROJ_TILE = 512
VMEM_LIMIT = 56 * 2**20

F32 = jnp.float32
BF16 = jnp.bfloat16


def _params(n_axes):
    return pltpu.CompilerParams(dimension_semantics=("arbitrary",) * n_axes,
                                vmem_limit_bytes=VMEM_LIMIT)


def _resident(shape, stack_index=()):
    index = tuple(stack_index) + (0,) * len(shape)
    return pl.BlockSpec((None,) * len(stack_index) + tuple(shape), lambda *_: index, pipeline_mode=pl.Buffered(1))


def _dot(a, b):
    return jnp.dot(a, b, preferred_element_type=F32)


def _dot_t(a, b):
    return lax.dot_general(a, b, (((1,), (1,)), ((), ())), preferred_element_type=F32)


def _modulate(x, g, shift, scale):
    ms = jnp.mean(x * x, axis=-1, keepdims=True)
    return (x * lax.rsqrt(ms + EPS)) * (g * (1.0 + scale)) + shift


def _silu(x):
    return x * (1.0 / (1.0 + jnp.exp(-x)))


def _adaln_kernel(c_ref, w_ref, b_ref, o_ref):
    s = _silu(c_ref[...]).astype(BF16)
    o_ref[...] = _dot(s, w_ref[...].astype(BF16)) + b_ref[...]


def _adaln(cc, ada_w, ada_b):
    rows = cc.shape[0]
    out = pl.pallas_call(
        _adaln_kernel,
        grid=(DEPTH, N_MOD),
        in_specs=[pl.BlockSpec((rows, D_MODEL), lambda i, n: (0, 0)),
                  pl.BlockSpec((None, D_MODEL, D_MODEL), lambda i, n: (i, 0, n)),
                  pl.BlockSpec((None, 1, D_MODEL), lambda i, n: (i, 0, n))],
        out_specs=pl.BlockSpec((None, rows, D_MODEL), lambda i, n: (i, 0, n)),
        out_shape=jax.ShapeDtypeStruct((DEPTH, rows, N_MOD * D_MODEL), F32),
        compiler_params=_params(2),
        name="adaln",
    )(cc, ada_w, ada_b.reshape(DEPTH, 1, N_MOD * D_MODEL))
    return out.reshape(DEPTH, rows, N_MOD, D_MODEL)


def _ffn_core(x, mod_ref, k, g_ref, win_ref, wout_ref):
    shift = mod_ref[3 * k:3 * k + 1, :]
    scale = mod_ref[3 * k + 1:3 * k + 2, :]
    gate = mod_ref[3 * k + 2:3 * k + 3, :]
    parts = [x[r:r + TOKEN_TILE] for r in range(0, x.shape[0], TOKEN_TILE)]
    ups = [_dot(_modulate(p, g_ref[...], shift, scale).astype(BF16), win_ref[...]) for p in parts]
    downs = [_dot((_silu(u[:, :D_FF]) * u[:, D_FF:]).astype(BF16), wout_ref[...]) for u in ups]
    outs = [p + (0.5 * gate) * y for p, y in zip(parts, downs)]
    return outs[0] if len(outs) == 1 else jnp.concatenate(outs, axis=0)


def _wide_tile_branches(axis, seq, body, with_ctx=True, tile=ATTN_TILE):
    j = pl.program_id(axis)
    lat_steps = seq // tile

    @pl.when(j < lat_steps)
    def _():
        body(slice(None), slice(None))

    if with_ctx:
        @pl.when(j >= lat_steps)
        def _():
            body(slice(0, TOKEN_TILE), slice(seq, None))


def _ffn_kernel(x_ref, mod_ref, g_ref, win_ref, wout_ref, o_ref, *, k, seq):
    def step(rows, _):
        o_ref[rows, :] = _ffn_core(x_ref[rows, :], mod_ref, k, g_ref, win_ref, wout_ref)

    _wide_tile_branches(1, seq, step, tile=FFN_TILE)


def _ffn_join_kernel(x_ref, c_ref, mod_ref, g_ref, win_ref, wout_ref, o_ref, *, seq):
    j = pl.program_id(1)

    @pl.when(j < seq // FFN_TILE)
    def _():
        o_ref[...] = _ffn_core(x_ref[...], mod_ref, 0, g_ref, win_ref, wout_ref)

    @pl.when(j >= seq // FFN_TILE)
    def _():
        o_ref[0:TOKEN_TILE, :] = _ffn_core(c_ref[...], mod_ref, 0, g_ref, win_ref, wout_ref)


def _mixout_ffn_kernel(x_ref, a_ref, mod_ref, g_ref, wo_ref, win_ref, wout_ref, o_ref, *, seq, with_ctx):
    def step(rows, _):
        x = x_ref[rows, :] + mod_ref[5:6, :] * _dot(a_ref[rows, :], wo_ref[...])
        o_ref[rows, :] = _ffn_core(x, mod_ref, 2, g_ref, win_ref, wout_ref)

    _wide_tile_branches(1, seq, step, with_ctx, FFN_TILE)


class _Stream:
    def __init__(self, batch, seq, ctx):
        assert seq % TOKEN_TILE == 0 and ctx == TOKEN_TILE and seq % GRID_W == 0
        self.batch, self.seq, self.ctx = batch, seq, ctx
        self.tokens = seq + ctx
        self.lat_tiles = seq // TOKEN_TILE
        self.tiles = self.tokens // TOKEN_TILE
        assert seq % ATTN_TILE == 0 and ATTN_TILE >= ctx
        self.attn_steps = seq // ATTN_TILE + 1
        assert seq % FFN_TILE == 0 and seq % PROJ_TILE == 0
        self.ffn_steps = seq // FFN_TILE + 1
        self.proj_steps = seq // PROJ_TILE + 1

    def tile_spec(self, width):
        return pl.BlockSpec((None, TOKEN_TILE, width), lambda b, j: (b, j, 0))

    def ffn_spec(self, width):
        return pl.BlockSpec((None, FFN_TILE, width), lambda b, j: (b, j, 0))

    def proj_spec(self, width):
        return pl.BlockSpec((None, PROJ_TILE, width), lambda b, j: (b, j, 0))

    def mod_spec(self, rows=TOKEN_TILE):
        lat, ctx_row = self.seq // rows, self.batch
        return pl.BlockSpec((None, N_MOD, D_MODEL),
                            lambda b, j: (jnp.where(j < lat, b, ctx_row), 0, 0))


def _ffn_weight_specs(layer, half):
    return [_resident((D_MODEL, 2 * D_FF), (layer, half)), _resident((D_FF, D_MODEL), (layer, half))]


def _ffn(st, xs, mod, k, g, w_in, w_out, layer):
    return pl.pallas_call(
        functools.partial(_ffn_kernel, k=k, seq=st.seq),
        grid=(st.batch, st.ffn_steps),
        in_specs=[st.ffn_spec(D_MODEL), st.mod_spec(FFN_TILE), _resident((1, D_MODEL))]
        + _ffn_weight_specs(layer, k // 2),
        out_specs=st.ffn_spec(D_MODEL),
        out_shape=jax.ShapeDtypeStruct((st.batch, st.tokens, D_MODEL), F32),
        compiler_params=_params(2),
        name="ffn",
    )(xs, mod, g, w_in, w_out)


def _ffn_join(st, x, ctx, mod, g, w_in, w_out):
    last_lat = st.ffn_steps - 2
    return pl.pallas_call(
        functools.partial(_ffn_join_kernel, seq=st.seq),
        grid=(st.batch, st.ffn_steps),
        in_specs=[pl.BlockSpec((None, FFN_TILE, D_MODEL), lambda b, j: (b, jnp.minimum(j, last_lat), 0)),
                  pl.BlockSpec((None, st.ctx, D_MODEL), lambda b, j: (b, 0, 0)),
                  st.mod_spec(FFN_TILE), _resident((1, D_MODEL))] + _ffn_weight_specs(0, 0),
        out_specs=st.ffn_spec(D_MODEL),
        out_shape=jax.ShapeDtypeStruct((st.batch, st.tokens, D_MODEL), F32),
        compiler_params=_params(2),
        name="ffn_join",
    )(x, ctx, mod, g, w_in, w_out)


def _mixout_ffn(st, xs, attn, mod, g, w_o, w_in, w_out, layer, latent_only):
    steps = st.ffn_steps - 1 if latent_only else st.ffn_steps
    tokens = st.seq if latent_only else st.tokens
    return pl.pallas_call(
        functools.partial(_mixout_ffn_kernel, seq=st.seq, with_ctx=not latent_only),
        grid=(st.batch, steps),
        in_specs=[st.ffn_spec(D_MODEL), st.ffn_spec(D_MODEL), st.mod_spec(FFN_TILE),
                  _resident((1, D_MODEL)), _resident((D_MODEL, D_MODEL))] + _ffn_weight_specs(layer, 1),
        out_specs=st.ffn_spec(D_MODEL),
        out_shape=jax.ShapeDtypeStruct((st.batch, tokens, D_MODEL), F32),
        compiler_params=_params(2),
        name="mixout_ffn",
    )(xs, attn, mod, g, w_o, w_in, w_out)


def _window_sum(e, start, win, rows):
    steps = [win >> (i + 1) for i in range(win.bit_length() - 1)]
    cur, base = e, start
    for idx, shift in enumerate(steps):
        n = rows + HALO * (len(steps) - 1 - idx)
        cur = cur[base:base + n] + cur[base + shift:base + shift + n]
        base = 0
    return cur


def _pool_kernel(x_ref, xp_ref, xn_ref, mod_ref, g_ref, w_ref, b_ref, s_ref, o_ref, hext_ref,
                 *, lat_tiles, tiles, seq, ctx):
    j = pl.program_id(1)
    g, shift, scale = g_ref[...], mod_ref[3:4, :], mod_ref[4:5, :]
    x = x_ref[...]
    h = _modulate(x, g, shift, scale)
    has_prev = jnp.logical_and(j != 0, j != lat_tiles)
    has_next = jnp.logical_and(j != lat_tiles - 1, j != tiles - 1)
    hext_ref[0:HALO, :] = jnp.where(has_prev, _modulate(xp_ref[...], g, shift, scale), 0.0)
    hext_ref[HALO:HALO + TOKEN_TILE, :] = h
    hext_ref[HALO + TOKEN_TILE:2 * HALO + TOKEN_TILE, :] = jnp.where(
        has_next, _modulate(xn_ref[...], g, shift, scale), 0.0)
    hext_ref[2 * HALO + TOKEN_TILE:, :] = jnp.zeros((2 * HALO, D_MODEL), F32)
    in_lat = j < lat_tiles
    t = (j - jnp.where(in_lat, 0, lat_tiles)) * TOKEN_TILE + lax.broadcasted_iota(jnp.int32, (TOKEN_TILE, 1), 0)
    n = jnp.where(in_lat, seq, ctx)
    for gi, win in enumerate(POOL_WINDOWS):
        cols = slice(gi * POOL_G, (gi + 1) * POOL_G)
        acc = _window_sum(hext_ref[:, cols], HALO - win // 2, win, TOKEN_TILE)
        lo = jnp.maximum(t - win // 2, 0)
        hi = jnp.minimum(t - win // 2 + win, n)
        mean = acc / (hi - lo).astype(F32)
        d = (mean - h[:, cols]).astype(BF16)
        y = (_dot(d, w_ref[gi]) + b_ref[:, cols]) * s_ref[:, cols]
        o_ref[:, cols] = x[:, cols] + mod_ref[5:6, cols] * y


def _pool(st, xs, mod, g, w, b, scale):
    per_tile = TOKEN_TILE // HALO
    last = st.tokens // HALO - 1
    kern = functools.partial(_pool_kernel, lat_tiles=st.lat_tiles, tiles=st.tiles, seq=st.seq, ctx=st.ctx)
    return pl.pallas_call(
        kern,
        grid=(st.batch, st.tiles),
        in_specs=[st.tile_spec(D_MODEL),
                  pl.BlockSpec((None, HALO, D_MODEL), lambda b_, j: (b_, jnp.maximum(j * per_tile - 1, 0), 0)),
                  pl.BlockSpec((None, HALO, D_MODEL), lambda b_, j: (b_, jnp.minimum((j + 1) * per_tile, last), 0)),
                  st.mod_spec(), _resident((1, D_MODEL)),
                  _resident((len(POOL_WINDOWS), POOL_G, POOL_G)),
                  _resident((1, D_MODEL)), _resident((1, D_MODEL))],
        out_specs=st.tile_spec(D_MODEL),
        out_shape=jax.ShapeDtypeStruct((st.batch, st.tokens, D_MODEL), F32),
        scratch_shapes=[pltpu.VMEM((TOKEN_TILE + 4 * HALO, D_MODEL), F32)],
        compiler_params=_params(2),
        name="pool",
    )(xs, xs, xs, mod, g, w, b, scale)


def _lane_mask(lo, hi):
    lane = lax.broadcasted_iota(jnp.int32, (1, LANES), 1)
    return jnp.logical_and(lane >= lo, lane < hi)


def _segment_mean_matrix(lengths):
    assert sum(lengths) == LANES and all(n & (n - 1) == 0 for n in lengths)
    ids = jnp.repeat(jnp.arange(len(lengths)), jnp.array(lengths), total_repeat_length=LANES)
    inv = 1.0 / jnp.array(lengths, F32)[ids]
    m = jnp.where(ids[:, None] == ids[None, :], inv[None, :], 0.0)
    return jnp.tile(m, (2, 1)).astype(BF16)


def _segment_inv_rms(x, seg_ref):
    sq = x * x
    hi = sq.astype(BF16)
    lo = (sq - hi.astype(F32)).astype(BF16)
    return lax.rsqrt(_dot(jnp.concatenate([hi, lo], axis=1), seg_ref[...]) + EPS)


def _halves_inv_rms(x):
    sq = x * x
    lower = _lane_mask(0, LANES // 2)
    low = jnp.sum(jnp.where(lower, sq, 0.0), axis=-1, keepdims=True)
    high = jnp.sum(jnp.where(lower, 0.0, sq), axis=-1, keepdims=True)
    inv_n = 2.0 / LANES
    return jnp.where(lower, lax.rsqrt(low * inv_n + EPS), lax.rsqrt(high * inv_n + EPS))


def _half_values_with_ones(v, upper):
    lane = lax.broadcasted_iota(jnp.int32, (1, LANES), 1)
    keep = (lane >= LANES // 2) if upper else (lane < LANES // 2)
    ones_lane = 0 if upper else LANES // 2
    return jnp.where(keep, v, (lane == ones_lane).astype(v.dtype)), keep, ones_lane


def _rotate(z, tabs, shift):
    c_ref, u_ref, d_ref = tabs
    return (z * c_ref[...] + pltpu.roll(z, LANES - shift, axis=1) * u_ref[...]
            + pltpu.roll(z, shift, axis=1) * d_ref[...])


def _rope_tables(st, dim, lane_lo, period, gain, post):
    nf = dim // 4
    inv = ROPE_BASE ** (-jnp.arange(nf, dtype=F32) / nf)
    t = jnp.arange(st.seq, dtype=jnp.int32)
    pos = jnp.stack([t // GRID_W, t % GRID_W], axis=-1).astype(F32)
    ang = jnp.broadcast_to(pos[:, :, None, None] * inv, (st.seq, 2, 2, nf)).reshape(st.seq, dim)
    cos, sin = jnp.cos(ang), jnp.sin(ang)
    first_half = (jnp.arange(dim) % (2 * nf)) < nf
    sin_up = jnp.where(first_half, -sin, 0.0)
    sin_dn = jnp.where(first_half, 0.0, sin)

    def place(tab, fill, lane_gain):
        blk = jnp.full((st.seq, period), fill, F32).at[:, lane_lo:lane_lo + dim].set(tab)
        blk = jnp.tile(blk, (1, LANES // period))
        return jnp.concatenate([blk, jnp.full((st.ctx, LANES), fill, F32)], axis=0) * (lane_gain * post)

    return (place(cos, 1.0, gain), place(sin_up, 0.0, jnp.roll(gain, -nf)), place(sin_dn, 0.0, jnp.roll(gain, nf)))


def _qkv_pre_kernel(*refs, rope, seq):
    if rope:
        x_ref, mod_ref, g_ref, w_ref, seg_ref, *tabs, q_ref, k_ref, v_ref = refs
    else:
        x_ref, mod_ref, g_ref, w_ref, gain_ref, q_ref, k_ref, v_ref = refs

    def step(rows, _):
        h = _modulate(x_ref[rows, :], g_ref[...], mod_ref[3:4, :], mod_ref[4:5, :]).astype(BF16)

        def project(t):
            return _dot(h, w_ref[:, t * D_MODEL:(t + 1) * D_MODEL])

        def finish(t, x, dst):
            for c in range(D_MODEL // LANES):
                xb = x[:, c * LANES:(c + 1) * LANES]
                if rope:
                    row_tabs = [tab.at[rows, :] for tab in tabs[3 * t:3 * t + 3]]
                    y = _rotate(xb * _segment_inv_rms(xb, seg_ref), row_tabs, DIFF_DH // 4)
                else:
                    y = xb * _halves_inv_rms(xb) * gain_ref[t:t + 1, :]
                dst[rows, c * LANES:(c + 1) * LANES] = y.astype(BF16)

        q, k = project(0), project(1)
        finish(0, q, q_ref)
        v = project(2)
        finish(1, k, k_ref)
        v_ref[rows, :] = v.astype(BF16)

    _wide_tile_branches(1, seq, step, tile=PROJ_TILE)


def _qkv_pre(st, xs, mod, g, w_qkv, qk_gain, rope):
    tab_spec = pl.BlockSpec((PROJ_TILE, LANES), lambda b, j: (j, 0))
    in_specs = [st.proj_spec(D_MODEL), st.mod_spec(PROJ_TILE), _resident((1, D_MODEL)),
                _resident((D_MODEL, 3 * D_MODEL))]
    args = [xs, mod, g, w_qkv]
    lane_gain = jnp.tile(qk_gain, (1, 2))
    post = (DIFF_DH ** -0.5 * LOG2E, 1.0)
    if rope:
        in_specs += [_resident((2 * LANES, LANES))] + [tab_spec] * 6
        args += [_segment_mean_matrix((DIFF_DH, DIFF_DH))]
        for t in range(2):
            args += list(_rope_tables(st, DIFF_DH, 0, DIFF_DH, lane_gain[t], post[t]))
    else:
        in_specs += [_resident((2, LANES))]
        args += [lane_gain * jnp.array(post, F32)[:, None]]
    shape = jax.ShapeDtypeStruct((st.batch, st.tokens, D_MODEL), BF16)
    return pl.pallas_call(
        functools.partial(_qkv_pre_kernel, rope=rope, seq=st.seq),
        grid=(st.batch, st.proj_steps),
        in_specs=in_specs,
        out_specs=[st.proj_spec(D_MODEL)] * 3,
        out_shape=[shape] * 3,
        compiler_params=_params(2),
        name="qkv_pre",
    )(*args)


DIFF_HEADS_PER_STEP = 8


def _diff_attn_kernel(q_ref, k_ref, v_ref, lam_ref, subg_ref, o_ref, *, seq, lam_init):
    lp = lam_ref[...]
    lam = (jnp.exp(jnp.sum(lp[0:1] * lp[1:2], axis=-1, keepdims=True))
           - jnp.exp(jnp.sum(lp[2:3] * lp[3:4], axis=-1, keepdims=True)) + lam_init)

    def attend(query_rows, key_rows):
        def logits(hd):
            cols = slice(hd * LANES, (hd + 1) * LANES)
            q, k = q_ref[query_rows, cols], k_ref[key_rows, cols]
            return [_dot_t(jnp.where(_lane_mask(c * DIFF_DH, (c + 1) * DIFF_DH), q, jnp.zeros_like(q)), k)
                    for c in range(2)]

        def mix(s_pair):
            probs = []
            for s in s_pair:
                p = jnp.exp2(s - jnp.max(s, axis=-1, keepdims=True))
                probs.append((p, 1.0 / jnp.sum(p, axis=-1, keepdims=True)))
            (p0, r0), (p1, r1) = probs
            return (p0 - p1 * (lam * r1 / r0)).astype(BF16), r0

        def finish(hd, a, r0):
            cols = slice(hd * LANES, (hd + 1) * LANES)
            o = _dot(a, v_ref[key_rows, cols]) * r0
            ms = jnp.mean(o * o, axis=-1, keepdims=True)
            o_ref[query_rows, cols] = ((o * lax.rsqrt(ms + EPS)) * subg_ref[...] * (1.0 - lam_init)).astype(BF16)

        heads = range(DIFF_HEADS_PER_STEP)
        if key_rows.start is not None:
            mixed = [mix(s) for s in [logits(hd) for hd in heads]]
            for hd in heads:
                finish(hd, *mixed[hd])
            return
        s_next = logits(0)
        for hd in heads:
            s_cur = s_next
            if hd + 1 < DIFF_HEADS_PER_STEP:
                s_next = logits(hd + 1)
            finish(hd, *mix(s_cur))

    _wide_tile_branches(2, seq, attend)


def _diff_attn(st, q, k, v, lam_p, sub_g, lam_init):
    width = DIFF_HEADS_PER_STEP * LANES
    head = lambda b, h, j: (b, j, h)
    keys = lambda b, h, j: (b, 0, h)
    kern = functools.partial(_diff_attn_kernel, seq=st.seq, lam_init=lam_init)
    return pl.pallas_call(
        kern,
        grid=(st.batch, DIFF_HEADS // DIFF_HEADS_PER_STEP, st.attn_steps),
        in_specs=[pl.BlockSpec((None, ATTN_TILE, width), head),
                  pl.BlockSpec((None, st.tokens, width), keys),
                  pl.BlockSpec((None, st.tokens, width), keys),
                  _resident((4, DIFF_DH)), _resident((1, LANES))],
        out_specs=pl.BlockSpec((None, ATTN_TILE, width), head),
        out_shape=jax.ShapeDtypeStruct((st.batch, st.tokens, D_MODEL), BF16),
        compiler_params=_params(3),
        name="diff_attn",
    )(q, k, v, lam_p, sub_g.reshape(1, LANES))


MLA_DOWN = MLA_Q_RANK + MLA_KV_RANK + LANES


def _mla_pre_kernel(x_ref, mod_ref, g_ref, wd_ref, qg_ref, kvg_ref, wuq_ref, wukv_ref, seg_ref, kgain_ref,
                    qc_ref, qu_ref, qd_ref, kc_ref, ku_ref, kd_ref, q_ref, k_ref, v_ref, *, seq):
    def full_norm(z, gain):
        ms = jnp.mean(z * z, axis=-1, keepdims=True)
        return ((z * lax.rsqrt(ms + EPS)) * gain).astype(BF16)

    def step(rows, _):
        q_tabs = [tab.at[rows, :] for tab in (qc_ref, qu_ref, qd_ref)]
        k_tabs = [tab.at[rows, :] for tab in (kc_ref, ku_ref, kd_ref)]
        h = _modulate(x_ref[rows, :], g_ref[...], mod_ref[3:4, :], mod_ref[4:5, :]).astype(BF16)
        down = _dot(h, wd_ref[...])
        q = _dot(full_norm(down[:, :MLA_Q_RANK], qg_ref[...]), wuq_ref[...])
        kv = _dot(full_norm(down[:, MLA_Q_RANK:MLA_Q_RANK + MLA_KV_RANK], kvg_ref[...]), wukv_ref[...])
        kpe = down[:, MLA_Q_RANK + MLA_KV_RANK:]
        kpe = _rotate(kpe * _segment_inv_rms(kpe, seg_ref), k_tabs, MLA_ROPE // 4)
        for hd in range(MLA_HEADS):
            cols = slice(hd * LANES, (hd + 1) * LANES)
            qb = q[:, cols]
            q_ref[rows, cols] = _rotate(qb * _segment_inv_rms(qb, seg_ref), q_tabs, MLA_ROPE // 4).astype(BF16)
            kb = kv[:, cols]
            k_ref[rows, cols] = (kb * _segment_inv_rms(kb, seg_ref) * kgain_ref[...] + kpe).astype(BF16)
        v_ref[rows, :] = kv[:, MLA_HEADS * LANES:].astype(BF16)

    _wide_tile_branches(1, seq, step, tile=PROJ_TILE)


def _mla_pre(st, xs, mod, g, w_down, q_g, kv_g, w_uq, w_ukv, qk_gain):
    tab_spec = pl.BlockSpec((PROJ_TILE, LANES), lambda b, j: (j, 0))
    slots = MLA_HEADS * LANES
    scale = (MLA_NOPE + MLA_ROPE) ** -0.5 * LOG2E
    q_tabs = _rope_tables(st, MLA_ROPE, MLA_NOPE, LANES, qk_gain[0], scale)
    k_tabs = _rope_tables(st, MLA_ROPE, MLA_NOPE, LANES, qk_gain[1], 1.0)
    seg = _segment_mean_matrix((MLA_NOPE, MLA_ROPE, LANES - MLA_NOPE - MLA_ROPE))
    return pl.pallas_call(
        functools.partial(_mla_pre_kernel, seq=st.seq),
        grid=(st.batch, st.proj_steps),
        in_specs=[st.proj_spec(D_MODEL), st.mod_spec(PROJ_TILE), _resident((1, D_MODEL)),
                  _resident((D_MODEL, MLA_DOWN)), _resident((1, MLA_Q_RANK)), _resident((1, MLA_KV_RANK)),
                  _resident((MLA_Q_RANK, slots)), _resident((MLA_KV_RANK, slots + D_MODEL)),
                  _resident((2 * LANES, LANES)), _resident((1, LANES))] + [tab_spec] * 6,
        out_specs=[st.proj_spec(slots), st.proj_spec(slots), st.proj_spec(D_MODEL)],
        out_shape=[jax.ShapeDtypeStruct((st.batch, st.tokens, slots), BF16),
                   jax.ShapeDtypeStruct((st.batch, st.tokens, slots), BF16),
                   jax.ShapeDtypeStruct((st.batch, st.tokens, D_MODEL), BF16)],
        compiler_params=_params(2),
        name="mla_pre",
    )(xs, mod, g, w_down, q_g, kv_g, w_uq, w_ukv, seg, qk_gain[1:2], *q_tabs, *k_tabs)


MLA_PAIRS_PER_STEP = 4


def _mla_attn_kernel(q_ref, k_ref, v_ref, o_ref, *, seq):
    def attend(query_rows, key_rows):
        def logits(head):
            slot = slice(head * LANES, (head + 1) * LANES)
            return _dot_t(q_ref[query_rows, slot], k_ref[key_rows, slot])

        def weights(s):
            return jnp.exp2(s - jnp.max(s, axis=-1, keepdims=True)).astype(BF16)

        lower = {}

        def finish(head, p):
            pair, hh = divmod(head, 2)
            cols = slice(pair * LANES, (pair + 1) * LANES)
            vh, keep, ones_lane = _half_values_with_ones(v_ref[key_rows, cols], hh == 1)
            o = _dot(p, vh)
            o = o * (1.0 / o[:, ones_lane:ones_lane + 1])
            if hh == 0:
                lower[pair] = o
            else:
                o_ref[query_rows, cols] = jnp.where(keep, o, lower.pop(pair)).astype(BF16)

        heads = range(2 * MLA_PAIRS_PER_STEP)
        if key_rows.start is not None:
            ps = [weights(s) for s in [logits(head) for head in heads]]
            for head in heads:
                finish(head, ps[head])
            return
        s_next = logits(0)
        for head in heads:
            s = s_next
            if head + 1 < 2 * MLA_PAIRS_PER_STEP:
                s_next = logits(head + 1)
            finish(head, weights(s))

    _wide_tile_branches(2, seq, attend)


def _mla_attn(st, q, k, v):
    slots = 2 * MLA_PAIRS_PER_STEP * LANES
    vals = MLA_PAIRS_PER_STEP * LANES
    kern = functools.partial(_mla_attn_kernel, seq=st.seq)
    return pl.pallas_call(
        kern,
        grid=(st.batch, MLA_HEADS // (2 * MLA_PAIRS_PER_STEP), st.attn_steps),
        in_specs=[pl.BlockSpec((None, ATTN_TILE, slots), lambda b, h, j: (b, j, h)),
                  pl.BlockSpec((None, st.tokens, slots), lambda b, h, j: (b, 0, h)),
                  pl.BlockSpec((None, st.tokens, vals), lambda b, h, j: (b, 0, h))],
        out_specs=pl.BlockSpec((None, ATTN_TILE, vals), lambda b, h, j: (b, j, h)),
        out_shape=jax.ShapeDtypeStruct((st.batch, st.tokens, D_MODEL), BF16),
        compiler_params=_params(3),
        name="mla_attn",
    )(q, k, v)


NA_QROWS = TOKEN_TILE // GRID_W
NA_WROWS = NA_QROWS + NA_KH
NA_PAIRS = NA_WROWS // 2
NA_BIAS_LO = -(NA_QROWS - 1) - (NA_WROWS - NA_QROWS - NA_KH // 2) + NA_KH - 1
NA_BIAS_N = (NA_WROWS - 2) + NA_KH - 1 - NA_BIAS_LO + 1


def _na_attn_kernel(q_ref, k_ref, v_ref, bias_ref, o_ref, cap_ref, *, rows, seq):
    r0 = pl.program_id(1) * NA_QROWS
    w0 = jnp.clip(r0 - NA_KH // 2, 0, rows - NA_WROWS)
    start = pl.multiple_of(w0 * GRID_W, GRID_W)
    n_loc = NA_WROWS * GRID_W
    qidx = lax.broadcasted_iota(jnp.int32, (TOKEN_TILE, n_loc), 0)
    kidx = lax.broadcasted_iota(jnp.int32, (TOKEN_TILE, n_loc), 1)
    q_row, q_col = r0 + qidx // GRID_W, qidx % GRID_W
    k_row, k_col = w0 + kidx // GRID_W, kidx % GRID_W
    row_start = jnp.clip(q_row - NA_KH // 2, 0, rows - NA_KH)
    col_start = jnp.clip(q_col - NA_KW // 2, 0, GRID_W - NA_KW)
    valid = ((k_row >= row_start) & (k_row < row_start + NA_KH)
             & (k_col >= col_start) & (k_col < col_start + NA_KW))
    cap_ref[...] = jnp.where(valid, jnp.inf, NEG_INF)
    base = w0 - r0 + NA_KH - 1 - NA_BIAS_LO

    def logits(head):
        pair, hh = divmod(head, 2)
        cols = slice(pair * LANES, (pair + 1) * LANES)
        q = q_ref[:, cols]
        qh = jnp.where(_lane_mask(hh * NA_DH, (hh + 1) * NA_DH), q, jnp.zeros_like(q))
        return _dot_t(qh, k_ref[pl.ds(start, n_loc), cols]), _dot_t(qh, k_ref[seq:, cols])

    s_next = logits(0)
    for pair in range(NA_HEADS // 2):
        cols = slice(pair * LANES, (pair + 1) * LANES)
        v_loc, v_ctx = v_ref[pl.ds(start, n_loc), cols], v_ref[seq:, cols]
        out = None
        for hh in range(2):
            head = 2 * pair + hh
            s_loc, s_ctx = s_next
            if head + 1 < NA_HEADS:
                s_next = logits(head + 1)
            bias = jnp.concatenate(
                [jnp.concatenate([bias_ref[head, base + 2 * p - a] for p in range(NA_PAIRS)], axis=1)
                 for a in range(NA_QROWS)], axis=0)
            s_loc = jnp.minimum(s_loc + bias, cap_ref[...])
            m = jnp.maximum(jnp.max(s_loc, axis=-1, keepdims=True), jnp.max(s_ctx, axis=-1, keepdims=True))
            p_loc, p_ctx = jnp.exp2(s_loc - m), jnp.exp2(s_ctx - m)
            vh_loc, keep, ones_lane = _half_values_with_ones(v_loc, hh == 1)
            vh_ctx, _, _ = _half_values_with_ones(v_ctx, hh == 1)
            o = _dot(p_loc.astype(BF16), vh_loc) + _dot(p_ctx.astype(BF16), vh_ctx)
            o = o * (1.0 / o[:, ones_lane:ones_lane + 1])
            out = o if out is None else jnp.where(keep, o, out)
        o_ref[:, cols] = out.astype(BF16)


def _na_bias_table(rpb):
    n_rel, n_col = 2 * NA_KH - 1, 2 * NA_KW - 1
    left = GRID_W - NA_KW
    v = jnp.pad(rpb * LOG2E, ((0, 0), (0, 0), (left, 2 * GRID_W - left - n_col)))
    v = jnp.tile(v, (1, 1, GRID_W))[:, :, :GRID_W * (2 * GRID_W - 1)]
    toep = v.reshape(NA_HEADS, n_rel, GRID_W, 2 * GRID_W - 1)[..., GRID_W - 1:]
    toep = jnp.pad(toep, ((0, 0), (-NA_BIAS_LO, NA_BIAS_LO + NA_BIAS_N + 1 - n_rel), (0, 0), (0, 0)))
    return jnp.concatenate([toep[:, :-1], toep[:, 1:]], axis=-1)


def _na_attn(st, q, k, v, bias_tab):
    rows = st.seq // GRID_W
    assert rows >= NA_WROWS and rows % NA_QROWS == 0
    kern = functools.partial(_na_attn_kernel, rows=rows, seq=st.seq)
    return pl.pallas_call(
        kern,
        grid=(st.batch, st.lat_tiles),
        in_specs=[st.tile_spec(D_MODEL),
                  pl.BlockSpec((None, st.tokens, D_MODEL), lambda b, j: (b, 0, 0)),
                  pl.BlockSpec((None, st.tokens, D_MODEL), lambda b, j: (b, 0, 0)),
                  _resident((NA_HEADS, NA_BIAS_N, GRID_W, 2 * GRID_W))],
        out_specs=st.tile_spec(D_MODEL),
        out_shape=jax.ShapeDtypeStruct((st.batch, st.seq, D_MODEL), BF16),
        scratch_shapes=[pltpu.VMEM((TOKEN_TILE, NA_WROWS * GRID_W), F32)],
        compiler_params=_params(2),
        name="na_attn",
    )(q, k, v, bias_tab)


def _slot_columns(w, width):
    kdim = w.shape[0]
    w = w.reshape(kdim, -1, width)
    return jnp.pad(w, ((0, 0), (0, 0), (0, LANES - width))).reshape(kdim, -1)


def _slot_gain(g):
    return jnp.pad(g, ((0, 0), (0, LANES - g.shape[1])))


def kernel(x, c, ctx, c_ctx, ada_w, ada_b, norm_g, ffn_w_in, ffn_w_out, pool_w, pool_b, pool_scale,
           diff_w_qkv, diff_qk_g, diff_lambda, diff_sub_g, diff_w_o,
           mla_w_dq, mla_q_g, mla_w_uq, mla_w_dkv, mla_kv_g, mla_w_ukv, mla_qk_g, mla_w_o,
           na_w_qkv, na_qk_g, na_rpb, na_w_o):
    assert DEPTH == N_MIXERS, "one layer per mixer: the context stream ends after the last softmax mixer's keys"
    batch, seq, _ = x.shape
    st = _Stream(batch, seq, ctx.shape[1])
    rows = -(-(batch + 1) // BF16_SUBLANES) * BF16_SUBLANES
    cc = jnp.zeros((rows, D_MODEL), F32).at[:batch].set(c).at[batch].set(c_ctx)
    mods = _adaln(cc, ada_w, ada_b)

    w_in = ffn_w_in.astype(BF16)
    w_out = ffn_w_out.astype(BF16)
    gains = norm_g.reshape(DEPTH, 3, 1, D_MODEL)

    for i in range(DEPTH):
        mod = mods[i]
        if i == 0:
            xs = _ffn_join(st, x, ctx, mod, gains[i, 0], w_in, w_out)
            xs = _pool(st, xs, mod, gains[i, 1], pool_w[0].astype(BF16), pool_b[0].reshape(1, D_MODEL),
                       pool_scale[0].reshape(1, D_MODEL))
            xs = _ffn(st, xs, mod, 2, gains[i, 2], w_in, w_out, i)
            continue
        xs = _ffn(st, xs, mod, 0, gains[i, 0], w_in, w_out, i)
        if i == 1:
            q, k, v = _qkv_pre(st, xs, mod, gains[i, 1], diff_w_qkv[0].astype(BF16), diff_qk_g[0], True)
            lam_init = 0.8 - 0.6 * math.exp(-0.3 * i)
            attn = _diff_attn(st, q, k, v, diff_lambda[0], diff_sub_g[0], lam_init)
            w_o = diff_w_o[0]
        elif i == 2:
            kpe_cols = jnp.pad(mla_w_dkv[0][:, MLA_KV_RANK:], ((0, 0), (MLA_NOPE, LANES - MLA_NOPE - MLA_ROPE)))
            w_down = jnp.concatenate([mla_w_dq[0], mla_w_dkv[0][:, :MLA_KV_RANK], kpe_cols], axis=1)
            ukv = mla_w_ukv[0].reshape(MLA_KV_RANK, MLA_HEADS, MLA_NOPE + MLA_V)
            w_ukv = jnp.concatenate([_slot_columns(ukv[:, :, :MLA_NOPE].reshape(MLA_KV_RANK, -1), MLA_NOPE),
                                     ukv[:, :, MLA_NOPE:].reshape(MLA_KV_RANK, -1)], axis=1)
            q, k, v = _mla_pre(st, xs, mod, gains[i, 1], w_down.astype(BF16),
                               mla_q_g[0].reshape(1, -1), mla_kv_g[0].reshape(1, -1),
                               _slot_columns(mla_w_uq[0], MLA_NOPE + MLA_ROPE).astype(BF16),
                               w_ukv.astype(BF16), _slot_gain(mla_qk_g[0]))
            attn = _mla_attn(st, q, k, v)
            w_o = mla_w_o[0]
        else:
            q, k, v = _qkv_pre(st, xs, mod, gains[i, 1], na_w_qkv[0].astype(BF16), na_qk_g[0], False)
            attn = _na_attn(st, q, k, v, _na_bias_table(na_rpb[0]))
            w_o = na_w_o[0]
        xs = _mixout_ffn(st, xs, attn, mod, gains[i, 2], w_o.astype(BF16), w_in, w_out, i,
                         latent_only=(i == DEPTH - 1))
    return xs
```
